```python
import jax, jax.numpy as jnp
from jax import lax
import numpy as np

D_MODEL = 1024
BATCH = 8
SEQ = 2048
DEPTH = 1

GRID_W = 64
CTX_LEN = 256
ATTN_WIDTH = 512
N_HEADS = 8
N_KV_HEADS = 2
HEAD_DIM = ATTN_WIDTH // N_HEADS
KV_REP = N_HEADS // N_KV_HEADS
KV_WIDTH = N_KV_HEADS * HEAD_DIM
CONV_WIDTH = D_MODEL - ATTN_WIDTH
CONV_HEADS = 8
CONV_K = 3
MIX_WIDTH = ATTN_WIDTH + CONV_WIDTH
IN_COLS = ATTN_WIDTH + 2 * KV_WIDTH + 3 * CONV_WIDTH
ROPE_THETA = 10000.0
ROPE_AXIS_DIM = HEAD_DIM // 2
ROPE_FREQS = ROPE_AXIS_DIM // 2
Q_BLOCK = 128
N_GROUPS = 4
EXPERTS_PER_GROUP = 8
N_EXPERTS = N_GROUPS * EXPERTS_PER_GROUP
TOP_K = 2
D_EXPERT = 768
MOE_BLOCK = 128
N_MOD = 6
EPS = 1e-6

kernel_name = "hybrid_gqa_shortconv_hmoe_dit_block"


def rms_norm(x, g):
    xf = x.astype(jnp.float32)
    y = xf * lax.rsqrt(jnp.mean(xf * xf, axis=-1, keepdims=True) + EPS)
    return (y * g.astype(jnp.float32)).astype(x.dtype)


def head_rms(x, g, n_heads):
    xs = x.reshape(*x.shape[:-1], n_heads, -1)
    return rms_norm(xs, g.reshape(n_heads, -1)).reshape(x.shape)


def modulation(cond, w_mod, b_mod):
    mod = jax.nn.silu(cond) @ w_mod + b_mod
    return jnp.split(mod[..., None, :], N_MOD, axis=-1)


def axial_rope_tables(n_tokens, dtype):
    rows = n_tokens // GRID_W
    row_idx = jnp.repeat(jnp.arange(rows, dtype=jnp.float32), GRID_W)
    col_idx = jnp.tile(jnp.arange(GRID_W, dtype=jnp.float32), rows)
    inv_freq = ROPE_THETA ** (-jnp.arange(0, ROPE_AXIS_DIM, 2, dtype=jnp.float32) / ROPE_AXIS_DIM)
    ang = jnp.stack([row_idx[:, None] * inv_freq, col_idx[:, None] * inv_freq], axis=1)
    return jnp.cos(ang).astype(dtype), jnp.sin(ang).astype(dtype)


def apply_axial_rope(x, cos, sin):
    xs = x.reshape(*x.shape[:-1], 2, 2, ROPE_FREQS)
    x1, x2 = xs[..., 0, :], xs[..., 1, :]
    c, s = cos[:, None], sin[:, None]
    return jnp.stack([x1 * c - x2 * s, x1 * s + x2 * c], axis=-2).reshape(x.shape)


def in_proj(h, w_in, q_g, k_g):
    z = h @ w_in
    b, s = h.shape[:2]
    cuts = [ATTN_WIDTH, ATTN_WIDTH + KV_WIDTH, ATTN_WIDTH + 2 * KV_WIDTH,
            ATTN_WIDTH + 2 * KV_WIDTH + CONV_WIDTH, ATTN_WIDTH + 2 * KV_WIDTH + 2 * CONV_WIDTH]
    q, k, v, gb, gc, u = jnp.split(z, cuts, axis=-1)
    q = rms_norm(q.reshape(b, s, N_HEADS, HEAD_DIM), q_g)
    k = rms_norm(k.reshape(b, s, N_KV_HEADS, HEAD_DIM), k_g)
    v = v.reshape(b, s, N_KV_HEADS, HEAD_DIM)
    return q, k, v, gb, gc, u


def attend(q, k, v):
    s = jnp.einsum('bqgrd,bkgd->bgrqk', q, k, preferred_element_type=jnp.float32) * (HEAD_DIM ** -0.5)
    p = jax.nn.softmax(s, axis=-1).astype(v.dtype)
    return jnp.einsum('bgrqk,bkgd->bqgrd', p, v)


def latent_attention(q, k_all, v_all):
    b, s = q.shape[:2]
    nb = s // Q_BLOCK
    qb = q.reshape(b, nb, Q_BLOCK, N_KV_HEADS, KV_REP, HEAD_DIM).transpose(1, 0, 2, 3, 4, 5)
    ob = lax.map(lambda qi: attend(qi, k_all, v_all), qb)
    return ob.transpose(1, 0, 2, 3, 4, 5).reshape(b, s, ATTN_WIDTH)


def short_conv(gb, gc, u, conv_w):
    v = gc * u
    vp = jnp.pad(v, ((0, 0), (1, 1), (0, 0)))
    y = conv_w[0] * vp[:, :-2] + conv_w[1] * vp[:, 1:-1] + conv_w[2] * vp[:, 2:]
    return gb * y


def merge_out(a, cv, attn_g, conv_g, w_out):
    return jnp.concatenate([head_rms(a, attn_g, N_HEADS), head_rms(cv, conv_g, CONV_HEADS)], axis=-1) @ w_out


def hier_moe(h, w_group, w_router, w_gate, w_up, w_down):
    d = h.shape[-1]
    ht = h.reshape(-1, d)
    n = ht.shape[0]
    g_prob = jax.nn.softmax((ht @ w_group).astype(jnp.float32), axis=-1)
    g_sel = jnp.argmax(g_prob, axis=-1)
    g_w = jnp.take_along_axis(g_prob, g_sel[:, None], axis=1)[:, 0]
    e_logits = (ht @ w_router).astype(jnp.float32).reshape(n, N_GROUPS, EXPERTS_PER_GROUP)
    e_logits = jnp.take_along_axis(e_logits, g_sel[:, None, None], axis=1)[:, 0]
    e_prob = jax.nn.softmax(e_logits, axis=-1)
    top_p, top_i = lax.top_k(e_prob, TOP_K)
    top_p = top_p / jnp.sum(top_p, axis=-1, keepdims=True)
    wts = (g_w[:, None] * top_p).reshape(-1)
    eid = (g_sel[:, None] * EXPERTS_PER_GROUP + top_i).reshape(-1).astype(jnp.int32)
    tok = jnp.repeat(jnp.arange(n, dtype=jnp.int32), TOP_K)
    m = eid.shape[0]
    order = jnp.argsort(eid)
    s_e, s_tok, s_w = eid[order], tok[order], wts[order]
    counts = jnp.bincount(eid, length=N_EXPERTS).astype(jnp.int32)
    starts = jnp.cumsum(counts) - counts
    padded = ((counts + MOE_BLOCK - 1) // MOE_BLOCK) * MOE_BLOCK
    pstarts = jnp.cumsum(padded) - padded
    pends = pstarts + padded
    dest = pstarts[s_e] + (jnp.arange(m, dtype=jnp.int32) - starts[s_e])
    p_rows = m + N_EXPERTS * MOE_BLOCK
    n_blk = p_rows // MOE_BLOCK
    xbuf = jnp.zeros((p_rows, d), h.dtype).at[dest].set(ht[s_tok])
    blk_start = jnp.arange(n_blk, dtype=jnp.int32) * MOE_BLOCK
    blk_e = jnp.minimum(jnp.searchsorted(pends, blk_start, side='right'), N_EXPERTS - 1)

    def expert_block(args):
        xb, e = args
        return (jax.nn.silu(xb @ w_gate[e]) * (xb @ w_up[e])) @ w_down[e]

    ybuf = lax.map(expert_block, (xbuf.reshape(n_blk, MOE_BLOCK, d), blk_e)).reshape(p_rows, d)
    y = ybuf[dest] * s_w[:, None].astype(h.dtype)
    return jax.ops.segment_sum(y, s_tok, num_segments=n).reshape(h.shape)


def trunk_layer(x, ctx, c, c_ctx, cos, sin, w_mod, b_mod, norm1_g, w_in, q_norm_g, k_norm_g,
                conv_w, attn_out_g, conv_out_g, w_out, norm2_g, w_group, w_router, w_gate, w_up,
                w_down, update_ctx):
    sh1, sc1, gt1, sh2, sc2, gt2 = modulation(c, w_mod, b_mod)
    csh1, csc1, cgt1, csh2, csc2, cgt2 = modulation(c_ctx, w_mod, b_mod)
    hx = rms_norm(x, norm1_g) * (1.0 + sc1) + sh1
    hc = rms_norm(ctx, norm1_g) * (1.0 + csc1) + csh1
    qx, kx, vx, bx, cx, ux = in_proj(hx, w_in, q_norm_g, k_norm_g)
    qc, kc, vc, bc, cc, uc = in_proj(hc, w_in, q_norm_g, k_norm_g)
    qx = apply_axial_rope(qx, cos, sin)
    kx = apply_axial_rope(kx, cos, sin)
    k_all = jnp.concatenate([kc, kx], axis=1)
    v_all = jnp.concatenate([vc, vx], axis=1)
    ax = latent_attention(qx, k_all, v_all)
    convx = short_conv(bx, cx, ux, conv_w)
    x_new = x + gt1 * merge_out(ax, convx, attn_out_g, conv_out_g, w_out)
    hx2 = rms_norm(x_new, norm2_g) * (1.0 + sc2) + sh2
    x_new = x_new + gt2 * hier_moe(hx2, w_group, w_router, w_gate, w_up, w_down)
    if update_ctx:
        b = ctx.shape[0]
        ac = attend(qc.reshape(b, -1, N_KV_HEADS, KV_REP, HEAD_DIM), kc, vc).reshape(b, -1, ATTN_WIDTH)
        convc = short_conv(bc, cc, uc, conv_w)
        ctx = ctx + cgt1 * merge_out(ac, convc, attn_out_g, conv_out_g, w_out)
        hc2 = rms_norm(ctx, norm2_g) * (1.0 + csc2) + csh2
        ctx = ctx + cgt2 * hier_moe(hc2, w_group, w_router, w_gate, w_up, w_down)
    return x_new, ctx


def setup_inputs(seed: int = 0) -> dict:
    key = jax.random.key(seed)
    ks = jax.random.split(key, 21)
    f32 = jnp.float32
    L, D = DEPTH, D_MODEL

    def nrm(k, shape, scale):
        return jax.random.normal(k, shape, f32) * scale

    return {
        "x": nrm(ks[0], (BATCH, SEQ, D), 1.0),
        "c": nrm(ks[1], (BATCH, D), 1.0),
        "ctx": nrm(ks[2], (BATCH, CTX_LEN, D), 1.0),
        "c_ctx": nrm(ks[3], (D,), 1.0),
        "w_mod": nrm(ks[4], (L, D, N_MOD * D), 0.5 * D ** -0.5),
        "b_mod": nrm(ks[5], (L, N_MOD * D), 0.02),
        "norm1_g": 1.0 + nrm(ks[6], (L, D), 0.02),
        "w_in": nrm(ks[7], (L, D, IN_COLS), D ** -0.5),
        "q_norm_g": 1.0 + nrm(ks[8], (L, HEAD_DIM), 0.02),
        "k_norm_g": 1.0 + nrm(ks[9], (L, HEAD_DIM), 0.02),
        "conv_w": nrm(ks[10], (L, CONV_K, CONV_WIDTH), CONV_K ** -0.5),
        "attn_out_g": 1.0 + nrm(ks[11], (L, ATTN_WIDTH), 0.02),
        "conv_out_g": 1.0 + nrm(ks[12], (L, CONV_WIDTH), 0.02),
        "w_out": nrm(ks[13], (L, MIX_WIDTH, D), MIX_WIDTH ** -0.5),
        "norm2_g": 1.0 + nrm(ks[14], (L, D), 0.02),
        "w_group": nrm(ks[15], (L, D, N_GROUPS), D ** -0.5),
        "w_router": nrm(ks[16], (L, D, N_EXPERTS), D ** -0.5),
        "w_gate": nrm(ks[17], (L, N_EXPERTS, D, D_EXPERT), D ** -0.5),
        "w_up": nrm(ks[18], (L, N_EXPERTS, D, D_EXPERT), D ** -0.5),
        "w_down": nrm(ks[19], (L, N_EXPERTS, D_EXPERT, D), D_EXPERT ** -0.5),
        "final_g": 1.0 + nrm(ks[20], (D,), 0.02),
    }


def reference(x, c, ctx, c_ctx, w_mod, b_mod, norm1_g, w_in, q_norm_g, k_norm_g, conv_w,
              attn_out_g, conv_out_g, w_out, norm2_g, w_group, w_router, w_gate, w_up, w_down,
              final_g):
    cos, sin = axial_rope_tables(x.shape[1], x.dtype)
    for i in range(DEPTH):
        x, ctx = trunk_layer(x, ctx, c, c_ctx, cos, sin, w_mod[i], b_mod[i], norm1_g[i], w_in[i],
                             q_norm_g[i], k_norm_g[i], conv_w[i], attn_out_g[i], conv_out_g[i],
                             w_out[i], norm2_g[i], w_group[i], w_router[i], w_gate[i], w_up[i],
                             w_down[i], update_ctx=(i < DEPTH - 1))
    return rms_norm(x, final_g)
```

```python
import functools

import jax
import jax.numpy as jnp
from jax import lax
from jax.experimental import pallas as pl
from jax.experimental.pallas import tpu as pltpu

F32 = jnp.float32
BF16 = jnp.bfloat16

D_MODEL = 1024
GRID_W = 64
ATTN_WIDTH = 512
N_HEADS = 8
N_KV_HEADS = 2
HEAD_DIM = 64
KV_REP = N_HEADS // N_KV_HEADS
KV_WIDTH = N_KV_HEADS * HEAD_DIM
CONV_WIDTH = 512
IN_COLS = ATTN_WIDTH + 2 * KV_WIDTH + 3 * CONV_WIDTH
ROPE_THETA = 10000.0
ROPE_AXIS_DIM = HEAD_DIM // 2
ROPE_FREQS = ROPE_AXIS_DIM // 2
N_GROUPS = 4
EXPERTS_PER_GROUP = 8
N_EXPERTS = N_GROUPS * EXPERTS_PER_GROUP
TOP_K = 2
D_EXPERT = 768
N_MOD = 6
EPS = 1e-6

LANES = 128
MOD_ROWS = 16
IN_ROWS = 256
CONV_PAD = 8
ATTN_TQ = 128
OUT_ROWS = 256
DISPATCH_ROWS = 512
MOE_ROWS = 256
COMBINE_ROWS = 256
VMEM_LIMIT = 56 * 1024 * 1024


def _params(semantics, vmem=None):
    return pltpu.CompilerParams(dimension_semantics=semantics,
                                vmem_limit_bytes=vmem if vmem else VMEM_LIMIT)


def _rms(x):
    return x * lax.rsqrt(jnp.mean(x * x, axis=-1, keepdims=True) + EPS)


def _mod_kernel(c_ref, w_ref, b_ref, o_ref):
    c = c_ref[...]
    s = c * jax.nn.sigmoid(c)
    o_ref[...] = jnp.dot(s.astype(BF16), w_ref[...].astype(BF16),
                         preferred_element_type=F32) + b_ref[...]


def _modulation(cond, w_mod, b_mod):
    d = cond.shape[1]
    n_out = w_mod.shape[1]
    tn = 1024
    return pl.pallas_call(
        _mod_kernel,
        grid=(n_out // tn,),
        in_specs=[pl.BlockSpec((MOD_ROWS, d), lambda j: (0, 0)),
                  pl.BlockSpec((d, tn), lambda j: (0, j)),
                  pl.BlockSpec((1, tn), lambda j: (0, j))],
        out_specs=pl.BlockSpec((MOD_ROWS, tn), lambda j: (0, j)),
        out_shape=jax.ShapeDtypeStruct((MOD_ROWS, n_out), F32),
        compiler_params=_params(("arbitrary",)),
        name="modulation",
    )(cond, w_mod, b_mod.reshape(1, n_out))


def _rope(x, cos, sin_signed, is_lo):
    partner = jnp.where(is_lo, pltpu.roll(x, LANES - ROPE_FREQS, 1), pltpu.roll(x, ROPE_FREQS, 1))
    return x * cos + partner * sin_signed


def _inproj_kernel(x_ref, ctx_ref, mod_ref, cmod_ref, n1g_ref, w_ref, qg_ref, kg_ref,
                   cos_ref, sin_ref, convw_ref, cog_ref, bd_ref,
                   q_ref, k_ref, v_ref, cv_ref, p_s, gb_s):
    seq = x_ref.shape[1]
    ctx_len = ctx_ref.shape[1]
    g1 = n1g_ref[...]
    bd = bd_ref[...]
    bd_kv = bd_ref[0:KV_WIDTH, 0:KV_WIDTH]
    kg = kg_ref[...]
    qg = qg_ref[...]

    def head_ms(z, m):
        return jnp.dot((z * z).astype(BF16), m, preferred_element_type=F32)

    cmod = cmod_ref[0]
    hc = _rms(ctx_ref[0]) * g1 * (1.0 + cmod[1:2]) + cmod[0:1]
    zc = jnp.dot(hc.astype(BF16), w_ref[:, ATTN_WIDTH:ATTN_WIDTH + 2 * KV_WIDTH],
                 preferred_element_type=F32)
    kc = zc[:, :KV_WIDTH]
    kc = kc * lax.rsqrt(head_ms(kc, bd_kv) + EPS) * kg
    vc = zc[:, KV_WIDTH:]
    for g in range(N_KV_HEADS):
        k_ref[0, g, 0:ctx_len, :] = kc[:, g * HEAD_DIM:(g + 1) * HEAD_DIM].astype(BF16)
        v_ref[0, g, 0:ctx_len, :] = vc[:, g * HEAD_DIM:(g + 1) * HEAD_DIM].astype(BF16)

    zeros = jnp.zeros((CONV_PAD, CONV_WIDTH), F32)
    p_s[0:CONV_PAD, :] = zeros
    p_s[CONV_PAD + seq:CONV_PAD + seq + CONV_PAD, :] = zeros

    mod = mod_ref[0]
    sh1 = mod[0:1]
    sc1 = mod[1:2]
    lane = lax.broadcasted_iota(jnp.int32, (IN_ROWS, LANES), 1)
    is_lo = (lane // ROPE_FREQS) % 2 == 0

    def proj_chunk(c, carry):
        r0 = pl.multiple_of(c * IN_ROWS, IN_ROWS)
        h = (_rms(x_ref[0, pl.ds(r0, IN_ROWS), :]) * g1 * (1.0 + sc1) + sh1).astype(BF16)
        cos = cos_ref[pl.ds(r0, IN_ROWS), :]
        sin = sin_ref[pl.ds(r0, IN_ROWS), :]
        zq = jnp.dot(h, w_ref[:, 0:ATTN_WIDTH], preferred_element_type=F32)
        qn = zq * lax.rsqrt(head_ms(zq, bd) + EPS) * qg
        for j in range(ATTN_WIDTH // LANES):
            blk = _rope(qn[:, j * LANES:(j + 1) * LANES], cos, sin, is_lo) * (HEAD_DIM ** -0.5)
            for hh in range(LANES // HEAD_DIM):
                head = j * (LANES // HEAD_DIM) + hh
                q_ref[0, head, pl.ds(r0, IN_ROWS), :] = (
                    blk[:, hh * HEAD_DIM:(hh + 1) * HEAD_DIM].astype(BF16))
        zkv = jnp.dot(h, w_ref[:, ATTN_WIDTH:ATTN_WIDTH + 2 * KV_WIDTH], preferred_element_type=F32)
        kx = zkv[:, :KV_WIDTH]
        kx = _rope(kx * lax.rsqrt(head_ms(kx, bd_kv) + EPS) * kg, cos, sin, is_lo)
        vx = zkv[:, KV_WIDTH:]
        for g in range(N_KV_HEADS):
            k_ref[0, g, pl.ds(ctx_len + r0, IN_ROWS), :] = kx[:, g * HEAD_DIM:(g + 1) * HEAD_DIM].astype(BF16)
            v_ref[0, g, pl.ds(ctx_len + r0, IN_ROWS), :] = vx[:, g * HEAD_DIM:(g + 1) * HEAD_DIM].astype(BF16)
        c0 = ATTN_WIDTH + 2 * KV_WIDTH
        gb_s[pl.ds(r0, IN_ROWS), :] = jnp.dot(h, w_ref[:, c0:c0 + CONV_WIDTH], preferred_element_type=F32)
        zc_ = jnp.dot(h, w_ref[:, c0 + CONV_WIDTH:c0 + 2 * CONV_WIDTH], preferred_element_type=F32)
        zu = jnp.dot(h, w_ref[:, c0 + 2 * CONV_WIDTH:c0 + 3 * CONV_WIDTH], preferred_element_type=F32)
        p_s[pl.ds(CONV_PAD + r0, IN_ROWS), :] = zc_ * zu
        return carry

    lax.fori_loop(0, seq // IN_ROWS, proj_chunk, 0)

    cw = convw_ref[...]
    cog = cog_ref[...]

    def conv_chunk(c, carry):
        r0 = pl.multiple_of(c * IN_ROWS, IN_ROWS)
        win = p_s[pl.ds(r0, IN_ROWS + 2 * CONV_PAD), :]
        n_win = IN_ROWS + 2 * CONV_PAD
        prev = pltpu.roll(win, 1, 0)[CONV_PAD:CONV_PAD + IN_ROWS]
        cur = win[CONV_PAD:CONV_PAD + IN_ROWS]
        nxt = pltpu.roll(win, n_win - 1, 0)[CONV_PAD:CONV_PAD + IN_ROWS]
        y = cw[0:1] * prev + cw[1:2] * cur + cw[2:3] * nxt
        cvv = gb_s[pl.ds(r0, IN_ROWS), :] * y
        cvn = cvv * lax.rsqrt(head_ms(cvv, bd) + EPS) * cog
        cv_ref[0, pl.ds(r0, IN_ROWS), :] = cvn.astype(BF16)
        return carry

    lax.fori_loop(0, seq // IN_ROWS, conv_chunk, 0)


def _in_projection(x, ctx, mod3, norm1_g, w_in_bf, qg_t, kg_t, cos_t, sin_t, conv_w, conv_out_g, bd):
    b, s, d = x.shape
    ctx_len = ctx.shape[1]
    n_keys = ctx_len + s
    const = lambda *shape: pl.BlockSpec(shape, lambda i: (0,) * len(shape))
    return pl.pallas_call(
        _inproj_kernel,
        grid=(b,),
        in_specs=[
            pl.BlockSpec((1, s, d), lambda i: (i, 0, 0)),
            pl.BlockSpec((1, ctx_len, d), lambda i: (i, 0, 0)),
            pl.BlockSpec((1, N_MOD, d), lambda i: (i, 0, 0)),
            pl.BlockSpec((1, N_MOD, d), lambda i: (b, 0, 0)),
            const(1, d),
            const(d, IN_COLS),
            const(1, ATTN_WIDTH),
            const(1, KV_WIDTH),
            const(s, LANES),
            const(s, LANES),
            const(3, CONV_WIDTH),
            const(1, CONV_WIDTH),
            const(ATTN_WIDTH, ATTN_WIDTH),
        ],
        out_specs=[
            pl.BlockSpec((1, N_HEADS, s, HEAD_DIM), lambda i: (i, 0, 0, 0)),
            pl.BlockSpec((1, N_KV_HEADS, n_keys, HEAD_DIM), lambda i: (i, 0, 0, 0)),
            pl.BlockSpec((1, N_KV_HEADS, n_keys, HEAD_DIM), lambda i: (i, 0, 0, 0)),
            pl.BlockSpec((1, s, CONV_WIDTH), lambda i: (i, 0, 0)),
        ],
        out_shape=[
            jax.ShapeDtypeStruct((b, N_HEADS, s, HEAD_DIM), BF16),
            jax.ShapeDtypeStruct((b, N_KV_HEADS, n_keys, HEAD_DIM), BF16),
            jax.ShapeDtypeStruct((b, N_KV_HEADS, n_keys, HEAD_DIM), BF16),
            jax.ShapeDtypeStruct((b, s, CONV_WIDTH), BF16),
        ],
        scratch_shapes=[pltpu.VMEM((s + 2 * CONV_PAD, CONV_WIDTH), F32),
                        pltpu.VMEM((s, CONV_WIDTH), F32)],
        compiler_params=_params(("arbitrary",)),
        name="in_projection",
    )(x, ctx, mod3, mod3, norm1_g, w_in_bf, qg_t, kg_t, cos_t, sin_t, conv_w, conv_out_g, bd)


def _attn_kernel(q_ref, k_ref, v_ref, g_ref, o_ref):
    tq = q_ref.shape[2]
    pieces = []
    for g in range(N_KV_HEADS):
        qg = jnp.concatenate([q_ref[0, g * KV_REP + r] for r in range(KV_REP)], axis=0)
        s = lax.dot_general(qg, k_ref[0, g], (((1,), (1,)), ((), ())), preferred_element_type=F32)
        m = jnp.max(s, axis=-1, keepdims=True)
        p = jnp.exp(s - m)
        l = jnp.sum(p, axis=-1, keepdims=True)
        o = jnp.dot(p.astype(BF16), v_ref[0, g], preferred_element_type=F32) / l
        o = _rms(o)
        for r in range(KV_REP):
            pieces.append(o[r * tq:(r + 1) * tq])
    o_ref[0] = (jnp.concatenate(pieces, axis=1) * g_ref[...]).astype(BF16)


def _attention(q, k, v, attn_out_g):
    b, _, s, _ = q.shape
    n_keys = k.shape[2]
    return pl.pallas_call(
        _attn_kernel,
        grid=(b, s // ATTN_TQ),
        in_specs=[
            pl.BlockSpec((1, N_HEADS, ATTN_TQ, HEAD_DIM), lambda i, j: (i, 0, j, 0)),
            pl.BlockSpec((1, N_KV_HEADS, n_keys, HEAD_DIM), lambda i, j: (i, 0, 0, 0)),
            pl.BlockSpec((1, N_KV_HEADS, n_keys, HEAD_DIM), lambda i, j: (i, 0, 0, 0)),
            pl.BlockSpec((1, ATTN_WIDTH), lambda i, j: (0, 0)),
        ],
        out_specs=pl.BlockSpec((1, ATTN_TQ, ATTN_WIDTH), lambda i, j: (i, j, 0)),
        out_shape=jax.ShapeDtypeStruct((b, s, ATTN_WIDTH), BF16),
        compiler_params=_params(("arbitrary", "arbitrary")),
        name="attention",
    )(q, k, v, attn_out_g)


def _outproj_kernel(a_ref, cv_ref, w_ref, x_ref, mod_ref, n2g_ref, wr_ref, tri_ref,
                    xn_ref, h2_ref, ri_ref, cnt_ref, carry_s):
    i = pl.program_id(0)

    @pl.when(i == 0)
    def _():
        carry_s[...] = jnp.zeros_like(carry_s)

    mod = mod_ref[0]
    merged = (jnp.dot(a_ref[...], w_ref[0:ATTN_WIDTH, :], preferred_element_type=F32)
              + jnp.dot(cv_ref[...], w_ref[ATTN_WIDTH:, :], preferred_element_type=F32))
    xn = x_ref[...] + mod[2:3] * merged
    xn_ref[...] = xn
    h2 = _rms(xn) * n2g_ref[...] * (1.0 + mod[4:5]) + mod[3:4]
    h2_ref[...] = h2

    logits = jnp.dot(h2.astype(BF16), wr_ref[...], preferred_element_type=F32)
    rows = logits.shape[0]
    li = lax.broadcasted_iota(jnp.int32, (rows, LANES), 1)
    neg = jnp.float32(-jnp.inf)

    gmask = (li >= N_EXPERTS) & (li < N_EXPERTS + N_GROUPS)
    lg = jnp.where(gmask, logits, neg)
    ge = jnp.exp(lg - jnp.max(lg, axis=-1, keepdims=True))
    g_prob = ge / jnp.sum(ge, axis=-1, keepdims=True)
    g_w = jnp.max(g_prob, axis=-1, keepdims=True)
    g_sel = jnp.min(jnp.where(gmask & (g_prob == g_w), li, 2 * LANES), axis=-1, keepdims=True) - N_EXPERTS

    emask = (li < N_EXPERTS) & (li // EXPERTS_PER_GROUP == g_sel)
    le = jnp.where(emask, logits, neg)
    ee = jnp.exp(le - jnp.max(le, axis=-1, keepdims=True))
    e_prob = ee / jnp.sum(ee, axis=-1, keepdims=True)
    p1 = jnp.max(jnp.where(emask, e_prob, -1.0), axis=-1, keepdims=True)
    i1 = jnp.min(jnp.where(emask & (e_prob == p1), li, 2 * LANES), axis=-1, keepdims=True)
    mask2 = emask & (li != i1)
    p2 = jnp.max(jnp.where(mask2, e_prob, -1.0), axis=-1, keepdims=True)
    i2 = jnp.min(jnp.where(mask2 & (e_prob == p2), li, 2 * LANES), axis=-1, keepdims=True)
    psum = p1 + p2
    w1 = g_w * (p1 / psum)
    w2 = g_w * (p2 / psum)

    hit1 = li == i1
    hit2 = li == i2
    onehot = jnp.where(hit1 | hit2, 1.0, 0.0)
    before = jnp.dot(tri_ref[...], onehot.astype(BF16), preferred_element_type=F32) + carry_s[...]
    r1 = jnp.sum(jnp.where(hit1, before, 0.0), axis=-1, keepdims=True)
    r2 = jnp.sum(jnp.where(hit2, before, 0.0), axis=-1, keepdims=True)
    carry_s[...] = carry_s[...] + jnp.sum(onehot, axis=0, keepdims=True)

    info = jnp.where(li == 0, i1.astype(F32),
           jnp.where(li == 1, i2.astype(F32),
           jnp.where(li == 2, r1,
           jnp.where(li == 3, r2,
           jnp.where(li == 4, w1,
           jnp.where(li == 5, w2, 0.0))))))
    ri_ref[...] = info
    cnt_ref[...] = carry_s[...]


def _out_projection(a, cv, w_out_bf, x2, mod3, norm2_g, wr_bf, tri, seq):
    n, d = x2.shape
    tiles_per_batch = seq // OUT_ROWS
    return pl.pallas_call(
        _outproj_kernel,
        grid=(n // OUT_ROWS,),
        in_specs=[
            pl.BlockSpec((OUT_ROWS, ATTN_WIDTH), lambda i: (i, 0)),
            pl.BlockSpec((OUT_ROWS, CONV_WIDTH), lambda i: (i, 0)),
            pl.BlockSpec((d, d), lambda i: (0, 0)),
            pl.BlockSpec((OUT_ROWS, d), lambda i: (i, 0)),
            pl.BlockSpec((1, N_MOD, d), lambda i: (i // tiles_per_batch, 0, 0)),
            pl.BlockSpec((1, d), lambda i: (0, 0)),
            pl.BlockSpec((d, LANES), lambda i: (0, 0)),
            pl.BlockSpec((OUT_ROWS, OUT_ROWS), lambda i: (0, 0)),
        ],
        out_specs=[
            pl.BlockSpec((OUT_ROWS, d), lambda i: (i, 0)),
            pl.BlockSpec((OUT_ROWS, d), lambda i: (i, 0)),
            pl.BlockSpec((OUT_ROWS, LANES), lambda i: (i, 0)),
            pl.BlockSpec((1, LANES), lambda i: (0, 0)),
        ],
        out_shape=[
            jax.ShapeDtypeStruct((n, d), F32),
            jax.ShapeDtypeStruct((n, d), F32),
            jax.ShapeDtypeStruct((n, LANES), F32),
            jax.ShapeDtypeStruct((1, LANES), F32),
        ],
        scratch_shapes=[pltpu.VMEM((1, LANES), F32)],
        compiler_params=_params(("arbitrary",)),
        name="out_projection_routing",
    )(a, cv, w_out_bf, x2, mod3, norm2_g, wr_bf, tri)


def _row_copy(src_hbm, dst_hbm, src_row, dst_row, sem):
    return pltpu.make_async_copy(src_hbm.at[pl.ds(src_row, 1)], dst_hbm.at[pl.ds(dst_row, 1)], sem)


def _dispatch_kernel(dest_ref, h_hbm, zero_hbm, xs_hbm, sem):
    del zero_hbm
    base = pl.program_id(0) * DISPATCH_ROWS

    def start(t, carry):
        for k in range(TOP_K):
            _row_copy(h_hbm, xs_hbm, base + t, dest_ref[0, 0, TOP_K * t + k], sem).start()
        return carry

    lax.fori_loop(0, DISPATCH_ROWS, start, 0)

    def drain(t, carry):
        for k in range(TOP_K):
            _row_copy(h_hbm, xs_hbm, 0, 0, sem).wait()
        return carry

    lax.fori_loop(0, DISPATCH_ROWS, drain, 0)


def _dispatch(dest3, h2, p_rows):
    n, d = h2.shape
    zeros = jnp.zeros((p_rows, d), h2.dtype)
    return pl.pallas_call(
        _dispatch_kernel,
        grid=(n // DISPATCH_ROWS,),
        in_specs=[
            pl.BlockSpec((1, 1, TOP_K * DISPATCH_ROWS), lambda i: (i, 0, 0), memory_space=pltpu.SMEM),
            pl.BlockSpec(memory_space=pl.ANY),
            pl.BlockSpec(memory_space=pl.ANY),
        ],
        out_specs=pl.BlockSpec(memory_space=pl.ANY),
        out_shape=jax.ShapeDtypeStruct((p_rows, d), h2.dtype),
        scratch_shapes=[pltpu.SemaphoreType.DMA(())],
        input_output_aliases={2: 0},
        compiler_params=_params(("arbitrary",)),
        name="moe_dispatch",
    )(dest3, h2, zeros)


def _experts_kernel(te_ref, nv_ref, xs_ref, wg_ref, wu_ref, wd_ref, y_ref):
    valid = pl.program_id(0) < nv_ref[0]

    @pl.when(jnp.logical_not(valid))
    def _():
        y_ref[...] = jnp.zeros_like(y_ref)

    @pl.when(valid)
    def _():
        x = xs_ref[...].astype(BF16)
        g = jnp.dot(x, wg_ref[0].astype(BF16), preferred_element_type=F32)
        u = jnp.dot(x, wu_ref[0].astype(BF16), preferred_element_type=F32)
        h = (g * jax.nn.sigmoid(g)) * u
        y_ref[...] = jnp.dot(h.astype(BF16), wd_ref[0].astype(BF16), preferred_element_type=F32)


def _experts(tile_expert, n_valid, xs, w_gate, w_up, w_down):
    p_rows, d = xs.shape
    n_tiles = p_rows // MOE_ROWS
    row_map = lambda i, te, nv: (jnp.minimum(i, nv[0] - 1), 0)
    w_map = lambda i, te, nv: (te[i], 0, 0)
    return pl.pallas_call(
        _experts_kernel,
        grid_spec=pltpu.PrefetchScalarGridSpec(
            num_scalar_prefetch=2,
            grid=(n_tiles,),
            in_specs=[
                pl.BlockSpec((MOE_ROWS, d), row_map),
                pl.BlockSpec((1, d, D_EXPERT), w_map),
                pl.BlockSpec((1, d, D_EXPERT), w_map),
                pl.BlockSpec((1, D_EXPERT, d), w_map),
            ],
            out_specs=pl.BlockSpec((MOE_ROWS, d), lambda i, te, nv: (i, 0)),
        ),
        out_shape=jax.ShapeDtypeStruct((p_rows, d), F32),
        compiler_params=_params(("arbitrary",)),
        name="moe_experts",
    )(tile_expert, n_valid, xs, w_gate, w_up, w_down)


def _combine_kernel(dest_ref, ri_ref, xn_ref, mod_ref, fg_ref, y_hbm, o_ref, buf, sem):
    def start(t, carry):
        for k in range(TOP_K):
            pltpu.make_async_copy(y_hbm.at[pl.ds(dest_ref[0, 0, TOP_K * t + k], 1)],
                                  buf.at[k, pl.ds(t, 1)], sem).start()
        return carry

    lax.fori_loop(0, COMBINE_ROWS, start, 0)

    def drain(t, carry):
        for k in range(TOP_K):
            pltpu.make_async_copy(y_hbm.at[pl.ds(0, 1)], buf.at[k, pl.ds(0, 1)], sem).wait()
        return carry

    lax.fori_loop(0, COMBINE_ROWS, drain, 0)

    ri = ri_ref[...]
    y = buf[0] * ri[:, 4:5] + buf[1] * ri[:, 5:6]
    xf = xn_ref[...] + mod_ref[0][5:6] * y
    o_ref[...] = _rms(xf) * fg_ref[...]


def _combine(dest3, rinfo, x_new, mod3, final_g, ybuf, seq):
    n, d = x_new.shape
    tiles_per_batch = seq // COMBINE_ROWS
    return pl.pallas_call(
        _combine_kernel,
        grid=(n // COMBINE_ROWS,),
        in_specs=[
            pl.BlockSpec((1, 1, TOP_K * COMBINE_ROWS), lambda i: (i, 0, 0), memory_space=pltpu.SMEM),
            pl.BlockSpec((COMBINE_ROWS, LANES), lambda i: (i, 0)),
            pl.BlockSpec((COMBINE_ROWS, d), lambda i: (i, 0)),
            pl.BlockSpec((1, N_MOD, d), lambda i: (i // tiles_per_batch, 0, 0)),
            pl.BlockSpec((1, d), lambda i: (0, 0)),
            pl.BlockSpec(memory_space=pl.ANY),
        ],
        out_specs=pl.BlockSpec((COMBINE_ROWS, d), lambda i: (i, 0)),
        out_shape=jax.ShapeDtypeStruct((n, d), F32),
        scratch_shapes=[pltpu.VMEM((TOP_K, COMBINE_ROWS, d), F32), pltpu.SemaphoreType.DMA(())],
        compiler_params=_params(("arbitrary",)),
        name="moe_combine",
    )(dest3, rinfo, x_new, mod3, final_g, ybuf)


def _rope_tables(seq):
    rows = seq // GRID_W
    row_idx = jnp.repeat(jnp.arange(rows, dtype=F32), GRID_W)
    col_idx = jnp.tile(jnp.arange(GRID_W, dtype=F32), rows)
    inv_freq = ROPE_THETA ** (-jnp.arange(0, ROPE_AXIS_DIM, 2, dtype=F32) / ROPE_AXIS_DIM)
    ang = jnp.stack([row_idx[:, None] * inv_freq, col_idx[:, None] * inv_freq], axis=1)
    cos = jnp.cos(ang)
    sin = jnp.sin(ang)
    cos_h = jnp.stack([cos, cos], axis=2).reshape(seq, HEAD_DIM)
    sin_h = jnp.stack([-sin, sin], axis=2).reshape(seq, HEAD_DIM)
    reps = LANES // HEAD_DIM
    return jnp.tile(cos_h, (1, reps)), jnp.tile(sin_h, (1, reps))


def kernel(x, c, ctx, c_ctx, w_mod, b_mod, norm1_g, w_in, q_norm_g, k_norm_g, conv_w, attn_out_g,
           conv_out_g, w_out, norm2_g, w_group, w_router, w_gate, w_up, w_down, final_g):
    assert w_mod.shape[0] == 1, "single-layer block"
    b, s, d = x.shape
    n = b * s
    assert b + 1 <= MOD_ROWS

    cond = jnp.zeros((MOD_ROWS, d), F32).at[:b].set(c).at[b].set(c_ctx)
    mod3 = _modulation(cond, w_mod[0], b_mod[0]).reshape(MOD_ROWS, N_MOD, d)

    cos_t, sin_t = _rope_tables(s)
    head_of = jnp.arange(ATTN_WIDTH) // HEAD_DIM
    bd = jnp.where(head_of[:, None] == head_of[None, :], 1.0 / HEAD_DIM, 0.0).astype(BF16)
    q, k, v, cv = _in_projection(
        x, ctx, mod3, norm1_g, w_in[0].astype(BF16),
        jnp.tile(q_norm_g[0], N_HEADS)[None], jnp.tile(k_norm_g[0], N_KV_HEADS)[None],
        cos_t, sin_t, conv_w[0], conv_out_g, bd)

    a = _attention(q, k, v, attn_out_g)

    wr = jnp.zeros((d, LANES), F32).at[:, :N_EXPERTS].set(w_router[0])
    wr = wr.at[:, N_EXPERTS:N_EXPERTS + N_GROUPS].set(w_group[0]).astype(BF16)
    ti = jnp.arange(OUT_ROWS)
    tri = (ti[None, :] < ti[:, None]).astype(BF16)
    x_new, h2, rinfo, counts = _out_projection(
        a.reshape(n, ATTN_WIDTH), cv.reshape(n, CONV_WIDTH), w_out[0].astype(BF16),
        x.reshape(n, d), mod3, norm2_g, wr, tri, s)

    cnt = counts[0, :N_EXPERTS].astype(jnp.int32)
    padded = ((cnt + MOE_ROWS - 1) // MOE_ROWS) * MOE_ROWS
    pends = jnp.cumsum(padded)
    pstarts = pends - padded
    eid = rinfo[:, 0:TOP_K].astype(jnp.int32)
    rank = rinfo[:, TOP_K:2 * TOP_K].astype(jnp.int32)
    dest = pstarts[eid] + rank
    p_rows = n * TOP_K + N_EXPERTS * MOE_ROWS
    n_tiles = p_rows // MOE_ROWS
    n_valid = (pends[-1] // MOE_ROWS).astype(jnp.int32)
    tile_start = jnp.arange(n_tiles, dtype=jnp.int32) * MOE_ROWS
    tile_expert = jnp.minimum(jnp.searchsorted(pends, tile_start, side='right'), N_EXPERTS - 1)
    tile_expert = jnp.where(tile_start < pends[-1], tile_expert,
                            tile_expert[jnp.maximum(n_valid - 1, 0)]).astype(jnp.int32)

    xs = _dispatch(dest.reshape(n // DISPATCH_ROWS, 1, TOP_K * DISPATCH_ROWS), h2, p_rows)
    ybuf = _experts(tile_expert, n_valid.reshape(1), xs, w_gate[0], w_up[0], w_down[0])
    out = _combine(dest.reshape(n // COMBINE_ROWS, 1, TOP_K * COMBINE_ROWS), rinfo, x_new, mod3,
                   final_g.reshape(1, d), ybuf, s)
    return out.reshape(b, s, d)
```

```python
import functools

import jax
import jax.numpy as jnp
from jax import lax
from jax.experimental import pallas as pl
from jax.experimental.pallas import tpu as pltpu

F32 = jnp.float32
BF16 = jnp.bfloat16

D_MODEL = 1024
GRID_W = 64
ATTN_WIDTH = 512
N_HEADS = 8
N_KV_HEADS = 2
HEAD_DIM = 64
KV_REP = N_HEADS // N_KV_HEADS
KV_WIDTH = N_KV_HEADS * HEAD_DIM
CONV_WIDTH = 512
IN_COLS = ATTN_WIDTH + 2 * KV_WIDTH + 3 * CONV_WIDTH
ROPE_THETA = 10000.0
ROPE_AXIS_DIM = HEAD_DIM // 2
ROPE_FREQS = ROPE_AXIS_DIM // 2
N_GROUPS = 4
EXPERTS_PER_GROUP = 8
N_EXPERTS = N_GROUPS * EXPERTS_PER_GROUP
TOP_K = 2
D_EXPERT = 768
N_MOD = 6
EPS = 1e-6

LANES = 128
ROW_TILE = 8
MOD_ROWS = 16
IN_ROWS = 256
CONV_PAD = 8
ATTN_TQ = 128
OUT_ROWS = 256
DISPATCH_ROWS = 1024
DMA_UNROLL = 8
MOE_ROWS = 256
COMBINE_ROWS = 256
VMEM_LIMIT = 56 * 1024 * 1024


def _params(semantics, vmem=None):
    return pltpu.CompilerParams(dimension_semantics=semantics,
                                vmem_limit_bytes=vmem if vmem else VMEM_LIMIT)


def _rms(x):
    return x * lax.rsqrt(jnp.mean(x * x, axis=-1, keepdims=True) + EPS)


def _store_row_tiles(ref, val):
    rows, width = val.shape
    assert width == ROW_TILE * LANES
    for i in range(ROW_TILE):
        ref[pl.ds(i, rows, stride=ROW_TILE), :] = val[:, i * LANES:(i + 1) * LANES]


def _load_row_tiles(ref, rows, *lead):
    return jnp.concatenate(
        [ref[(*lead, pl.ds(i, rows, stride=ROW_TILE), slice(None))] for i in range(ROW_TILE)], axis=1)


def _row_tile(ref, row, *lead):
    return ref.at[(*lead, pl.ds(pl.multiple_of(row * ROW_TILE, ROW_TILE), ROW_TILE))]


def _mod_kernel(c_ref, w_ref, b_ref, o_ref):
    c = c_ref[...]
    s = c * jax.nn.sigmoid(c)
    o_ref[...] = jnp.dot(s.astype(BF16), w_ref[...].astype(BF16),
                         preferred_element_type=F32) + b_ref[...]


def _modulation(cond, w_mod, b_mod):
    d = cond.shape[1]
    n_out = w_mod.shape[1]
    tn = 1024
    return pl.pallas_call(
        _mod_kernel,
        grid=(n_out // tn,),
        in_specs=[pl.BlockSpec((MOD_ROWS, d), lambda j: (0, 0)),
                  pl.BlockSpec((d, tn), lambda j: (0, j)),
                  pl.BlockSpec((1, tn), lambda j: (0, j))],
        out_specs=pl.BlockSpec((MOD_ROWS, tn), lambda j: (0, j)),
        out_shape=jax.ShapeDtypeStruct((MOD_ROWS, n_out), F32),
        compiler_params=_params(("arbitrary",)),
        name="modulation",
    )(cond, w_mod, b_mod.reshape(1, n_out))


def _rope(x, cos, sin_signed, is_lo):
    partner = jnp.where(is_lo, pltpu.roll(x, LANES - ROPE_FREQS, 1), pltpu.roll(x, ROPE_FREQS, 1))
    return x * cos + partner * sin_signed


def _inproj_kernel(x_ref, ctx_ref, mod_ref, cmod_ref, n1g_ref, w_ref, qg_ref, kg_ref,
                   cos_ref, sin_ref, convw_ref, cog_ref, bd_ref,
                   q_ref, k_ref, v_ref, cv_ref, p_s, gb_s):
    seq = x_ref.shape[1]
    ctx_len = ctx_ref.shape[1]
    g1 = n1g_ref[...]
    bd = bd_ref[...]
    bd_kv = bd_ref[0:KV_WIDTH, 0:KV_WIDTH]
    kg = kg_ref[...]
    qg = qg_ref[...]

    def head_ms(z, m):
        return jnp.dot((z * z).astype(BF16), m, preferred_element_type=F32)

    cmod = cmod_ref[0]
    hc = _rms(ctx_ref[0]) * g1 * (1.0 + cmod[1:2]) + cmod[0:1]
    zc = jnp.dot(hc.astype(BF16), w_ref[:, ATTN_WIDTH:ATTN_WIDTH + 2 * KV_WIDTH],
                 preferred_element_type=F32)
    kc = zc[:, :KV_WIDTH]
    kc = kc * lax.rsqrt(head_ms(kc, bd_kv) + EPS) * kg
    vc = zc[:, KV_WIDTH:]
    for g in range(N_KV_HEADS):
        k_ref[0, g, 0:ctx_len, :] = kc[:, g * HEAD_DIM:(g + 1) * HEAD_DIM].astype(BF16)
        v_ref[0, g, 0:ctx_len, :] = vc[:, g * HEAD_DIM:(g + 1) * HEAD_DIM].astype(BF16)

    zeros = jnp.zeros((CONV_PAD, CONV_WIDTH), F32)
    p_s[0:CONV_PAD, :] = zeros
    p_s[CONV_PAD + seq:CONV_PAD + seq + CONV_PAD, :] = zeros

    mod = mod_ref[0]
    sh1 = mod[0:1]
    sc1 = mod[1:2]
    lane = lax.broadcasted_iota(jnp.int32, (IN_ROWS, LANES), 1)
    is_lo = (lane // ROPE_FREQS) % 2 == 0

    def proj_chunk(c, carry):
        r0 = pl.multiple_of(c * IN_ROWS, IN_ROWS)
        h = (_rms(x_ref[0, pl.ds(r0, IN_ROWS), :]) * g1 * (1.0 + sc1) + sh1).astype(BF16)
        cos = cos_ref[pl.ds(r0, IN_ROWS), :]
        sin = sin_ref[pl.ds(r0, IN_ROWS), :]
        zq = jnp.dot(h, w_ref[:, 0:ATTN_WIDTH], preferred_element_type=F32)
        qn = zq * lax.rsqrt(head_ms(zq, bd) + EPS) * qg
        for j in range(ATTN_WIDTH // LANES):
            blk = _rope(qn[:, j * LANES:(j + 1) * LANES], cos, sin, is_lo) * (HEAD_DIM ** -0.5)
            for hh in range(LANES // HEAD_DIM):
                head = j * (LANES // HEAD_DIM) + hh
                q_ref[0, head, pl.ds(r0, IN_ROWS), :] = (
                    blk[:, hh * HEAD_DIM:(hh + 1) * HEAD_DIM].astype(BF16))
        zkv = jnp.dot(h, w_ref[:, ATTN_WIDTH:ATTN_WIDTH + 2 * KV_WIDTH], preferred_element_type=F32)
        kx = zkv[:, :KV_WIDTH]
        kx = _rope(kx * lax.rsqrt(head_ms(kx, bd_kv) + EPS) * kg, cos, sin, is_lo)
        vx = zkv[:, KV_WIDTH:]
        for g in range(N_KV_HEADS):
            k_ref[0, g, pl.ds(ctx_len + r0, IN_ROWS), :] = kx[:, g * HEAD_DIM:(g + 1) * HEAD_DIM].astype(BF16)
            v_ref[0, g, pl.ds(ctx_len + r0, IN_ROWS), :] = vx[:, g * HEAD_DIM:(g + 1) * HEAD_DIM].astype(BF16)
        c0 = ATTN_WIDTH + 2 * KV_WIDTH
        gb_s[pl.ds(r0, IN_ROWS), :] = jnp.dot(h, w_ref[:, c0:c0 + CONV_WIDTH], preferred_element_type=F32)
        zc_ = jnp.dot(h, w_ref[:, c0 + CONV_WIDTH:c0 + 2 * CONV_WIDTH], preferred_element_type=F32)
        zu = jnp.dot(h, w_ref[:, c0 + 2 * CONV_WIDTH:c0 + 3 * CONV_WIDTH], preferred_element_type=F32)
        p_s[pl.ds(CONV_PAD + r0, IN_ROWS), :] = zc_ * zu
        return carry

    lax.fori_loop(0, seq // IN_ROWS, proj_chunk, 0)

    cw = convw_ref[...]
    cog = cog_ref[...]

    def conv_chunk(c, carry):
        r0 = pl.multiple_of(c * IN_ROWS, IN_ROWS)
        win = p_s[pl.ds(r0, IN_ROWS + 2 * CONV_PAD), :]
        n_win = IN_ROWS + 2 * CONV_PAD
        prev = pltpu.roll(win, 1, 0)[CONV_PAD:CONV_PAD + IN_ROWS]
        cur = win[CONV_PAD:CONV_PAD + IN_ROWS]
        nxt = pltpu.roll(win, n_win - 1, 0)[CONV_PAD:CONV_PAD + IN_ROWS]
        y = cw[0:1] * prev + cw[1:2] * cur + cw[2:3] * nxt
        cvv = gb_s[pl.ds(r0, IN_ROWS), :] * y
        cvn = cvv * lax.rsqrt(head_ms(cvv, bd) + EPS) * cog
        cv_ref[0, pl.ds(r0, IN_ROWS), :] = cvn.astype(BF16)
        return carry

    lax.fori_loop(0, seq // IN_ROWS, conv_chunk, 0)


def _in_projection(x, ctx, mod3, norm1_g, w_in_bf, qg_t, kg_t, cos_t, sin_t, conv_w, conv_out_g, bd):
    b, s, d = x.shape
    ctx_len = ctx.shape[1]
    n_keys = ctx_len + s
    const = lambda *shape: pl.BlockSpec(shape, lambda i: (0,) * len(shape))
    return pl.pallas_call(
        _inproj_kernel,
        grid=(b,),
        in_specs=[
            pl.BlockSpec((1, s, d), lambda i: (i, 0, 0)),
            pl.BlockSpec((1, ctx_len, d), lambda i: (i, 0, 0)),
            pl.BlockSpec((1, N_MOD, d), lambda i: (i, 0, 0)),
            pl.BlockSpec((1, N_MOD, d), lambda i: (b, 0, 0)),
            const(1, d),
            const(d, IN_COLS),
            const(1, ATTN_WIDTH),
            const(1, KV_WIDTH),
            const(s, LANES),
            const(s, LANES),
            const(3, CONV_WIDTH),
            const(1, CONV_WIDTH),
            const(ATTN_WIDTH, ATTN_WIDTH),
        ],
        out_specs=[
            pl.BlockSpec((1, N_HEADS, s, HEAD_DIM), lambda i: (i, 0, 0, 0)),
            pl.BlockSpec((1, N_KV_HEADS, n_keys, HEAD_DIM), lambda i: (i, 0, 0, 0)),
            pl.BlockSpec((1, N_KV_HEADS, n_keys, HEAD_DIM), lambda i: (i, 0, 0, 0)),
            pl.BlockSpec((1, s, CONV_WIDTH), lambda i: (i, 0, 0)),
        ],
        out_shape=[
            jax.ShapeDtypeStruct((b, N_HEADS, s, HEAD_DIM), BF16),
            jax.ShapeDtypeStruct((b, N_KV_HEADS, n_keys, HEAD_DIM), BF16),
            jax.ShapeDtypeStruct((b, N_KV_HEADS, n_keys, HEAD_DIM), BF16),
            jax.ShapeDtypeStruct((b, s, CONV_WIDTH), BF16),
        ],
        scratch_shapes=[pltpu.VMEM((s + 2 * CONV_PAD, CONV_WIDTH), F32),
                        pltpu.VMEM((s, CONV_WIDTH), F32)],
        compiler_params=_params(("arbitrary",)),
        name="in_projection",
    )(x, ctx, mod3, mod3, norm1_g, w_in_bf, qg_t, kg_t, cos_t, sin_t, conv_w, conv_out_g, bd)


def _attn_kernel(q_ref, k_ref, v_ref, g_ref, o_ref):
    tq = q_ref.shape[2]
    pieces = []
    for g in range(N_KV_HEADS):
        qg = jnp.concatenate([q_ref[0, g * KV_REP + r] for r in range(KV_REP)], axis=0)
        s = lax.dot_general(qg, k_ref[0, g], (((1,), (1,)), ((), ())), preferred_element_type=F32)
        m = jnp.max(s, axis=-1, keepdims=True)
        p = jnp.exp(s - m)
        l = jnp.sum(p, axis=-1, keepdims=True)
        o = jnp.dot(p.astype(BF16), v_ref[0, g], preferred_element_type=F32) / l
        o = _rms(o)
        for r in range(KV_REP):
            pieces.append(o[r * tq:(r + 1) * tq])
    o_ref[0] = (jnp.concatenate(pieces, axis=1) * g_ref[...]).astype(BF16)


def _attention(q, k, v, attn_out_g):
    b, _, s, _ = q.shape
    n_keys = k.shape[2]
    return pl.pallas_call(
        _attn_kernel,
        grid=(b, s // ATTN_TQ),
        in_specs=[
            pl.BlockSpec((1, N_HEADS, ATTN_TQ, HEAD_DIM), lambda i, j: (i, 0, j, 0)),
            pl.BlockSpec((1, N_KV_HEADS, n_keys, HEAD_DIM), lambda i, j: (i, 0, 0, 0)),
            pl.BlockSpec((1, N_KV_HEADS, n_keys, HEAD_DIM), lambda i, j: (i, 0, 0, 0)),
            pl.BlockSpec((1, ATTN_WIDTH), lambda i, j: (0, 0)),
        ],
        out_specs=pl.BlockSpec((1, ATTN_TQ, ATTN_WIDTH), lambda i, j: (i, j, 0)),
        out_shape=jax.ShapeDtypeStruct((b, s, ATTN_WIDTH), BF16),
        compiler_params=_params(("arbitrary", "arbitrary")),
        name="attention",
    )(q, k, v, attn_out_g)


def _outproj_kernel(a_ref, cv_ref, w_ref, x_ref, mod_ref, n2g_ref, wr_ref, tri_ref,
                    xn_ref, h2_ref, ri_ref, rit_ref, cnt_ref, carry_s):
    i = pl.program_id(0)

    @pl.when(i == 0)
    def _():
        carry_s[...] = jnp.zeros_like(carry_s)

    mod = mod_ref[0]
    merged = (jnp.dot(a_ref[...], w_ref[0:ATTN_WIDTH, :], preferred_element_type=F32)
              + jnp.dot(cv_ref[...], w_ref[ATTN_WIDTH:, :], preferred_element_type=F32))
    xn = x_ref[...] + mod[2:3] * merged
    xn_ref[...] = xn
    h2 = _rms(xn) * n2g_ref[...] * (1.0 + mod[4:5]) + mod[3:4]
    _store_row_tiles(h2_ref, h2)

    logits = jnp.dot(h2.astype(BF16), wr_ref[...], preferred_element_type=F32)
    rows = logits.shape[0]
    li = lax.broadcasted_iota(jnp.int32, (rows, LANES), 1)
    neg = jnp.float32(-jnp.inf)

    gmask = (li >= N_EXPERTS) & (li < N_EXPERTS + N_GROUPS)
    lg = jnp.where(gmask, logits, neg)
    ge = jnp.exp(lg - jnp.max(lg, axis=-1, keepdims=True))
    g_prob = ge / jnp.sum(ge, axis=-1, keepdims=True)
    g_w = jnp.max(g_prob, axis=-1, keepdims=True)
    g_sel = jnp.min(jnp.where(gmask & (g_prob == g_w), li, 2 * LANES), axis=-1, keepdims=True) - N_EXPERTS

    emask = (li < N_EXPERTS) & (li // EXPERTS_PER_GROUP == g_sel)
    le = jnp.where(emask, logits, neg)
    ee = jnp.exp(le - jnp.max(le, axis=-1, keepdims=True))
    e_prob = ee / jnp.sum(ee, axis=-1, keepdims=True)
    p1 = jnp.max(jnp.where(emask, e_prob, -1.0), axis=-1, keepdims=True)
    i1 = jnp.min(jnp.where(emask & (e_prob == p1), li, 2 * LANES), axis=-1, keepdims=True)
    mask2 = emask & (li != i1)
    p2 = jnp.max(jnp.where(mask2, e_prob, -1.0), axis=-1, keepdims=True)
    i2 = jnp.min(jnp.where(mask2 & (e_prob == p2), li, 2 * LANES), axis=-1, keepdims=True)
    psum = p1 + p2
    w1 = g_w * (p1 / psum)
    w2 = g_w * (p2 / psum)

    hit1 = li == i1
    hit2 = li == i2
    onehot = jnp.where(hit1 | hit2, 1.0, 0.0)
    before = jnp.dot(tri_ref[...], onehot.astype(BF16), preferred_element_type=F32) + carry_s[...]
    r1 = jnp.sum(jnp.where(hit1, before, 0.0), axis=-1, keepdims=True)
    r2 = jnp.sum(jnp.where(hit2, before, 0.0), axis=-1, keepdims=True)
    carry_s[...] = carry_s[...] + jnp.sum(onehot, axis=0, keepdims=True)

    info = jnp.where(li == 0, i1.astype(F32),
           jnp.where(li == 1, i2.astype(F32),
           jnp.where(li == 2, r1,
           jnp.where(li == 3, r2,
           jnp.where(li == 4, w1,
           jnp.where(li == 5, w2, 0.0))))))
    ri_ref[...] = info
    rit_ref[...] = jnp.transpose(info)[0:ROW_TILE, :]
    cnt_ref[...] = carry_s[...]


def _out_projection(a, cv, w_out_bf, x2, mod3, norm2_g, wr_bf, tri, seq):
    n, d = x2.shape
    tiles_per_batch = seq // OUT_ROWS
    return pl.pallas_call(
        _outproj_kernel,
        grid=(n // OUT_ROWS,),
        in_specs=[
            pl.BlockSpec((OUT_ROWS, ATTN_WIDTH), lambda i: (i, 0)),
            pl.BlockSpec((OUT_ROWS, CONV_WIDTH), lambda i: (i, 0)),
            pl.BlockSpec((d, d), lambda i: (0, 0)),
            pl.BlockSpec((OUT_ROWS, d), lambda i: (i, 0)),
            pl.BlockSpec((1, N_MOD, d), lambda i: (i // tiles_per_batch, 0, 0)),
            pl.BlockSpec((1, d), lambda i: (0, 0)),
            pl.BlockSpec((d, LANES), lambda i: (0, 0)),
            pl.BlockSpec((OUT_ROWS, OUT_ROWS), lambda i: (0, 0)),
        ],
        out_specs=[
            pl.BlockSpec((OUT_ROWS, d), lambda i: (i, 0)),
            pl.BlockSpec((OUT_ROWS * ROW_TILE, LANES), lambda i: (i, 0)),
            pl.BlockSpec((OUT_ROWS, LANES), lambda i: (i, 0)),
            pl.BlockSpec((ROW_TILE, OUT_ROWS), lambda i: (0, i)),
            pl.BlockSpec((1, LANES), lambda i: (0, 0)),
        ],
        out_shape=[
            jax.ShapeDtypeStruct((n, d), F32),
            jax.ShapeDtypeStruct((n * ROW_TILE, LANES), F32),
            jax.ShapeDtypeStruct((n, LANES), F32),
            jax.ShapeDtypeStruct((ROW_TILE, n), F32),
            jax.ShapeDtypeStruct((1, LANES), F32),
        ],
        scratch_shapes=[pltpu.VMEM((1, LANES), F32)],
        compiler_params=_params(("arbitrary",)),
        name="out_projection_routing",
    )(a, cv, w_out_bf, x2, mod3, norm2_g, wr_bf, tri)


def _dispatch_kernel(pend_ref, padded_ref, d0_ref, d1_ref, h_ref, xs_hbm, zero_s, sem):
    tile = MOE_ROWS * ROW_TILE

    @pl.when(pl.program_id(0) == 0)
    def _():
        zero_s[...] = jnp.zeros_like(zero_s)

        def last_tile(e):
            start = pl.multiple_of((pend_ref[e] - MOE_ROWS) * ROW_TILE, tile)
            return pltpu.make_async_copy(zero_s, xs_hbm.at[pl.ds(start, tile)], sem)

        def spare_tile(j):
            return pltpu.make_async_copy(zero_s, xs_hbm.at[pl.ds(j * tile, tile)], sem)

        n_tiles = xs_hbm.shape[0] // tile
        used = pend_ref[N_EXPERTS - 1] // MOE_ROWS
        for e in range(N_EXPERTS):
            @pl.when(padded_ref[e] > 0)
            def _():
                last_tile(e).start()
        for j in range(n_tiles - N_EXPERTS, n_tiles):
            @pl.when(j >= used)
            def _():
                spare_tile(j).start()
        for e in range(N_EXPERTS):
            @pl.when(padded_ref[e] > 0)
            def _():
                last_tile(e).wait()
        for j in range(n_tiles - N_EXPERTS, n_tiles):
            @pl.when(j >= used)
            def _():
                spare_tile(j).wait()

    def start(j, carry):
        for u in range(DMA_UNROLL):
            t = j * DMA_UNROLL + u
            for d_ref in (d0_ref, d1_ref):
                pltpu.make_async_copy(_row_tile(h_ref, t), _row_tile(xs_hbm, d_ref[0, 0, t]), sem).start()
        return carry

    lax.fori_loop(0, DISPATCH_ROWS // DMA_UNROLL, start, 0)

    def drain(j, carry):
        for _ in range(DMA_UNROLL * TOP_K):
            pltpu.make_async_copy(_row_tile(h_ref, 0), _row_tile(xs_hbm, 0), sem).wait()
        return carry

    lax.fori_loop(0, DISPATCH_ROWS // DMA_UNROLL, drain, 0)


def _dispatch(pends, padded, dest0, dest1, h2t, p_rows):
    n = h2t.shape[0] // ROW_TILE
    steps = n // DISPATCH_ROWS
    smem_rows = lambda: pl.BlockSpec((1, 1, DISPATCH_ROWS), lambda i, pe, pa: (i, 0, 0),
                                     memory_space=pltpu.SMEM)
    return pl.pallas_call(
        _dispatch_kernel,
        grid_spec=pltpu.PrefetchScalarGridSpec(
            num_scalar_prefetch=2,
            grid=(steps,),
            in_specs=[
                smem_rows(),
                smem_rows(),
                pl.BlockSpec((DISPATCH_ROWS * ROW_TILE, LANES), lambda i, pe, pa: (i, 0)),
            ],
            out_specs=pl.BlockSpec(memory_space=pl.ANY),
            scratch_shapes=[pltpu.VMEM((MOE_ROWS * ROW_TILE, LANES), F32), pltpu.SemaphoreType.DMA(())],
        ),
        out_shape=jax.ShapeDtypeStruct((p_rows * ROW_TILE, LANES), F32),
        compiler_params=_params(("arbitrary",)),
        name="moe_dispatch",
    )(pends, padded, dest0.reshape(steps, 1, DISPATCH_ROWS), dest1.reshape(steps, 1, DISPATCH_ROWS), h2t)


def _experts_kernel(te_ref, nv_ref, xs_ref, wg_ref, wu_ref, wd_ref, y_ref):
    valid = pl.program_id(0) < nv_ref[0]

    @pl.when(jnp.logical_not(valid))
    def _():
        y_ref[...] = jnp.zeros_like(y_ref)

    @pl.when(valid)
    def _():
        x = _load_row_tiles(xs_ref, MOE_ROWS).astype(BF16)
        g = jnp.dot(x, wg_ref[0].astype(BF16), preferred_element_type=F32)
        u = jnp.dot(x, wu_ref[0].astype(BF16), preferred_element_type=F32)
        h = (g * jax.nn.sigmoid(g)) * u
        _store_row_tiles(y_ref, jnp.dot(h.astype(BF16), wd_ref[0].astype(BF16), preferred_element_type=F32))


def _experts(tile_expert, n_valid, xs, w_gate, w_up, w_down):
    d = w_gate.shape[1]
    n_tiles = xs.shape[0] // (MOE_ROWS * ROW_TILE)
    row_map = lambda i, te, nv: (jnp.minimum(i, nv[0] - 1), 0)
    w_map = lambda i, te, nv: (te[i], 0, 0)
    return pl.pallas_call(
        _experts_kernel,
        grid_spec=pltpu.PrefetchScalarGridSpec(
            num_scalar_prefetch=2,
            grid=(n_tiles,),
            in_specs=[
                pl.BlockSpec((MOE_ROWS * ROW_TILE, LANES), row_map),
                pl.BlockSpec((1, d, D_EXPERT), w_map),
                pl.BlockSpec((1, d, D_EXPERT), w_map),
                pl.BlockSpec((1, D_EXPERT, d), w_map),
            ],
            out_specs=pl.BlockSpec((MOE_ROWS * ROW_TILE, LANES), lambda i, te, nv: (i, 0)),
        ),
        out_shape=jax.ShapeDtypeStruct(xs.shape, F32),
        compiler_params=_params(("arbitrary",)),
        name="moe_experts",
    )(tile_expert, n_valid, xs, w_gate, w_up, w_down)


def _combine_kernel(d0_ref, d1_ref, d0n_ref, d1n_ref, ri_ref, xn_ref, mod_ref, fg_ref, y_hbm,
                    o_ref, buf, sem):
    i = pl.program_id(0)
    slot = i % 2

    def gather(refs, to_slot):
        def start(j, carry):
            for u in range(DMA_UNROLL):
                t = j * DMA_UNROLL + u
                for k, d_ref in enumerate(refs):
                    pltpu.make_async_copy(_row_tile(y_hbm, d_ref[0, 0, t]), _row_tile(buf, t, to_slot, k),
                                          sem.at[to_slot]).start()
            return carry

        lax.fori_loop(0, COMBINE_ROWS // DMA_UNROLL, start, 0)

    @pl.when(i == 0)
    def _():
        gather((d0_ref, d1_ref), 0)

    @pl.when(i + 1 < pl.num_programs(0))
    def _():
        gather((d0n_ref, d1n_ref), 1 - slot)

    def drain(j, carry):
        for _ in range(DMA_UNROLL * TOP_K):
            pltpu.make_async_copy(_row_tile(y_hbm, 0), _row_tile(buf, 0, slot, 0), sem.at[slot]).wait()
        return carry

    lax.fori_loop(0, COMBINE_ROWS // DMA_UNROLL, drain, 0)

    ri = ri_ref[...]
    y = (_load_row_tiles(buf, COMBINE_ROWS, slot, 0) * ri[:, 4:5]
         + _load_row_tiles(buf, COMBINE_ROWS, slot, 1) * ri[:, 5:6])
    xf = xn_ref[...] + mod_ref[0][5:6] * y
    o_ref[...] = _rms(xf) * fg_ref[...]


def _combine(dest0, dest1, rinfo, x_new, mod3, final_g, ybuf, seq):
    n, d = x_new.shape
    steps = n // COMBINE_ROWS
    tiles_per_batch = seq // COMBINE_ROWS
    cur = lambda: pl.BlockSpec((1, 1, COMBINE_ROWS), lambda i: (i, 0, 0), memory_space=pltpu.SMEM)
    nxt = lambda: pl.BlockSpec((1, 1, COMBINE_ROWS), lambda i: (jnp.minimum(i + 1, steps - 1), 0, 0),
                               memory_space=pltpu.SMEM)
    d0 = dest0.reshape(steps, 1, COMBINE_ROWS)
    d1 = dest1.reshape(steps, 1, COMBINE_ROWS)
    return pl.pallas_call(
        _combine_kernel,
        grid=(steps,),
        in_specs=[
            cur(), cur(), nxt(), nxt(),
            pl.BlockSpec((COMBINE_ROWS, LANES), lambda i: (i, 0)),
            pl.BlockSpec((COMBINE_ROWS, d), lambda i: (i, 0)),
            pl.BlockSpec((1, N_MOD, d), lambda i: (i // tiles_per_batch, 0, 0)),
            pl.BlockSpec((1, d), lambda i: (0, 0)),
            pl.BlockSpec(memory_space=pl.ANY),
        ],
        out_specs=pl.BlockSpec((COMBINE_ROWS, d), lambda i: (i, 0)),
        out_shape=jax.ShapeDtypeStruct((n, d), F32),
        scratch_shapes=[pltpu.VMEM((2, TOP_K, COMBINE_ROWS * ROW_TILE, LANES), F32),
                        pltpu.SemaphoreType.DMA((2,))],
        compiler_params=_params(("arbitrary",)),
        name="moe_combine",
    )(d0, d1, d0, d1, rinfo, x_new, mod3, final_g, ybuf)


def _rope_tables(seq):
    rows = seq // GRID_W
    row_idx = jnp.repeat(jnp.arange(rows, dtype=F32), GRID_W)
    col_idx = jnp.tile(jnp.arange(GRID_W, dtype=F32), rows)
    inv_freq = ROPE_THETA ** (-jnp.arange(0, ROPE_AXIS_DIM, 2, dtype=F32) / ROPE_AXIS_DIM)
    ang = jnp.stack([row_idx[:, None] * inv_freq, col_idx[:, None] * inv_freq], axis=1)
    cos = jnp.cos(ang)
    sin = jnp.sin(ang)
    cos_h = jnp.stack([cos, cos], axis=2).reshape(seq, HEAD_DIM)
    sin_h = jnp.stack([-sin, sin], axis=2).reshape(seq, HEAD_DIM)
    reps = LANES // HEAD_DIM
    return jnp.tile(cos_h, (1, reps)), jnp.tile(sin_h, (1, reps))


def kernel(x, c, ctx, c_ctx, w_mod, b_mod, norm1_g, w_in, q_norm_g, k_norm_g, conv_w, attn_out_g,
           conv_out_g, w_out, norm2_g, w_group, w_router, w_gate, w_up, w_down, final_g):
    assert w_mod.shape[0] == 1, "single-layer block"
    b, s, d = x.shape
    n = b * s
    assert b + 1 <= MOD_ROWS

    cond = jnp.zeros((MOD_ROWS, d), F32).at[:b].set(c).at[b].set(c_ctx)
    mod3 = _modulation(cond, w_mod[0], b_mod[0]).reshape(MOD_ROWS, N_MOD, d)

    cos_t, sin_t = _rope_tables(s)
    head_of = jnp.arange(ATTN_WIDTH) // HEAD_DIM
    bd = jnp.where(head_of[:, None] == head_of[None, :], 1.0 / HEAD_DIM, 0.0).astype(BF16)
    q, k, v, cv = _in_projection(
        x, ctx, mod3, norm1_g, w_in[0].astype(BF16),
        jnp.tile(q_norm_g[0], N_HEADS)[None], jnp.tile(k_norm_g[0], N_KV_HEADS)[None],
        cos_t, sin_t, conv_w[0], conv_out_g, bd)

    a = _attention(q, k, v, attn_out_g)

    wr = jnp.zeros((d, LANES), F32).at[:, :N_EXPERTS].set(w_router[0])
    wr = wr.at[:, N_EXPERTS:N_EXPERTS + N_GROUPS].set(w_group[0]).astype(BF16)
    ti = jnp.arange(OUT_ROWS)
    tri = (ti[None, :] < ti[:, None]).astype(BF16)
    x_new, h2t, rinfo, rinfo_t, counts = _out_projection(
        a.reshape(n, ATTN_WIDTH), cv.reshape(n, CONV_WIDTH), w_out[0].astype(BF16),
        x.reshape(n, d), mod3, norm2_g, wr, tri, s)

    cnt = counts[0, :N_EXPERTS].astype(jnp.int32)
    padded = ((cnt + MOE_ROWS - 1) // MOE_ROWS) * MOE_ROWS
    pends = jnp.cumsum(padded)
    pstarts = pends - padded
    experts = jnp.arange(N_EXPERTS, dtype=jnp.int32)[:, None]

    def slots(k):
        eid = rinfo_t[k].astype(jnp.int32)
        rank = rinfo_t[TOP_K + k].astype(jnp.int32)
        return jnp.sum(jnp.where(eid[None, :] == experts, pstarts[:, None], 0), axis=0) + rank

    dest0, dest1 = slots(0), slots(1)
    p_rows = n * TOP_K + N_EXPERTS * MOE_ROWS
    n_tiles = p_rows // MOE_ROWS
    n_valid = pends[-1] // MOE_ROWS
    tile_start = jnp.arange(n_tiles, dtype=jnp.int32) * MOE_ROWS
    last_start = jnp.maximum(pends[-1] - MOE_ROWS, 0)
    tile_expert = jnp.sum(jnp.minimum(tile_start, last_start)[:, None] >= pends[None, :], axis=1)
    tile_expert = jnp.minimum(tile_expert, N_EXPERTS - 1).astype(jnp.int32)

    xs = _dispatch(pends, padded, dest0, dest1, h2t, p_rows)
    ybuf = _experts(tile_expert, n_valid.reshape(1), xs, w_gate[0], w_up[0], w_down[0])
    out = _combine(dest0, dest1, rinfo, x_new, mod3, final_g.reshape(1, d), ybuf, s)
    return out.reshape(b, s, d)
```

```python
import functools

import jax
import jax.numpy as jnp
from jax import lax
from jax.experimental import pallas as pl
from jax.experimental.pallas import tpu as pltpu

F32 = jnp.float32
BF16 = jnp.bfloat16

D_MODEL = 1024
GRID_W = 64
ATTN_WIDTH = 512
N_HEADS = 8
N_KV_HEADS = 2
HEAD_DIM = 64
KV_REP = N_HEADS // N_KV_HEADS
KV_WIDTH = N_KV_HEADS * HEAD_DIM
CONV_WIDTH = 512
IN_COLS = ATTN_WIDTH + 2 * KV_WIDTH + 3 * CONV_WIDTH
ROPE_THETA = 10000.0
ROPE_AXIS_DIM = HEAD_DIM // 2
ROPE_FREQS = ROPE_AXIS_DIM // 2
N_GROUPS = 4
EXPERTS_PER_GROUP = 8
N_EXPERTS = N_GROUPS * EXPERTS_PER_GROUP
TOP_K = 2
D_EXPERT = 768
N_MOD = 6
EPS = 1e-6
LOG2_E = 1.4426950408889634
Q_SCALE = HEAD_DIM ** -0.5 * LOG2_E
V_LANES = 2 * HEAD_DIM

LANES = 128
ROW_TILE = 8
MOD_ROWS = 16
IN_ROWS = 256
CONV_PAD = 8
ATTN_TQ = 128
ATTN_HEAD_SPLIT = 2
OUT_ROWS = 256
DISPATCH_ROWS = 1024
DMA_UNROLL = 8
MOE_ROWS = 256
COMBINE_ROWS = 256
VMEM_LIMIT = 56 * 1024 * 1024


def _params(semantics, vmem=None):
    return pltpu.CompilerParams(dimension_semantics=semantics,
                                vmem_limit_bytes=vmem if vmem else VMEM_LIMIT)


def _rms(x):
    return x * lax.rsqrt(jnp.mean(x * x, axis=-1, keepdims=True) + EPS)


def _store_row_tiles(ref, val):
    rows, width = val.shape
    assert width == ROW_TILE * LANES
    for i in range(ROW_TILE):
        ref[pl.ds(i, rows, stride=ROW_TILE), :] = val[:, i * LANES:(i + 1) * LANES]


def _load_row_tiles(ref, rows, *lead):
    return jnp.concatenate(
        [ref[(*lead, pl.ds(i, rows, stride=ROW_TILE), slice(None))] for i in range(ROW_TILE)], axis=1)


def _row_tile(ref, row, *lead):
    return ref.at[(*lead, pl.ds(pl.multiple_of(row * ROW_TILE, ROW_TILE), ROW_TILE))]


def _mod_kernel(c_ref, w_ref, b_ref, o_ref):
    c = c_ref[...]
    s = c * jax.nn.sigmoid(c)
    o_ref[...] = jnp.dot(s.astype(BF16), w_ref[...].astype(BF16),
                         preferred_element_type=F32) + b_ref[...]


def _modulation(cond, w_mod, b_mod):
    d = cond.shape[1]
    n_out = w_mod.shape[1]
    tn = 1024
    return pl.pallas_call(
        _mod_kernel,
        grid=(n_out // tn,),
        in_specs=[pl.BlockSpec((MOD_ROWS, d), lambda j: (0, 0)),
                  pl.BlockSpec((d, tn), lambda j: (0, j)),
                  pl.BlockSpec((1, tn), lambda j: (0, j))],
        out_specs=pl.BlockSpec((MOD_ROWS, tn), lambda j: (0, j)),
        out_shape=jax.ShapeDtypeStruct((MOD_ROWS, n_out), F32),
        compiler_params=_params(("arbitrary",)),
        name="modulation",
    )(cond, w_mod, b_mod.reshape(1, n_out))


def _with_ones(v):
    return jnp.concatenate([v, jnp.ones_like(v)], axis=1).astype(BF16)


def _rope(x, cos, sin_signed, is_lo):
    partner = jnp.where(is_lo, pltpu.roll(x, LANES - ROPE_FREQS, 1), pltpu.roll(x, ROPE_FREQS, 1))
    return x * cos + partner * sin_signed


def _inproj_kernel(x_ref, ctx_ref, mod_ref, cmod_ref, n1g_ref, w_ref, qg_ref, kg_ref,
                   cos_ref, sin_ref, convw_ref, cog_ref, bd_ref,
                   q_ref, k_ref, v_ref, cv_ref, p_s, gb_s):
    seq = x_ref.shape[1]
    ctx_len = ctx_ref.shape[1]
    g1 = n1g_ref[...]
    bd = bd_ref[...]
    bd_kv = bd_ref[0:KV_WIDTH, 0:KV_WIDTH]
    kg = kg_ref[...]
    qg = qg_ref[...]

    def head_ms(z, m):
        return jnp.dot((z * z).astype(BF16), m, preferred_element_type=F32)

    cmod = cmod_ref[0]
    hc = _rms(ctx_ref[0]) * g1 * (1.0 + cmod[1:2]) + cmod[0:1]
    zc = jnp.dot(hc.astype(BF16), w_ref[:, ATTN_WIDTH:ATTN_WIDTH + 2 * KV_WIDTH],
                 preferred_element_type=F32)
    kc = zc[:, :KV_WIDTH]
    kc = kc * lax.rsqrt(head_ms(kc, bd_kv) + EPS) * kg
    vc = zc[:, KV_WIDTH:]
    for g in range(N_KV_HEADS):
        k_ref[0, g, 0:ctx_len, :] = kc[:, g * HEAD_DIM:(g + 1) * HEAD_DIM].astype(BF16)
        v_ref[0, g, 0:ctx_len, :] = _with_ones(vc[:, g * HEAD_DIM:(g + 1) * HEAD_DIM])

    zeros = jnp.zeros((CONV_PAD, CONV_WIDTH), F32)
    p_s[0:CONV_PAD, :] = zeros
    p_s[CONV_PAD + seq:CONV_PAD + seq + CONV_PAD, :] = zeros

    mod = mod_ref[0]
    sh1 = mod[0:1]
    sc1 = mod[1:2]
    lane = lax.broadcasted_iota(jnp.int32, (IN_ROWS, LANES), 1)
    is_lo = (lane // ROPE_FREQS) % 2 == 0

    def proj_chunk(c, carry):
        r0 = pl.multiple_of(c * IN_ROWS, IN_ROWS)
        h = (_rms(x_ref[0, pl.ds(r0, IN_ROWS), :]) * g1 * (1.0 + sc1) + sh1).astype(BF16)
        cos = cos_ref[pl.ds(r0, IN_ROWS), :]
        sin = sin_ref[pl.ds(r0, IN_ROWS), :]
        zq = jnp.dot(h, w_ref[:, 0:ATTN_WIDTH], preferred_element_type=F32)
        qn = zq * lax.rsqrt(head_ms(zq, bd) + EPS) * qg
        for j in range(ATTN_WIDTH // LANES):
            blk = _rope(qn[:, j * LANES:(j + 1) * LANES], cos, sin, is_lo) * Q_SCALE
            for hh in range(LANES // HEAD_DIM):
                head = j * (LANES // HEAD_DIM) + hh
                q_ref[0, head, pl.ds(r0, IN_ROWS), :] = (
                    blk[:, hh * HEAD_DIM:(hh + 1) * HEAD_DIM].astype(BF16))
        zkv = jnp.dot(h, w_ref[:, ATTN_WIDTH:ATTN_WIDTH + 2 * KV_WIDTH], preferred_element_type=F32)
        kx = zkv[:, :KV_WIDTH]
        kx = _rope(kx * lax.rsqrt(head_ms(kx, bd_kv) + EPS) * kg, cos, sin, is_lo)
        vx = zkv[:, KV_WIDTH:]
        for g in range(N_KV_HEADS):
            k_ref[0, g, pl.ds(ctx_len + r0, IN_ROWS), :] = kx[:, g * HEAD_DIM:(g + 1) * HEAD_DIM].astype(BF16)
            v_ref[0, g, pl.ds(ctx_len + r0, IN_ROWS), :] = _with_ones(vx[:, g * HEAD_DIM:(g + 1) * HEAD_DIM])
        c0 = ATTN_WIDTH + 2 * KV_WIDTH
        gb_s[pl.ds(r0, IN_ROWS), :] = jnp.dot(h, w_ref[:, c0:c0 + CONV_WIDTH], preferred_element_type=F32)
        zc_ = jnp.dot(h, w_ref[:, c0 + CONV_WIDTH:c0 + 2 * CONV_WIDTH], preferred_element_type=F32)
        zu = jnp.dot(h, w_ref[:, c0 + 2 * CONV_WIDTH:c0 + 3 * CONV_WIDTH], preferred_element_type=F32)
        p_s[pl.ds(CONV_PAD + r0, IN_ROWS), :] = zc_ * zu
        return carry

    lax.fori_loop(0, seq // IN_ROWS, proj_chunk, 0)

    cw = convw_ref[...]
    cog = cog_ref[...]

    def conv_chunk(c, carry):
        r0 = pl.multiple_of(c * IN_ROWS, IN_ROWS)
        win = p_s[pl.ds(r0, IN_ROWS + 2 * CONV_PAD), :]
        n_win = IN_ROWS + 2 * CONV_PAD
        prev = pltpu.roll(win, 1, 0)[CONV_PAD:CONV_PAD + IN_ROWS]
        cur = win[CONV_PAD:CONV_PAD + IN_ROWS]
        nxt = pltpu.roll(win, n_win - 1, 0)[CONV_PAD:CONV_PAD + IN_ROWS]
        y = cw[0:1] * prev + cw[1:2] * cur + cw[2:3] * nxt
        cvv = gb_s[pl.ds(r0, IN_ROWS), :] * y
        cvn = cvv * lax.rsqrt(head_ms(cvv, bd) + EPS) * cog
        cv_ref[0, pl.ds(r0, IN_ROWS), :] = cvn.astype(BF16)
        return carry

    lax.fori_loop(0, seq // IN_ROWS, conv_chunk, 0)


def _in_projection(x, ctx, mod3, norm1_g, w_in_bf, qg_t, kg_t, cos_t, sin_t, conv_w, conv_out_g, bd):
    b, s, d = x.shape
    ctx_len = ctx.shape[1]
    n_keys = ctx_len + s
    const = lambda *shape: pl.BlockSpec(shape, lambda i: (0,) * len(shape))
    return pl.pallas_call(
        _inproj_kernel,
        grid=(b,),
        in_specs=[
            pl.BlockSpec((1, s, d), lambda i: (i, 0, 0)),
            pl.BlockSpec((1, ctx_len, d), lambda i: (i, 0, 0)),
            pl.BlockSpec((1, N_MOD, d), lambda i: (i, 0, 0)),
            pl.BlockSpec((1, N_MOD, d), lambda i: (b, 0, 0)),
            const(1, d),
            const(d, IN_COLS),
            const(1, ATTN_WIDTH),
            const(1, KV_WIDTH),
            const(s, LANES),
            const(s, LANES),
            const(3, CONV_WIDTH),
            const(1, CONV_WIDTH),
            const(ATTN_WIDTH, ATTN_WIDTH),
        ],
        out_specs=[
            pl.BlockSpec((1, N_HEADS, s, HEAD_DIM), lambda i: (i, 0, 0, 0)),
            pl.BlockSpec((1, N_KV_HEADS, n_keys, HEAD_DIM), lambda i: (i, 0, 0, 0)),
            pl.BlockSpec((1, N_KV_HEADS, n_keys, V_LANES), lambda i: (i, 0, 0, 0)),
            pl.BlockSpec((1, s, CONV_WIDTH), lambda i: (i, 0, 0)),
        ],
        out_shape=[
            jax.ShapeDtypeStruct((b, N_HEADS, s, HEAD_DIM), BF16),
            jax.ShapeDtypeStruct((b, N_KV_HEADS, n_keys, HEAD_DIM), BF16),
            jax.ShapeDtypeStruct((b, N_KV_HEADS, n_keys, V_LANES), BF16),
            jax.ShapeDtypeStruct((b, s, CONV_WIDTH), BF16),
        ],
        scratch_shapes=[pltpu.VMEM((s + 2 * CONV_PAD, CONV_WIDTH), F32),
                        pltpu.VMEM((s, CONV_WIDTH), F32)],
        compiler_params=_params(("arbitrary",)),
        name="in_projection",
    )(x, ctx, mod3, mod3, norm1_g, w_in_bf, qg_t, kg_t, cos_t, sin_t, conv_w, conv_out_g, bd)


def _attn_kernel(q_ref, k_ref, v_ref, g_ref, o_ref):
    tq = q_ref.shape[2]
    pieces = []
    hs = KV_REP // ATTN_HEAD_SPLIT
    subs = [(g, g * KV_REP + j * hs) for g in range(N_KV_HEADS) for j in range(ATTN_HEAD_SPLIT)]
    scores = []
    for g, h0 in subs:
        qs = jnp.concatenate([q_ref[0, h0 + r] for r in range(hs)], axis=0)
        scores.append(lax.dot_general(qs, k_ref[0, g], (((1,), (1,)), ((), ())),
                                      preferred_element_type=F32))
    probs = [jnp.exp2(s - jnp.max(s, axis=-1, keepdims=True)).astype(BF16) for s in scores]
    lane = lax.broadcasted_iota(jnp.int32, (hs * tq, LANES), 1)
    for (g, h0), p in zip(subs, probs):
        ov = jnp.dot(p, v_ref[0, g], preferred_element_type=F32)
        o = ov / pltpu.roll(ov, HEAD_DIM, 1)
        ms = jnp.sum(jnp.where(lane < HEAD_DIM, o * o, 0.0), axis=-1, keepdims=True) * (1.0 / HEAD_DIM)
        o = o * lax.rsqrt(ms + EPS)
        for r in range(hs):
            pieces.append(o[r * tq:(r + 1) * tq, 0:HEAD_DIM])
    o_ref[0] = (jnp.concatenate(pieces, axis=1) * g_ref[...]).astype(BF16)


def _attention(q, k, v, attn_out_g):
    b, _, s, _ = q.shape
    n_keys = k.shape[2]
    return pl.pallas_call(
        _attn_kernel,
        grid=(b, s // ATTN_TQ),
        in_specs=[
            pl.BlockSpec((1, N_HEADS, ATTN_TQ, HEAD_DIM), lambda i, j: (i, 0, j, 0)),
            pl.BlockSpec((1, N_KV_HEADS, n_keys, HEAD_DIM), lambda i, j: (i, 0, 0, 0)),
            pl.BlockSpec((1, N_KV_HEADS, n_keys, V_LANES), lambda i, j: (i, 0, 0, 0)),
            pl.BlockSpec((1, ATTN_WIDTH), lambda i, j: (0, 0)),
        ],
        out_specs=pl.BlockSpec((1, ATTN_TQ, ATTN_WIDTH), lambda i, j: (i, j, 0)),
        out_shape=jax.ShapeDtypeStruct((b, s, ATTN_WIDTH), BF16),
        compiler_params=_params(("arbitrary", "arbitrary")),
        name="attention",
    )(q, k, v, attn_out_g)


def _outproj_kernel(a_ref, cv_ref, w_ref, x_ref, mod_ref, n2g_ref, wr_ref, tri_ref,
                    xn_ref, h2_ref, ri_ref, rit_ref, cnt_ref, carry_s):
    i = pl.program_id(0)

    @pl.when(i == 0)
    def _():
        carry_s[...] = jnp.zeros_like(carry_s)

    mod = mod_ref[0]
    merged = (jnp.dot(a_ref[...], w_ref[0:ATTN_WIDTH, :], preferred_element_type=F32)
              + jnp.dot(cv_ref[...], w_ref[ATTN_WIDTH:, :], preferred_element_type=F32))
    xn = x_ref[...] + mod[2:3] * merged
    xn_ref[...] = xn
    h2 = _rms(xn) * n2g_ref[...] * (1.0 + mod[4:5]) + mod[3:4]
    _store_row_tiles(h2_ref, h2)

    logits = jnp.dot(h2.astype(BF16), wr_ref[...], preferred_element_type=F32)
    rows = logits.shape[0]
    li = lax.broadcasted_iota(jnp.int32, (rows, LANES), 1)
    neg = jnp.float32(-jnp.inf)

    gmask = (li >= N_EXPERTS) & (li < N_EXPERTS + N_GROUPS)
    lg = jnp.where(gmask, logits, neg)
    ge = jnp.exp(lg - jnp.max(lg, axis=-1, keepdims=True))
    g_prob = ge / jnp.sum(ge, axis=-1, keepdims=True)
    g_w = jnp.max(g_prob, axis=-1, keepdims=True)
    g_sel = jnp.min(jnp.where(gmask & (g_prob == g_w), li, 2 * LANES), axis=-1, keepdims=True) - N_EXPERTS

    emask = (li < N_EXPERTS) & (li // EXPERTS_PER_GROUP == g_sel)
    le = jnp.where(emask, logits, neg)
    ee = jnp.exp(le - jnp.max(le, axis=-1, keepdims=True))
    e_prob = ee / jnp.sum(ee, axis=-1, keepdims=True)
    p1 = jnp.max(jnp.where(emask, e_prob, -1.0), axis=-1, keepdims=True)
    i1 = jnp.min(jnp.where(emask & (e_prob == p1), li, 2 * LANES), axis=-1, keepdims=True)
    mask2 = emask & (li != i1)
    p2 = jnp.max(jnp.where(mask2, e_prob, -1.0), axis=-1, keepdims=True)
    i2 = jnp.min(jnp.where(mask2 & (e_prob == p2), li, 2 * LANES), axis=-1, keepdims=True)
    psum = p1 + p2
    w1 = g_w * (p1 / psum)
    w2 = g_w * (p2 / psum)

    hit1 = li == i1
    hit2 = li == i2
    onehot = jnp.where(hit1 | hit2, 1.0, 0.0)
    before = jnp.dot(tri_ref[...], onehot.astype(BF16), preferred_element_type=F32) + carry_s[...]
    r1 = jnp.sum(jnp.where(hit1, before, 0.0), axis=-1, keepdims=True)
    r2 = jnp.sum(jnp.where(hit2, before, 0.0), axis=-1, keepdims=True)
    carry_s[...] = carry_s[...] + jnp.sum(onehot, axis=0, keepdims=True)

    info = jnp.where(li == 0, i1.astype(F32),
           jnp.where(li == 1, i2.astype(F32),
           jnp.where(li == 2, r1,
           jnp.where(li == 3, r2,
           jnp.where(li == 4, w1,
           jnp.where(li == 5, w2, 0.0))))))
    ri_ref[...] = info
    rit_ref[...] = jnp.transpose(info)[0:ROW_TILE, :]
    cnt_ref[...] = carry_s[...]


def _out_projection(a, cv, w_out_bf, x2, mod3, norm2_g, wr_bf, tri, seq):
    n, d = x2.shape
    tiles_per_batch = seq // OUT_ROWS
    return pl.pallas_call(
        _outproj_kernel,
        grid=(n // OUT_ROWS,),
        in_specs=[
            pl.BlockSpec((OUT_ROWS, ATTN_WIDTH), lambda i: (i, 0)),
            pl.BlockSpec((OUT_ROWS, CONV_WIDTH), lambda i: (i, 0)),
            pl.BlockSpec((d, d), lambda i: (0, 0)),
            pl.BlockSpec((OUT_ROWS, d), lambda i: (i, 0)),
            pl.BlockSpec((1, N_MOD, d), lambda i: (i // tiles_per_batch, 0, 0)),
            pl.BlockSpec((1, d), lambda i: (0, 0)),
            pl.BlockSpec((d, LANES), lambda i: (0, 0)),
            pl.BlockSpec((OUT_ROWS, OUT_ROWS), lambda i: (0, 0)),
        ],
        out_specs=[
            pl.BlockSpec((OUT_ROWS, d), lambda i: (i, 0)),
            pl.BlockSpec((OUT_ROWS * ROW_TILE, LANES), lambda i: (i, 0)),
            pl.BlockSpec((OUT_ROWS, LANES), lambda i: (i, 0)),
            pl.BlockSpec((ROW_TILE, OUT_ROWS), lambda i: (0, i)),
            pl.BlockSpec((1, LANES), lambda i: (0, 0)),
        ],
        out_shape=[
            jax.ShapeDtypeStruct((n, d), F32),
            jax.ShapeDtypeStruct((n * ROW_TILE, LANES), F32),
            jax.ShapeDtypeStruct((n, LANES), F32),
            jax.ShapeDtypeStruct((ROW_TILE, n), F32),
            jax.ShapeDtypeStruct((1, LANES), F32),
        ],
        scratch_shapes=[pltpu.VMEM((1, LANES), F32)],
        compiler_params=_params(("arbitrary",)),
        name="out_projection_routing",
    )(a, cv, w_out_bf, x2, mod3, norm2_g, wr_bf, tri)


def _dispatch_kernel(pend_ref, padded_ref, d0_ref, d1_ref, h_ref, xs_hbm, zero_s, sem):
    tile = MOE_ROWS * ROW_TILE

    @pl.when(pl.program_id(0) == 0)
    def _():
        zero_s[...] = jnp.zeros_like(zero_s)

        def last_tile(e):
            start = pl.multiple_of((pend_ref[e] - MOE_ROWS) * ROW_TILE, tile)
            return pltpu.make_async_copy(zero_s, xs_hbm.at[pl.ds(start, tile)], sem)

        def spare_tile(j):
            return pltpu.make_async_copy(zero_s, xs_hbm.at[pl.ds(j * tile, tile)], sem)

        n_tiles = xs_hbm.shape[0] // tile
        used = pend_ref[N_EXPERTS - 1] // MOE_ROWS
        for e in range(N_EXPERTS):
            @pl.when(padded_ref[e] > 0)
            def _():
                last_tile(e).start()
        for j in range(n_tiles - N_EXPERTS, n_tiles):
            @pl.when(j >= used)
            def _():
                spare_tile(j).start()
        for e in range(N_EXPERTS):
            @pl.when(padded_ref[e] > 0)
            def _():
                last_tile(e).wait()
        for j in range(n_tiles - N_EXPERTS, n_tiles):
            @pl.when(j >= used)
            def _():
                spare_tile(j).wait()

    def start(j, carry):
        for u in range(DMA_UNROLL):
            t = j * DMA_UNROLL + u
            for d_ref in (d0_ref, d1_ref):
                pltpu.make_async_copy(_row_tile(h_ref, t), _row_tile(xs_hbm, d_ref[0, 0, t]), sem).start()
        return carry

    lax.fori_loop(0, DISPATCH_ROWS // DMA_UNROLL, start, 0)

    def drain(j, carry):
        for _ in range(DMA_UNROLL * TOP_K):
            pltpu.make_async_copy(_row_tile(h_ref, 0), _row_tile(xs_hbm, 0), sem).wait()
        return carry

    lax.fori_loop(0, DISPATCH_ROWS // DMA_UNROLL, drain, 0)


def _dispatch(pends, padded, dest0, dest1, h2t, p_rows):
    n = h2t.shape[0] // ROW_TILE
    steps = n // DISPATCH_ROWS
    smem_rows = lambda: pl.BlockSpec((1, 1, DISPATCH_ROWS), lambda i, pe, pa: (i, 0, 0),
                                     memory_space=pltpu.SMEM)
    return pl.pallas_call(
        _dispatch_kernel,
        grid_spec=pltpu.PrefetchScalarGridSpec(
            num_scalar_prefetch=2,
            grid=(steps,),
            in_specs=[
                smem_rows(),
                smem_rows(),
                pl.BlockSpec((DISPATCH_ROWS * ROW_TILE, LANES), lambda i, pe, pa: (i, 0)),
            ],
            out_specs=pl.BlockSpec(memory_space=pl.ANY),
            scratch_shapes=[pltpu.VMEM((MOE_ROWS * ROW_TILE, LANES), F32), pltpu.SemaphoreType.DMA(())],
        ),
        out_shape=jax.ShapeDtypeStruct((p_rows * ROW_TILE, LANES), F32),
        compiler_params=_params(("arbitrary",)),
        name="moe_dispatch",
    )(pends, padded, dest0.reshape(steps, 1, DISPATCH_ROWS), dest1.reshape(steps, 1, DISPATCH_ROWS), h2t)


def _experts_kernel(te_ref, nv_ref, xs_ref, wg_ref, wu_ref, wd_ref, y_ref):
    valid = pl.program_id(0) < nv_ref[0]

    @pl.when(jnp.logical_not(valid))
    def _():
        y_ref[...] = jnp.zeros_like(y_ref)

    @pl.when(valid)
    def _():
        x = _load_row_tiles(xs_ref, MOE_ROWS).astype(BF16)
        g = jnp.dot(x, wg_ref[0].astype(BF16), preferred_element_type=F32)
        u = jnp.dot(x, wu_ref[0].astype(BF16), preferred_element_type=F32)
        h = ((g * jax.nn.sigmoid(g)) * u).astype(BF16)
        half = wd_ref.shape[2] // 2
        y = [jnp.dot(h, wd_ref[0, :, j * half:(j + 1) * half].astype(BF16), preferred_element_type=F32)
             for j in range(2)]
        _store_row_tiles(y_ref, jnp.concatenate(y, axis=1))


def _experts(tile_expert, n_valid, xs, w_gate, w_up, w_down):
    d = w_gate.shape[1]
    n_tiles = xs.shape[0] // (MOE_ROWS * ROW_TILE)
    row_map = lambda i, te, nv: (jnp.minimum(i, nv[0] - 1), 0)
    w_map = lambda i, te, nv: (te[i], 0, 0)
    return pl.pallas_call(
        _experts_kernel,
        grid_spec=pltpu.PrefetchScalarGridSpec(
            num_scalar_prefetch=2,
            grid=(n_tiles,),
            in_specs=[
                pl.BlockSpec((MOE_ROWS * ROW_TILE, LANES), row_map),
                pl.BlockSpec((1, d, D_EXPERT), w_map),
                pl.BlockSpec((1, d, D_EXPERT), w_map),
                pl.BlockSpec((1, D_EXPERT, d), w_map),
            ],
            out_specs=pl.BlockSpec((MOE_ROWS * ROW_TILE, LANES), lambda i, te, nv: (i, 0)),
        ),
        out_shape=jax.ShapeDtypeStruct(xs.shape, F32),
        compiler_params=_params(("arbitrary",)),
        name="moe_experts",
    )(tile_expert, n_valid, xs, w_gate, w_up, w_down)


def _combine_kernel(d0_ref, d1_ref, d0n_ref, d1n_ref, ri_ref, xn_ref, mod_ref, fg_ref, y_hbm,
                    o_ref, buf, sem):
    i = pl.program_id(0)
    slot = i % 2

    def gather(refs, to_slot):
        def start(j, carry):
            for u in range(DMA_UNROLL):
                t = j * DMA_UNROLL + u
                for k, d_ref in enumerate(refs):
                    pltpu.make_async_copy(_row_tile(y_hbm, d_ref[0, 0, t]), _row_tile(buf, t, to_slot, k),
                                          sem.at[to_slot]).start()
            return carry

        lax.fori_loop(0, COMBINE_ROWS // DMA_UNROLL, start, 0)

    @pl.when(i == 0)
    def _():
        gather((d0_ref, d1_ref), 0)

    @pl.when(i + 1 < pl.num_programs(0))
    def _():
        gather((d0n_ref, d1n_ref), 1 - slot)

    def drain(j, carry):
        for _ in range(DMA_UNROLL * TOP_K):
            pltpu.make_async_copy(_row_tile(y_hbm, 0), _row_tile(buf, 0, slot, 0), sem.at[slot]).wait()
        return carry

    lax.fori_loop(0, COMBINE_ROWS // DMA_UNROLL, drain, 0)

    ri = ri_ref[...]
    y = (_load_row_tiles(buf, COMBINE_ROWS, slot, 0) * ri[:, 4:5]
         + _load_row_tiles(buf, COMBINE_ROWS, slot, 1) * ri[:, 5:6])
    xf = xn_ref[...] + mod_ref[0][5:6] * y
    o_ref[...] = _rms(xf) * fg_ref[...]


def _combine(dest0, dest1, rinfo, x_new, mod3, final_g, ybuf, seq):
    n, d = x_new.shape
    steps = n // COMBINE_ROWS
    tiles_per_batch = seq // COMBINE_ROWS
    cur = lambda: pl.BlockSpec((1, 1, COMBINE_ROWS), lambda i: (i, 0, 0), memory_space=pltpu.SMEM)
    nxt = lambda: pl.BlockSpec((1, 1, COMBINE_ROWS), lambda i: (jnp.minimum(i + 1, steps - 1), 0, 0),
                               memory_space=pltpu.SMEM)
    d0 = dest0.reshape(steps, 1, COMBINE_ROWS)
    d1 = dest1.reshape(steps, 1, COMBINE_ROWS)
    return pl.pallas_call(
        _combine_kernel,
        grid=(steps,),
        in_specs=[
            cur(), cur(), nxt(), nxt(),
            pl.BlockSpec((COMBINE_ROWS, LANES), lambda i: (i, 0)),
            pl.BlockSpec((COMBINE_ROWS, d), lambda i: (i, 0)),
            pl.BlockSpec((1, N_MOD, d), lambda i: (i // tiles_per_batch, 0, 0)),
            pl.BlockSpec((1, d), lambda i: (0, 0)),
            pl.BlockSpec(memory_space=pl.ANY),
        ],
        out_specs=pl.BlockSpec((COMBINE_ROWS, d), lambda i: (i, 0)),
        out_shape=jax.ShapeDtypeStruct((n, d), F32),
        scratch_shapes=[pltpu.VMEM((2, TOP_K, COMBINE_ROWS * ROW_TILE, LANES), F32),
                        pltpu.SemaphoreType.DMA((2,))],
        compiler_params=_params(("arbitrary",)),
        name="moe_combine",
    )(d0, d1, d0, d1, rinfo, x_new, mod3, final_g, ybuf)


def _rope_tables(seq):
    rows = seq // GRID_W
    row_idx = jnp.repeat(jnp.arange(rows, dtype=F32), GRID_W)
    col_idx = jnp.tile(jnp.arange(GRID_W, dtype=F32), rows)
    inv_freq = ROPE_THETA ** (-jnp.arange(0, ROPE_AXIS_DIM, 2, dtype=F32) / ROPE_AXIS_DIM)
    ang = jnp.stack([row_idx[:, None] * inv_freq, col_idx[:, None] * inv_freq], axis=1)
    cos = jnp.cos(ang)
    sin = jnp.sin(ang)
    cos_h = jnp.stack([cos, cos], axis=2).reshape(seq, HEAD_DIM)
    sin_h = jnp.stack([-sin, sin], axis=2).reshape(seq, HEAD_DIM)
    reps = LANES // HEAD_DIM
    return jnp.tile(cos_h, (1, reps)), jnp.tile(sin_h, (1, reps))


def kernel(x, c, ctx, c_ctx, w_mod, b_mod, norm1_g, w_in, q_norm_g, k_norm_g, conv_w, attn_out_g,
           conv_out_g, w_out, norm2_g, w_group, w_router, w_gate, w_up, w_down, final_g):
    assert w_mod.shape[0] == 1, "single-layer block"
    b, s, d = x.shape
    n = b * s
    assert b + 1 <= MOD_ROWS

    cond = jnp.zeros((MOD_ROWS, d), F32).at[:b].set(c).at[b].set(c_ctx)
    mod3 = _modulation(cond, w_mod[0], b_mod[0]).reshape(MOD_ROWS, N_MOD, d)

    cos_t, sin_t = _rope_tables(s)
    head_of = jnp.arange(ATTN_WIDTH) // HEAD_DIM
    bd = jnp.where(head_of[:, None] == head_of[None, :], 1.0 / HEAD_DIM, 0.0).astype(BF16)
    q, k, v, cv = _in_projection(
        x, ctx, mod3, norm1_g, w_in[0].astype(BF16),
        jnp.tile(q_norm_g[0], N_HEADS)[None], jnp.tile(k_norm_g[0], N_KV_HEADS)[None],
        cos_t, sin_t, conv_w[0], conv_out_g, bd)

    a = _attention(q, k, v, attn_out_g)

    wr = jnp.zeros((d, LANES), F32).at[:, :N_EXPERTS].set(w_router[0])
    wr = wr.at[:, N_EXPERTS:N_EXPERTS + N_GROUPS].set(w_group[0]).astype(BF16)
    ti = jnp.arange(OUT_ROWS)
    tri = (ti[None, :] < ti[:, None]).astype(BF16)
    x_new, h2t, rinfo, rinfo_t, counts = _out_projection(
        a.reshape(n, ATTN_WIDTH), cv.reshape(n, CONV_WIDTH), w_out[0].astype(BF16),
        x.reshape(n, d), mod3, norm2_g, wr, tri, s)

    cnt = counts[0, :N_EXPERTS].astype(jnp.int32)
    padded = ((cnt + MOE_ROWS - 1) // MOE_ROWS) * MOE_ROWS
    pends = jnp.cumsum(padded)
    pstarts = pends - padded
    experts = jnp.arange(N_EXPERTS, dtype=jnp.int32)[:, None]

    def slots(k):
        eid = rinfo_t[k].astype(jnp.int32)
        rank = rinfo_t[TOP_K + k].astype(jnp.int32)
        return jnp.sum(jnp.where(eid[None, :] == experts, pstarts[:, None], 0), axis=0) + rank

    dest0, dest1 = slots(0), slots(1)
    p_rows = n * TOP_K + N_EXPERTS * MOE_ROWS
    n_tiles = p_rows // MOE_ROWS
    n_valid = pends[-1] // MOE_ROWS
    tile_start = jnp.arange(n_tiles, dtype=jnp.int32) * MOE_ROWS
    last_start = jnp.maximum(pends[-1] - MOE_ROWS, 0)
    tile_expert = jnp.sum(jnp.minimum(tile_start, last_start)[:, None] >= pends[None, :], axis=1)
    tile_expert = jnp.minimum(tile_expert, N_EXPERTS - 1).astype(jnp.int32)

    xs = _dispatch(pends, padded, dest0, dest1, h2t, p_rows)
    ybuf = _experts(tile_expert, n_valid.reshape(1), xs, w_gate[0], w_up[0], w_down[0])
    out = _combine(dest0, dest1, rinfo, x_new, mod3, final_g.reshape(1, d), ybuf, s)
    return out.reshape(b, s, d)
```

```python
import functools

import jax
import jax.numpy as jnp
from jax import lax
from jax.experimental import pallas as pl
from jax.experimental.pallas import tpu as pltpu

F32 = jnp.float32
BF16 = jnp.bfloat16

D_MODEL = 1024
GRID_W = 64
ATTN_WIDTH = 512
N_HEADS = 8
N_KV_HEADS = 2
HEAD_DIM = 64
KV_REP = N_HEADS // N_KV_HEADS
KV_WIDTH = N_KV_HEADS * HEAD_DIM
CONV_WIDTH = 512
IN_COLS = ATTN_WIDTH + 2 * KV_WIDTH + 3 * CONV_WIDTH
ROPE_THETA = 10000.0
ROPE_AXIS_DIM = HEAD_DIM // 2
ROPE_FREQS = ROPE_AXIS_DIM // 2
N_GROUPS = 4
EXPERTS_PER_GROUP = 8
N_EXPERTS = N_GROUPS * EXPERTS_PER_GROUP
TOP_K = 2
D_EXPERT = 768
N_MOD = 6
EPS = 1e-6
LOG2_E = 1.4426950408889634
Q_SCALE = HEAD_DIM ** -0.5 * LOG2_E
V_LANES = 2 * HEAD_DIM

LANES = 128
ROW_TILE = 8
MOD_ROWS = 16
IN_ROWS = 512
CONV_PAD = 8
ATTN_TQ = 128
ATTN_HEAD_SPLIT = 2
OUT_ROWS = 512
DISPATCH_ROWS = 1024
DMA_UNROLL = 8
MOE_ROWS = 256
COMBINE_ROWS = 256
VMEM_LIMIT = 56 * 1024 * 1024


def _params(semantics, vmem=None):
    return pltpu.CompilerParams(dimension_semantics=semantics,
                                vmem_limit_bytes=vmem if vmem else VMEM_LIMIT)


def _rms(x):
    return x * lax.rsqrt(jnp.mean(x * x, axis=-1, keepdims=True) + EPS)


def _store_row_tiles(ref, val):
    rows, width = val.shape
    assert width == ROW_TILE * LANES
    for i in range(ROW_TILE):
        ref[pl.ds(i, rows, stride=ROW_TILE), :] = val[:, i * LANES:(i + 1) * LANES]


def _load_row_tiles(ref, rows, *lead):
    return jnp.concatenate(
        [ref[(*lead, pl.ds(i, rows, stride=ROW_TILE), slice(None))] for i in range(ROW_TILE)], axis=1)


def _row_tile(ref, row, *lead):
    return ref.at[(*lead, pl.ds(pl.multiple_of(row * ROW_TILE, ROW_TILE), ROW_TILE))]


def _mod_kernel(c_ref, w_ref, b_ref, o_ref):
    c = c_ref[...]
    s = c * jax.nn.sigmoid(c)
    o_ref[...] = jnp.dot(s.astype(BF16), w_ref[...].astype(BF16),
                         preferred_element_type=F32) + b_ref[...]


def _modulation(cond, w_mod, b_mod):
    d = cond.shape[1]
    n_out = w_mod.shape[1]
    tn = 1024
    return pl.pallas_call(
        _mod_kernel,
        grid=(n_out // tn,),
        in_specs=[pl.BlockSpec((MOD_ROWS, d), lambda j: (0, 0)),
                  pl.BlockSpec((d, tn), lambda j: (0, j)),
                  pl.BlockSpec((1, tn), lambda j: (0, j))],
        out_specs=pl.BlockSpec((MOD_ROWS, tn), lambda j: (0, j)),
        out_shape=jax.ShapeDtypeStruct((MOD_ROWS, n_out), F32),
        compiler_params=_params(("arbitrary",)),
        name="modulation",
    )(cond, w_mod, b_mod.reshape(1, n_out))


def _with_ones(v):
    return jnp.concatenate([v, jnp.ones_like(v)], axis=1).astype(BF16)


def _rope(x, cos, sin_signed, is_lo):
    partner = jnp.where(is_lo, pltpu.roll(x, LANES - ROPE_FREQS, 1), pltpu.roll(x, ROPE_FREQS, 1))
    return x * cos + partner * sin_signed


def _inproj_kernel(x_ref, ctx_ref, mod_ref, cmod_ref, n1g_ref, w_ref, qg_ref, kg_ref,
                   cos_ref, sin_ref, convw_ref, cog_ref, bd_ref,
                   q_ref, k_ref, v_ref, cv_ref, p_s, gb_s):
    seq = x_ref.shape[1]
    ctx_len = ctx_ref.shape[1]
    g1 = n1g_ref[...]
    bd = bd_ref[...]
    bd_kv = bd_ref[0:KV_WIDTH, 0:KV_WIDTH]
    kg = kg_ref[...]
    qg = qg_ref[...]

    def head_ms(z, m):
        return jnp.dot((z * z).astype(BF16), m, preferred_element_type=F32)

    cmod = cmod_ref[0]
    hc = _rms(ctx_ref[0]) * g1 * (1.0 + cmod[1:2]) + cmod[0:1]
    zc = jnp.dot(hc.astype(BF16), w_ref[:, ATTN_WIDTH:ATTN_WIDTH + 2 * KV_WIDTH],
                 preferred_element_type=F32)
    kc = zc[:, :KV_WIDTH]
    kc = kc * lax.rsqrt(head_ms(kc, bd_kv) + EPS) * kg
    vc = zc[:, KV_WIDTH:]
    for g in range(N_KV_HEADS):
        k_ref[0, g, 0:ctx_len, :] = kc[:, g * HEAD_DIM:(g + 1) * HEAD_DIM].astype(BF16)
        v_ref[0, g, 0:ctx_len, :] = _with_ones(vc[:, g * HEAD_DIM:(g + 1) * HEAD_DIM])

    zeros = jnp.zeros((CONV_PAD, CONV_WIDTH), F32)
    p_s[0:CONV_PAD, :] = zeros
    p_s[CONV_PAD + seq:CONV_PAD + seq + CONV_PAD, :] = zeros

    mod = mod_ref[0]
    sh1 = mod[0:1]
    sc1 = mod[1:2]
    lane = lax.broadcasted_iota(jnp.int32, (IN_ROWS, LANES), 1)
    is_lo = (lane // ROPE_FREQS) % 2 == 0

    def proj_chunk(c, carry):
        r0 = pl.multiple_of(c * IN_ROWS, IN_ROWS)
        h = (_rms(x_ref[0, pl.ds(r0, IN_ROWS), :]) * g1 * (1.0 + sc1) + sh1).astype(BF16)
        cos = cos_ref[pl.ds(r0, IN_ROWS), :]
        sin = sin_ref[pl.ds(r0, IN_ROWS), :]
        zq = jnp.dot(h, w_ref[:, 0:ATTN_WIDTH], preferred_element_type=F32)
        qn = zq * lax.rsqrt(head_ms(zq, bd) + EPS) * qg
        for j in range(ATTN_WIDTH // LANES):
            blk = _rope(qn[:, j * LANES:(j + 1) * LANES], cos, sin, is_lo) * Q_SCALE
            for hh in range(LANES // HEAD_DIM):
                head = j * (LANES // HEAD_DIM) + hh
                q_ref[0, head, pl.ds(r0, IN_ROWS), :] = (
                    blk[:, hh * HEAD_DIM:(hh + 1) * HEAD_DIM].astype(BF16))
        zkv = jnp.dot(h, w_ref[:, ATTN_WIDTH:ATTN_WIDTH + 2 * KV_WIDTH], preferred_element_type=F32)
        kx = zkv[:, :KV_WIDTH]
        kx = _rope(kx * lax.rsqrt(head_ms(kx, bd_kv) + EPS) * kg, cos, sin, is_lo)
        vx = zkv[:, KV_WIDTH:]
        for g in range(N_KV_HEADS):
            k_ref[0, g, pl.ds(ctx_len + r0, IN_ROWS), :] = kx[:, g * HEAD_DIM:(g + 1) * HEAD_DIM].astype(BF16)
            v_ref[0, g, pl.ds(ctx_len + r0, IN_ROWS), :] = _with_ones(vx[:, g * HEAD_DIM:(g + 1) * HEAD_DIM])
        c0 = ATTN_WIDTH + 2 * KV_WIDTH
        gb_s[pl.ds(r0, IN_ROWS), :] = jnp.dot(h, w_ref[:, c0:c0 + CONV_WIDTH], preferred_element_type=F32)
        zc_ = jnp.dot(h, w_ref[:, c0 + CONV_WIDTH:c0 + 2 * CONV_WIDTH], preferred_element_type=F32)
        zu = jnp.dot(h, w_ref[:, c0 + 2 * CONV_WIDTH:c0 + 3 * CONV_WIDTH], preferred_element_type=F32)
        p_s[pl.ds(CONV_PAD + r0, IN_ROWS), :] = zc_ * zu
        return carry

    lax.fori_loop(0, seq // IN_ROWS, proj_chunk, 0)

    cw = convw_ref[...]
    cog = cog_ref[...]

    def conv_chunk(c, carry):
        r0 = pl.multiple_of(c * IN_ROWS, IN_ROWS)
        win = p_s[pl.ds(r0, IN_ROWS + 2 * CONV_PAD), :]
        n_win = IN_ROWS + 2 * CONV_PAD
        prev = pltpu.roll(win, 1, 0)[CONV_PAD:CONV_PAD + IN_ROWS]
        cur = win[CONV_PAD:CONV_PAD + IN_ROWS]
        nxt = pltpu.roll(win, n_win - 1, 0)[CONV_PAD:CONV_PAD + IN_ROWS]
        y = cw[0:1] * prev + cw[1:2] * cur + cw[2:3] * nxt
        cvv = gb_s[pl.ds(r0, IN_ROWS), :] * y
        cvn = cvv * lax.rsqrt(head_ms(cvv, bd) + EPS) * cog
        cv_ref[0, pl.ds(r0, IN_ROWS), :] = cvn.astype(BF16)
        return carry

    lax.fori_loop(0, seq // IN_ROWS, conv_chunk, 0)


def _in_projection(x, ctx, mod3, norm1_g, w_in_bf, qg_t, kg_t, cos_t, sin_t, conv_w, conv_out_g, bd):
    b, s, d = x.shape
    ctx_len = ctx.shape[1]
    n_keys = ctx_len + s
    const = lambda *shape: pl.BlockSpec(shape, lambda i: (0,) * len(shape))
    return pl.pallas_call(
        _inproj_kernel,
        grid=(b,),
        in_specs=[
            pl.BlockSpec((1, s, d), lambda i: (i, 0, 0)),
            pl.BlockSpec((1, ctx_len, d), lambda i: (i, 0, 0)),
            pl.BlockSpec((1, N_MOD, d), lambda i: (i, 0, 0)),
            pl.BlockSpec((1, N_MOD, d), lambda i: (b, 0, 0)),
            const(1, d),
            const(d, IN_COLS),
            const(1, ATTN_WIDTH),
            const(1, KV_WIDTH),
            const(s, LANES),
            const(s, LANES),
            const(3, CONV_WIDTH),
            const(1, CONV_WIDTH),
            const(ATTN_WIDTH, ATTN_WIDTH),
        ],
        out_specs=[
            pl.BlockSpec((1, N_HEADS, s, HEAD_DIM), lambda i: (i, 0, 0, 0)),
            pl.BlockSpec((1, N_KV_HEADS, n_keys, HEAD_DIM), lambda i: (i, 0, 0, 0)),
            pl.BlockSpec((1, N_KV_HEADS, n_keys, V_LANES), lambda i: (i, 0, 0, 0)),
            pl.BlockSpec((1, s, CONV_WIDTH), lambda i: (i, 0, 0)),
        ],
        out_shape=[
            jax.ShapeDtypeStruct((b, N_HEADS, s, HEAD_DIM), BF16),
            jax.ShapeDtypeStruct((b, N_KV_HEADS, n_keys, HEAD_DIM), BF16),
            jax.ShapeDtypeStruct((b, N_KV_HEADS, n_keys, V_LANES), BF16),
            jax.ShapeDtypeStruct((b, s, CONV_WIDTH), BF16),
        ],
        scratch_shapes=[pltpu.VMEM((s + 2 * CONV_PAD, CONV_WIDTH), F32),
                        pltpu.VMEM((s, CONV_WIDTH), F32)],
        compiler_params=_params(("arbitrary",)),
        name="in_projection",
    )(x, ctx, mod3, mod3, norm1_g, w_in_bf, qg_t, kg_t, cos_t, sin_t, conv_w, conv_out_g, bd)


def _attn_kernel(q_ref, k_ref, v_ref, g_ref, o_ref):
    tq = q_ref.shape[2]
    pieces = []
    hs = KV_REP // ATTN_HEAD_SPLIT
    subs = [(g, g * KV_REP + j * hs) for g in range(N_KV_HEADS) for j in range(ATTN_HEAD_SPLIT)]
    scores = []
    for g, h0 in subs:
        qs = jnp.concatenate([q_ref[0, h0 + r] for r in range(hs)], axis=0)
        scores.append(lax.dot_general(qs, k_ref[0, g], (((1,), (1,)), ((), ())),
                                      preferred_element_type=F32))
    probs = [jnp.exp2(s - jnp.max(s, axis=-1, keepdims=True)).astype(BF16) for s in scores]
    lane = lax.broadcasted_iota(jnp.int32, (hs * tq, LANES), 1)
    for (g, h0), p in zip(subs, probs):
        ov = jnp.dot(p, v_ref[0, g], preferred_element_type=F32)
        o = ov / pltpu.roll(ov, HEAD_DIM, 1)
        ms = jnp.sum(jnp.where(lane < HEAD_DIM, o * o, 0.0), axis=-1, keepdims=True) * (1.0 / HEAD_DIM)
        o = o * lax.rsqrt(ms + EPS)
        for r in range(hs):
            pieces.append(o[r * tq:(r + 1) * tq, 0:HEAD_DIM])
    o_ref[0] = (jnp.concatenate(pieces, axis=1) * g_ref[...]).astype(BF16)


def _attention(q, k, v, attn_out_g):
    b, _, s, _ = q.shape
    n_keys = k.shape[2]
    return pl.pallas_call(
        _attn_kernel,
        grid=(b, s // ATTN_TQ),
        in_specs=[
            pl.BlockSpec((1, N_HEADS, ATTN_TQ, HEAD_DIM), lambda i, j: (i, 0, j, 0)),
            pl.BlockSpec((1, N_KV_HEADS, n_keys, HEAD_DIM), lambda i, j: (i, 0, 0, 0)),
            pl.BlockSpec((1, N_KV_HEADS, n_keys, V_LANES), lambda i, j: (i, 0, 0, 0)),
            pl.BlockSpec((1, ATTN_WIDTH), lambda i, j: (0, 0)),
        ],
        out_specs=pl.BlockSpec((1, ATTN_TQ, ATTN_WIDTH), lambda i, j: (i, j, 0)),
        out_shape=jax.ShapeDtypeStruct((b, s, ATTN_WIDTH), BF16),
        compiler_params=_params(("arbitrary", "arbitrary")),
        name="attention",
    )(q, k, v, attn_out_g)


def _outproj_kernel(a_ref, cv_ref, w_ref, x_ref, mod_ref, n2g_ref, wr_ref, tri_ref,
                    xn_ref, h2_ref, ri_ref, rit_ref, cnt_ref, carry_s):
    i = pl.program_id(0)

    @pl.when(i == 0)
    def _():
        carry_s[...] = jnp.zeros_like(carry_s)

    mod = mod_ref[0]
    merged = (jnp.dot(a_ref[...], w_ref[0:ATTN_WIDTH, :], preferred_element_type=F32)
              + jnp.dot(cv_ref[...], w_ref[ATTN_WIDTH:, :], preferred_element_type=F32))
    xn = x_ref[...] + mod[2:3] * merged
    xn_ref[...] = xn
    h2 = _rms(xn) * n2g_ref[...] * (1.0 + mod[4:5]) + mod[3:4]
    _store_row_tiles(h2_ref, h2)

    logits = jnp.dot(h2.astype(BF16), wr_ref[...], preferred_element_type=F32)
    rows = logits.shape[0]
    li = lax.broadcasted_iota(jnp.int32, (rows, LANES), 1)
    neg = jnp.float32(-jnp.inf)

    gmask = (li >= N_EXPERTS) & (li < N_EXPERTS + N_GROUPS)
    lg = jnp.where(gmask, logits, neg)
    ge = jnp.exp(lg - jnp.max(lg, axis=-1, keepdims=True))
    g_prob = ge / jnp.sum(ge, axis=-1, keepdims=True)
    g_w = jnp.max(g_prob, axis=-1, keepdims=True)
    g_sel = jnp.min(jnp.where(gmask & (g_prob == g_w), li, 2 * LANES), axis=-1, keepdims=True) - N_EXPERTS

    emask = (li < N_EXPERTS) & (li // EXPERTS_PER_GROUP == g_sel)
    le = jnp.where(emask, logits, neg)
    ee = jnp.exp(le - jnp.max(le, axis=-1, keepdims=True))
    e_prob = ee / jnp.sum(ee, axis=-1, keepdims=True)
    p1 = jnp.max(jnp.where(emask, e_prob, -1.0), axis=-1, keepdims=True)
    i1 = jnp.min(jnp.where(emask & (e_prob == p1), li, 2 * LANES), axis=-1, keepdims=True)
    mask2 = emask & (li != i1)
    p2 = jnp.max(jnp.where(mask2, e_prob, -1.0), axis=-1, keepdims=True)
    i2 = jnp.min(jnp.where(mask2 & (e_prob == p2), li, 2 * LANES), axis=-1, keepdims=True)
    psum = p1 + p2
    w1 = g_w * (p1 / psum)
    w2 = g_w * (p2 / psum)

    hit1 = li == i1
    hit2 = li == i2
    onehot = jnp.where(hit1 | hit2, 1.0, 0.0)
    before = jnp.dot(tri_ref[...], onehot.astype(BF16), preferred_element_type=F32) + carry_s[...]
    r1 = jnp.sum(jnp.where(hit1, before, 0.0), axis=-1, keepdims=True)
    r2 = jnp.sum(jnp.where(hit2, before, 0.0), axis=-1, keepdims=True)
    carry_s[...] = carry_s[...] + jnp.sum(onehot, axis=0, keepdims=True)

    info = jnp.where(li == 0, i1.astype(F32),
           jnp.where(li == 1, i2.astype(F32),
           jnp.where(li == 2, r1,
           jnp.where(li == 3, r2,
           jnp.where(li == 4, w1,
           jnp.where(li == 5, w2, 0.0))))))
    ri_ref[...] = info
    rit_ref[...] = jnp.transpose(info)[0:ROW_TILE, :]
    cnt_ref[...] = carry_s[...]


def _out_projection(a, cv, w_out_bf, x2, mod3, norm2_g, wr_bf, tri, seq):
    n, d = x2.shape
    tiles_per_batch = seq // OUT_ROWS
    return pl.pallas_call(
        _outproj_kernel,
        grid=(n // OUT_ROWS,),
        in_specs=[
            pl.BlockSpec((OUT_ROWS, ATTN_WIDTH), lambda i: (i, 0)),
            pl.BlockSpec((OUT_ROWS, CONV_WIDTH), lambda i: (i, 0)),
            pl.BlockSpec((d, d), lambda i: (0, 0)),
            pl.BlockSpec((OUT_ROWS, d), lambda i: (i, 0)),
            pl.BlockSpec((1, N_MOD, d), lambda i: (i // tiles_per_batch, 0, 0)),
            pl.BlockSpec((1, d), lambda i: (0, 0)),
            pl.BlockSpec((d, LANES), lambda i: (0, 0)),
            pl.BlockSpec((OUT_ROWS, OUT_ROWS), lambda i: (0, 0)),
        ],
        out_specs=[
            pl.BlockSpec((OUT_ROWS, d), lambda i: (i, 0)),
            pl.BlockSpec((OUT_ROWS * ROW_TILE, LANES), lambda i: (i, 0)),
            pl.BlockSpec((OUT_ROWS, LANES), lambda i: (i, 0)),
            pl.BlockSpec((ROW_TILE, OUT_ROWS), lambda i: (0, i)),
            pl.BlockSpec((1, LANES), lambda i: (0, 0)),
        ],
        out_shape=[
            jax.ShapeDtypeStruct((n, d), F32),
            jax.ShapeDtypeStruct((n * ROW_TILE, LANES), F32),
            jax.ShapeDtypeStruct((n, LANES), F32),
            jax.ShapeDtypeStruct((ROW_TILE, n), F32),
            jax.ShapeDtypeStruct((1, LANES), F32),
        ],
        scratch_shapes=[pltpu.VMEM((1, LANES), F32)],
        compiler_params=_params(("arbitrary",)),
        name="out_projection_routing",
    )(a, cv, w_out_bf, x2, mod3, norm2_g, wr_bf, tri)


def _dispatch_kernel(pend_ref, padded_ref, d0_ref, d1_ref, h_ref, xs_hbm, zero_s, sem):
    tile = MOE_ROWS * ROW_TILE

    @pl.when(pl.program_id(0) == 0)
    def _():
        zero_s[...] = jnp.zeros_like(zero_s)

        def last_tile(e):
            start = pl.multiple_of((pend_ref[e] - MOE_ROWS) * ROW_TILE, tile)
            return pltpu.make_async_copy(zero_s, xs_hbm.at[pl.ds(start, tile)], sem)

        def spare_tile(j):
            return pltpu.make_async_copy(zero_s, xs_hbm.at[pl.ds(j * tile, tile)], sem)

        n_tiles = xs_hbm.shape[0] // tile
        used = pend_ref[N_EXPERTS - 1] // MOE_ROWS
        for e in range(N_EXPERTS):
            @pl.when(padded_ref[e] > 0)
            def _():
                last_tile(e).start()
        for j in range(n_tiles - N_EXPERTS, n_tiles):
            @pl.when(j >= used)
            def _():
                spare_tile(j).start()
        for e in range(N_EXPERTS):
            @pl.when(padded_ref[e] > 0)
            def _():
                last_tile(e).wait()
        for j in range(n_tiles - N_EXPERTS, n_tiles):
            @pl.when(j >= used)
            def _():
                spare_tile(j).wait()

    def start(j, carry):
        for u in range(DMA_UNROLL):
            t = j * DMA_UNROLL + u
            for k, d_ref in enumerate((d0_ref, d1_ref)):
                pltpu.make_async_copy(_row_tile(h_ref, t), _row_tile(xs_hbm, d_ref[0, 0, t]),
                                      sem).start(priority=k)
        return carry

    lax.fori_loop(0, DISPATCH_ROWS // DMA_UNROLL, start, 0)

    def drain(j, carry):
        for _ in range(DMA_UNROLL * TOP_K):
            pltpu.make_async_copy(_row_tile(h_ref, 0), _row_tile(xs_hbm, 0), sem).wait()
        return carry

    lax.fori_loop(0, DISPATCH_ROWS // DMA_UNROLL, drain, 0)


def _dispatch(pends, padded, dest0, dest1, h2t, p_rows):
    n = h2t.shape[0] // ROW_TILE
    steps = n // DISPATCH_ROWS
    smem_rows = lambda: pl.BlockSpec((1, 1, DISPATCH_ROWS), lambda i, pe, pa: (i, 0, 0),
                                     memory_space=pltpu.SMEM)
    return pl.pallas_call(
        _dispatch_kernel,
        grid_spec=pltpu.PrefetchScalarGridSpec(
            num_scalar_prefetch=2,
            grid=(steps,),
            in_specs=[
                smem_rows(),
                smem_rows(),
                pl.BlockSpec((DISPATCH_ROWS * ROW_TILE, LANES), lambda i, pe, pa: (i, 0)),
            ],
            out_specs=pl.BlockSpec(memory_space=pl.ANY),
            scratch_shapes=[pltpu.VMEM((MOE_ROWS * ROW_TILE, LANES), F32), pltpu.SemaphoreType.DMA(())],
        ),
        out_shape=jax.ShapeDtypeStruct((p_rows * ROW_TILE, LANES), F32),
        compiler_params=_params(("arbitrary",)),
        name="moe_dispatch",
    )(pends, padded, dest0.reshape(steps, 1, DISPATCH_ROWS), dest1.reshape(steps, 1, DISPATCH_ROWS), h2t)


def _experts_kernel(first_ref, count_ref, xs_hbm, wg_ref, wu_ref, wd_ref, y_hbm,
                    x_s, y_s, wg_s, wu_s, wd_s, in_sem, out_sem):
    e = pl.program_id(0)
    n = count_ref[e]
    first = first_ref[e]
    tile = MOE_ROWS * ROW_TILE

    def rows(j):
        return pl.ds(pl.multiple_of((first + j) * tile, tile), tile)

    def fetch(j, slot):
        return pltpu.make_async_copy(xs_hbm.at[rows(j)], x_s.at[slot], in_sem.at[slot])

    def writeback(j, slot):
        return pltpu.make_async_copy(y_s.at[slot], y_hbm.at[rows(j)], out_sem.at[slot])

    @pl.when(n > 0)
    def _():
        fetch(0, 0).start()
        wg_s[...] = wg_ref[0].astype(BF16)
        wu_s[...] = wu_ref[0].astype(BF16)
        wd_s[...] = wd_ref[0].astype(BF16)

        def tile_step(j, carry):
            slot = j % 2

            @pl.when(j + 1 < n)
            def _():
                fetch(j + 1, 1 - slot).start()

            fetch(j, slot).wait()

            @pl.when(j >= 2)
            def _():
                writeback(j - 2, slot).wait()

            x = _load_row_tiles(x_s, MOE_ROWS, slot).astype(BF16)
            g = jnp.dot(x, wg_s[...], preferred_element_type=F32)
            u = jnp.dot(x, wu_s[...], preferred_element_type=F32)
            h = ((g * jax.nn.sigmoid(g)) * u).astype(BF16)
            _store_row_tiles(y_s.at[slot], jnp.dot(h, wd_s[...], preferred_element_type=F32))
            writeback(j, slot).start()
            return carry

        lax.fori_loop(0, n, tile_step, 0)

        @pl.when(n >= 2)
        def _():
            writeback(n - 2, n % 2).wait()

        writeback(n - 1, (n - 1) % 2).wait()


def _experts(first_tile, tile_count, xs, w_gate, w_up, w_down):
    n_exp, d, d_exp = w_gate.shape
    tile = MOE_ROWS * ROW_TILE
    w_map = lambda i, ft, tc: (i, 0, 0)
    return pl.pallas_call(
        _experts_kernel,
        grid_spec=pltpu.PrefetchScalarGridSpec(
            num_scalar_prefetch=2,
            grid=(n_exp,),
            in_specs=[
                pl.BlockSpec(memory_space=pl.ANY),
                pl.BlockSpec((1, d, d_exp), w_map),
                pl.BlockSpec((1, d, d_exp), w_map),
                pl.BlockSpec((1, d_exp, d), w_map),
            ],
            out_specs=pl.BlockSpec(memory_space=pl.ANY),
            scratch_shapes=[
                pltpu.VMEM((2, tile, LANES), F32),
                pltpu.VMEM((2, tile, LANES), F32),
                pltpu.VMEM((d, d_exp), BF16),
                pltpu.VMEM((d, d_exp), BF16),
                pltpu.VMEM((d_exp, d), BF16),
                pltpu.SemaphoreType.DMA((2,)),
                pltpu.SemaphoreType.DMA((2,)),
            ],
        ),
        out_shape=jax.ShapeDtypeStruct(xs.shape, F32),
        input_output_aliases={2: 0},
        compiler_params=_params(("arbitrary",)),
        name="moe_experts",
    )(first_tile, tile_count, xs, w_gate, w_up, w_down)


def _combine_kernel(d0_ref, d1_ref, d0n_ref, d1n_ref, ri_ref, xn_ref, mod_ref, fg_ref, y_hbm,
                    o_ref, buf, sem):
    i = pl.program_id(0)
    slot = i % 2

    def gather(refs, to_slot):
        def start(j, carry):
            for u in range(DMA_UNROLL):
                t = j * DMA_UNROLL + u
                for k, d_ref in enumerate(refs):
                    pltpu.make_async_copy(_row_tile(y_hbm, d_ref[0, 0, t]), _row_tile(buf, t, to_slot, k),
                                          sem.at[to_slot]).start(priority=k)
            return carry

        lax.fori_loop(0, COMBINE_ROWS // DMA_UNROLL, start, 0)

    @pl.when(i == 0)
    def _():
        gather((d0_ref, d1_ref), 0)

    @pl.when(i + 1 < pl.num_programs(0))
    def _():
        gather((d0n_ref, d1n_ref), 1 - slot)

    def drain(j, carry):
        for _ in range(DMA_UNROLL * TOP_K):
            pltpu.make_async_copy(_row_tile(y_hbm, 0), _row_tile(buf, 0, slot, 0), sem.at[slot]).wait()
        return carry

    lax.fori_loop(0, COMBINE_ROWS // DMA_UNROLL, drain, 0)

    ri = ri_ref[...]
    y = (_load_row_tiles(buf, COMBINE_ROWS, slot, 0) * ri[:, 4:5]
         + _load_row_tiles(buf, COMBINE_ROWS, slot, 1) * ri[:, 5:6])
    xf = xn_ref[...] + mod_ref[0][5:6] * y
    o_ref[...] = _rms(xf) * fg_ref[...]


def _combine(dest0, dest1, rinfo, x_new, mod3, final_g, ybuf, seq):
    n, d = x_new.shape
    steps = n // COMBINE_ROWS
    tiles_per_batch = seq // COMBINE_ROWS
    cur = lambda: pl.BlockSpec((1, 1, COMBINE_ROWS), lambda i: (i, 0, 0), memory_space=pltpu.SMEM)
    nxt = lambda: pl.BlockSpec((1, 1, COMBINE_ROWS), lambda i: (jnp.minimum(i + 1, steps - 1), 0, 0),
                               memory_space=pltpu.SMEM)
    d0 = dest0.reshape(steps, 1, COMBINE_ROWS)
    d1 = dest1.reshape(steps, 1, COMBINE_ROWS)
    return pl.pallas_call(
        _combine_kernel,
        grid=(steps,),
        in_specs=[
            cur(), cur(), nxt(), nxt(),
            pl.BlockSpec((COMBINE_ROWS, LANES), lambda i: (i, 0)),
            pl.BlockSpec((COMBINE_ROWS, d), lambda i: (i, 0)),
            pl.BlockSpec((1, N_MOD, d), lambda i: (i // tiles_per_batch, 0, 0)),
            pl.BlockSpec((1, d), lambda i: (0, 0)),
            pl.BlockSpec(memory_space=pl.ANY),
        ],
        out_specs=pl.BlockSpec((COMBINE_ROWS, d), lambda i: (i, 0)),
        out_shape=jax.ShapeDtypeStruct((n, d), F32),
        scratch_shapes=[pltpu.VMEM((2, TOP_K, COMBINE_ROWS * ROW_TILE, LANES), F32),
                        pltpu.SemaphoreType.DMA((2,))],
        compiler_params=_params(("arbitrary",)),
        name="moe_combine",
    )(d0, d1, d0, d1, rinfo, x_new, mod3, final_g, ybuf)


def _rope_tables(seq):
    rows = seq // GRID_W
    row_idx = jnp.repeat(jnp.arange(rows, dtype=F32), GRID_W)
    col_idx = jnp.tile(jnp.arange(GRID_W, dtype=F32), rows)
    inv_freq = ROPE_THETA ** (-jnp.arange(0, ROPE_AXIS_DIM, 2, dtype=F32) / ROPE_AXIS_DIM)
    ang = jnp.stack([row_idx[:, None] * inv_freq, col_idx[:, None] * inv_freq], axis=1)
    cos = jnp.cos(ang)
    sin = jnp.sin(ang)
    cos_h = jnp.stack([cos, cos], axis=2).reshape(seq, HEAD_DIM)
    sin_h = jnp.stack([-sin, sin], axis=2).reshape(seq, HEAD_DIM)
    reps = LANES // HEAD_DIM
    return jnp.tile(cos_h, (1, reps)), jnp.tile(sin_h, (1, reps))


def kernel(x, c, ctx, c_ctx, w_mod, b_mod, norm1_g, w_in, q_norm_g, k_norm_g, conv_w, attn_out_g,
           conv_out_g, w_out, norm2_g, w_group, w_router, w_gate, w_up, w_down, final_g):
    assert w_mod.shape[0] == 1, "single-layer block"
    b, s, d = x.shape
    n = b * s
    assert b + 1 <= MOD_ROWS

    cond = jnp.zeros((MOD_ROWS, d), F32).at[:b].set(c).at[b].set(c_ctx)
    mod3 = _modulation(cond, w_mod[0], b_mod[0]).reshape(MOD_ROWS, N_MOD, d)

    cos_t, sin_t = _rope_tables(s)
    head_of = jnp.arange(ATTN_WIDTH) // HEAD_DIM
    bd = jnp.where(head_of[:, None] == head_of[None, :], 1.0 / HEAD_DIM, 0.0).astype(BF16)
    q, k, v, cv = _in_projection(
        x, ctx, mod3, norm1_g, w_in[0].astype(BF16),
        jnp.tile(q_norm_g[0], N_HEADS)[None], jnp.tile(k_norm_g[0], N_KV_HEADS)[None],
        cos_t, sin_t, conv_w[0], conv_out_g, bd)

    a = _attention(q, k, v, attn_out_g)

    wr = jnp.zeros((d, LANES), F32).at[:, :N_EXPERTS].set(w_router[0])
    wr = wr.at[:, N_EXPERTS:N_EXPERTS + N_GROUPS].set(w_group[0]).astype(BF16)
    ti = jnp.arange(OUT_ROWS)
    tri = (ti[None, :] < ti[:, None]).astype(BF16)
    x_new, h2t, rinfo, rinfo_t, counts = _out_projection(
        a.reshape(n, ATTN_WIDTH), cv.reshape(n, CONV_WIDTH), w_out[0].astype(BF16),
        x.reshape(n, d), mod3, norm2_g, wr, tri, s)

    cnt = counts[0, :N_EXPERTS].astype(jnp.int32)
    padded = ((cnt + MOE_ROWS - 1) // MOE_ROWS) * MOE_ROWS
    pends = jnp.cumsum(padded)
    pstarts = pends - padded
    experts = jnp.arange(N_EXPERTS, dtype=jnp.int32)[:, None]

    def slots(k):
        eid = rinfo_t[k].astype(jnp.int32)
        rank = rinfo_t[TOP_K + k].astype(jnp.int32)
        return jnp.sum(jnp.where(eid[None, :] == experts, pstarts[:, None], 0), axis=0) + rank

    dest0, dest1 = slots(0), slots(1)
    p_rows = n * TOP_K + N_EXPERTS * MOE_ROWS

    xs = _dispatch(pends, padded, dest0, dest1, h2t, p_rows)
    ybuf = _experts(pstarts // MOE_ROWS, padded // MOE_ROWS, xs, w_gate[0], w_up[0], w_down[0])
    out = _combine(dest0, dest1, rinfo, x_new, mod3, final_g.reshape(1, d), ybuf, s)
    return out.reshape(b, s, d)
```

```python
import functools

import jax
import jax.numpy as jnp
from jax import lax
from jax.experimental import pallas as pl
from jax.experimental.pallas import tpu as pltpu

F32 = jnp.float32
BF16 = jnp.bfloat16

D_MODEL = 1024
GRID_W = 64
ATTN_WIDTH = 512
N_HEADS = 8
N_KV_HEADS = 2
HEAD_DIM = 64
KV_REP = N_HEADS // N_KV_HEADS
KV_WIDTH = N_KV_HEADS * HEAD_DIM
CONV_WIDTH = 512
IN_COLS = ATTN_WIDTH + 2 * KV_WIDTH + 3 * CONV_WIDTH
ROPE_THETA = 10000.0
ROPE_AXIS_DIM = HEAD_DIM // 2
ROPE_FREQS = ROPE_AXIS_DIM // 2
N_GROUPS = 4
EXPERTS_PER_GROUP = 8
N_EXPERTS = N_GROUPS * EXPERTS_PER_GROUP
TOP_K = 2
D_EXPERT = 768
N_MOD = 6
EPS = 1e-6
LOG2_E = 1.4426950408889634
Q_SCALE = HEAD_DIM ** -0.5 * LOG2_E
V_LANES = 2 * HEAD_DIM

LANES = 128
ROW_TILE = 8
MOD_ROWS = 16
IN_ROWS = 512
CONV_PAD = 8
ATTN_TQ = 128
ATTN_HEAD_SPLIT = 2
OUT_ROWS = 512
DISPATCH_ROWS = 1024
DMA_UNROLL = 8
MOE_ROWS = 256
COMBINE_ROWS = 256
VMEM_LIMIT = 56 * 1024 * 1024


def _params(semantics, vmem=None):
    return pltpu.CompilerParams(dimension_semantics=semantics,
                                vmem_limit_bytes=vmem if vmem else VMEM_LIMIT)


def _rms(x):
    return x * lax.rsqrt(jnp.mean(x * x, axis=-1, keepdims=True) + EPS)


def _store_row_tiles(ref, val):
    rows, width = val.shape
    assert width == ROW_TILE * LANES
    for i in range(ROW_TILE):
        ref[pl.ds(i, rows, stride=ROW_TILE), :] = val[:, i * LANES:(i + 1) * LANES]


def _load_row_tiles(ref, rows, *lead):
    return jnp.concatenate(
        [ref[(*lead, pl.ds(i, rows, stride=ROW_TILE), slice(None))] for i in range(ROW_TILE)], axis=1)


def _row_tile(ref, row, *lead):
    return ref.at[(*lead, pl.ds(pl.multiple_of(row * ROW_TILE, ROW_TILE), ROW_TILE))]


def _mod_kernel(c_ref, w_ref, b_ref, o_ref):
    c = c_ref[...]
    s = c * jax.nn.sigmoid(c)
    o_ref[...] = jnp.dot(s.astype(BF16), w_ref[...].astype(BF16),
                         preferred_element_type=F32) + b_ref[...]


def _modulation(cond, w_mod, b_mod):
    d = cond.shape[1]
    n_out = w_mod.shape[1]
    tn = 1024
    return pl.pallas_call(
        _mod_kernel,
        grid=(n_out // tn,),
        in_specs=[pl.BlockSpec((MOD_ROWS, d), lambda j: (0, 0)),
                  pl.BlockSpec((d, tn), lambda j: (0, j)),
                  pl.BlockSpec((1, tn), lambda j: (0, j))],
        out_specs=pl.BlockSpec((MOD_ROWS, tn), lambda j: (0, j)),
        out_shape=jax.ShapeDtypeStruct((MOD_ROWS, n_out), F32),
        compiler_params=_params(("arbitrary",)),
        name="modulation",
    )(cond, w_mod, b_mod.reshape(1, n_out))


def _with_ones(v):
    return jnp.concatenate([v, jnp.ones_like(v)], axis=1).astype(BF16)


def _rope(x, cos, sin_signed, is_lo):
    partner = jnp.where(is_lo, pltpu.roll(x, LANES - ROPE_FREQS, 1), pltpu.roll(x, ROPE_FREQS, 1))
    return x * cos + partner * sin_signed


def _inproj_kernel(x_ref, ctx_ref, mod_ref, cmod_ref, n1g_ref, w_ref, qg_ref, kg_ref,
                   cos_ref, sin_ref, convw_ref, cog_ref, bd_ref,
                   q_ref, k_ref, v_ref, cv_ref, p_s, gb_s):
    seq = x_ref.shape[1]
    ctx_len = ctx_ref.shape[1]
    g1 = n1g_ref[...]
    bd = bd_ref[...]
    bd_kv = bd_ref[0:KV_WIDTH, 0:KV_WIDTH]
    kg = kg_ref[...]
    qg = qg_ref[...]

    def head_ms(z, m):
        return jnp.dot((z * z).astype(BF16), m, preferred_element_type=F32)

    cmod = cmod_ref[0]
    hc = _rms(ctx_ref[0]) * g1 * (1.0 + cmod[1:2]) + cmod[0:1]
    zc = jnp.dot(hc.astype(BF16), w_ref[:, ATTN_WIDTH:ATTN_WIDTH + 2 * KV_WIDTH],
                 preferred_element_type=F32)
    kc = zc[:, :KV_WIDTH]
    kc = kc * lax.rsqrt(head_ms(kc, bd_kv) + EPS) * kg
    vc = zc[:, KV_WIDTH:]
    for g in range(N_KV_HEADS):
        k_ref[0, g, 0:ctx_len, :] = kc[:, g * HEAD_DIM:(g + 1) * HEAD_DIM].astype(BF16)
        v_ref[0, g, 0:ctx_len, :] = _with_ones(vc[:, g * HEAD_DIM:(g + 1) * HEAD_DIM])

    zeros = jnp.zeros((CONV_PAD, CONV_WIDTH), F32)
    p_s[0:CONV_PAD, :] = zeros
    p_s[CONV_PAD + seq:CONV_PAD + seq + CONV_PAD, :] = zeros

    mod = mod_ref[0]
    sh1 = mod[0:1]
    sc1 = mod[1:2]
    lane = lax.broadcasted_iota(jnp.int32, (IN_ROWS, LANES), 1)
    is_lo = (lane // ROPE_FREQS) % 2 == 0

    def proj_chunk(c, carry):
        r0 = pl.multiple_of(c * IN_ROWS, IN_ROWS)
        h = (_rms(x_ref[0, pl.ds(r0, IN_ROWS), :]) * g1 * (1.0 + sc1) + sh1).astype(BF16)
        cos = cos_ref[pl.ds(r0, IN_ROWS), :]
        sin = sin_ref[pl.ds(r0, IN_ROWS), :]
        zq = jnp.dot(h, w_ref[:, 0:ATTN_WIDTH], preferred_element_type=F32)
        qn = zq * lax.rsqrt(head_ms(zq, bd) + EPS) * qg
        for j in range(ATTN_WIDTH // LANES):
            blk = _rope(qn[:, j * LANES:(j + 1) * LANES], cos, sin, is_lo) * Q_SCALE
            for hh in range(LANES // HEAD_DIM):
                head = j * (LANES // HEAD_DIM) + hh
                q_ref[0, head, pl.ds(r0, IN_ROWS), :] = (
                    blk[:, hh * HEAD_DIM:(hh + 1) * HEAD_DIM].astype(BF16))
        zkv = jnp.dot(h, w_ref[:, ATTN_WIDTH:ATTN_WIDTH + 2 * KV_WIDTH], preferred_element_type=F32)
        kx = zkv[:, :KV_WIDTH]
        kx = _rope(kx * lax.rsqrt(head_ms(kx, bd_kv) + EPS) * kg, cos, sin, is_lo)
        vx = zkv[:, KV_WIDTH:]
        for g in range(N_KV_HEADS):
            k_ref[0, g, pl.ds(ctx_len + r0, IN_ROWS), :] = kx[:, g * HEAD_DIM:(g + 1) * HEAD_DIM].astype(BF16)
            v_ref[0, g, pl.ds(ctx_len + r0, IN_ROWS), :] = _with_ones(vx[:, g * HEAD_DIM:(g + 1) * HEAD_DIM])
        c0 = ATTN_WIDTH + 2 * KV_WIDTH
        gb_s[pl.ds(r0, IN_ROWS), :] = jnp.dot(h, w_ref[:, c0:c0 + CONV_WIDTH], preferred_element_type=F32)
        zc_ = jnp.dot(h, w_ref[:, c0 + CONV_WIDTH:c0 + 2 * CONV_WIDTH], preferred_element_type=F32)
        zu = jnp.dot(h, w_ref[:, c0 + 2 * CONV_WIDTH:c0 + 3 * CONV_WIDTH], preferred_element_type=F32)
        p_s[pl.ds(CONV_PAD + r0, IN_ROWS), :] = zc_ * zu
        return carry

    lax.fori_loop(0, seq // IN_ROWS, proj_chunk, 0)

    cw = convw_ref[...]
    cog = cog_ref[...]

    def conv_chunk(c, carry):
        r0 = pl.multiple_of(c * IN_ROWS, IN_ROWS)
        win = p_s[pl.ds(r0, IN_ROWS + 2 * CONV_PAD), :]
        n_win = IN_ROWS + 2 * CONV_PAD
        prev = pltpu.roll(win, 1, 0)[CONV_PAD:CONV_PAD + IN_ROWS]
        cur = win[CONV_PAD:CONV_PAD + IN_ROWS]
        nxt = pltpu.roll(win, n_win - 1, 0)[CONV_PAD:CONV_PAD + IN_ROWS]
        y = cw[0:1] * prev + cw[1:2] * cur + cw[2:3] * nxt
        cvv = gb_s[pl.ds(r0, IN_ROWS), :] * y
        cvn = cvv * lax.rsqrt(head_ms(cvv, bd) + EPS) * cog
        cv_ref[0, pl.ds(r0, IN_ROWS), :] = cvn.astype(BF16)
        return carry

    lax.fori_loop(0, seq // IN_ROWS, conv_chunk, 0)


def _in_projection(x, ctx, mod3, norm1_g, w_in_bf, qg_t, kg_t, cos_t, sin_t, conv_w, conv_out_g, bd):
    b, s, d = x.shape
    ctx_len = ctx.shape[1]
    n_keys = ctx_len + s
    const = lambda *shape: pl.BlockSpec(shape, lambda i: (0,) * len(shape))
    return pl.pallas_call(
        _inproj_kernel,
        grid=(b,),
        in_specs=[
            pl.BlockSpec((1, s, d), lambda i: (i, 0, 0)),
            pl.BlockSpec((1, ctx_len, d), lambda i: (i, 0, 0)),
            pl.BlockSpec((1, N_MOD, d), lambda i: (i, 0, 0)),
            pl.BlockSpec((1, N_MOD, d), lambda i: (b, 0, 0)),
            const(1, d),
            const(d, IN_COLS),
            const(1, ATTN_WIDTH),
            const(1, KV_WIDTH),
            const(s, LANES),
            const(s, LANES),
            const(3, CONV_WIDTH),
            const(1, CONV_WIDTH),
            const(ATTN_WIDTH, ATTN_WIDTH),
        ],
        out_specs=[
            pl.BlockSpec((1, N_HEADS, s, HEAD_DIM), lambda i: (i, 0, 0, 0)),
            pl.BlockSpec((1, N_KV_HEADS, n_keys, HEAD_DIM), lambda i: (i, 0, 0, 0)),
            pl.BlockSpec((1, N_KV_HEADS, n_keys, V_LANES), lambda i: (i, 0, 0, 0)),
            pl.BlockSpec((1, s, CONV_WIDTH), lambda i: (i, 0, 0)),
        ],
        out_shape=[
            jax.ShapeDtypeStruct((b, N_HEADS, s, HEAD_DIM), BF16),
            jax.ShapeDtypeStruct((b, N_KV_HEADS, n_keys, HEAD_DIM), BF16),
            jax.ShapeDtypeStruct((b, N_KV_HEADS, n_keys, V_LANES), BF16),
            jax.ShapeDtypeStruct((b, s, CONV_WIDTH), BF16),
        ],
        scratch_shapes=[pltpu.VMEM((s + 2 * CONV_PAD, CONV_WIDTH), F32),
                        pltpu.VMEM((s, CONV_WIDTH), F32)],
        compiler_params=_params(("arbitrary",)),
        name="in_projection",
    )(x, ctx, mod3, mod3, norm1_g, w_in_bf, qg_t, kg_t, cos_t, sin_t, conv_w, conv_out_g, bd)


def _attn_kernel(q_ref, k_ref, v_ref, g_ref, o_ref):
    tq = q_ref.shape[2]
    pieces = []
    hs = KV_REP // ATTN_HEAD_SPLIT
    subs = [(g, g * KV_REP + j * hs) for g in range(N_KV_HEADS) for j in range(ATTN_HEAD_SPLIT)]
    scores = []
    for g, h0 in subs:
        qs = jnp.concatenate([q_ref[0, h0 + r] for r in range(hs)], axis=0)
        scores.append(lax.dot_general(qs, k_ref[0, g], (((1,), (1,)), ((), ())),
                                      preferred_element_type=F32))
    probs = [jnp.exp2(s - jnp.max(s, axis=-1, keepdims=True)).astype(BF16) for s in scores]
    lane = lax.broadcasted_iota(jnp.int32, (hs * tq, LANES), 1)
    for (g, h0), p in zip(subs, probs):
        ov = jnp.dot(p, v_ref[0, g], preferred_element_type=F32)
        o = ov / pltpu.roll(ov, HEAD_DIM, 1)
        ms = jnp.sum(jnp.where(lane < HEAD_DIM, o * o, 0.0), axis=-1, keepdims=True) * (1.0 / HEAD_DIM)
        o = o * lax.rsqrt(ms + EPS)
        for r in range(hs):
            pieces.append(o[r * tq:(r + 1) * tq, 0:HEAD_DIM])
    o_ref[0] = (jnp.concatenate(pieces, axis=1) * g_ref[...]).astype(BF16)


def _attention(q, k, v, attn_out_g):
    b, _, s, _ = q.shape
    n_keys = k.shape[2]
    return pl.pallas_call(
        _attn_kernel,
        grid=(b, s // ATTN_TQ),
        in_specs=[
            pl.BlockSpec((1, N_HEADS, ATTN_TQ, HEAD_DIM), lambda i, j: (i, 0, j, 0)),
            pl.BlockSpec((1, N_KV_HEADS, n_keys, HEAD_DIM), lambda i, j: (i, 0, 0, 0)),
            pl.BlockSpec((1, N_KV_HEADS, n_keys, V_LANES), lambda i, j: (i, 0, 0, 0)),
            pl.BlockSpec((1, ATTN_WIDTH), lambda i, j: (0, 0)),
        ],
        out_specs=pl.BlockSpec((1, ATTN_TQ, ATTN_WIDTH), lambda i, j: (i, j, 0)),
        out_shape=jax.ShapeDtypeStruct((b, s, ATTN_WIDTH), BF16),
        compiler_params=_params(("arbitrary", "arbitrary")),
        name="attention",
    )(q, k, v, attn_out_g)


def _outproj_kernel(a_ref, cv_ref, w_ref, x_ref, mod_ref, n2g_ref, wr_ref, upper_ref,
                    xn_ref, h2_ref, ri_ref, rit_ref, cnt_ref, carry_s):
    i = pl.program_id(0)

    @pl.when(i == 0)
    def _():
        carry_s[...] = jnp.zeros_like(carry_s)

    mod = mod_ref[0]
    merged = (jnp.dot(a_ref[...], w_ref[0:ATTN_WIDTH, :], preferred_element_type=F32)
              + jnp.dot(cv_ref[...], w_ref[ATTN_WIDTH:, :], preferred_element_type=F32))
    xn = x_ref[...] + mod[2:3] * merged
    xn_ref[...] = xn
    h2 = _rms(xn) * n2g_ref[...] * (1.0 + mod[4:5]) + mod[3:4]
    _store_row_tiles(h2_ref, h2)

    logits = jnp.dot(h2.astype(BF16), wr_ref[...], preferred_element_type=F32)
    lt = jnp.transpose(logits)
    rows = logits.shape[0]
    row = lax.broadcasted_iota(jnp.int32, (EXPERTS_PER_GROUP, rows), 0).astype(F32)
    neg = jnp.float32(-jnp.inf)
    none = jnp.float32(EXPERTS_PER_GROUP)

    def first_at(mask):
        return jnp.min(jnp.where(mask, row, none), axis=0, keepdims=True)

    gvalid = row < N_GROUPS
    lg = jnp.where(gvalid, lt[N_EXPERTS:N_EXPERTS + EXPERTS_PER_GROUP], neg)
    ge = jnp.exp(lg - jnp.max(lg, axis=0, keepdims=True))
    g_prob = ge / jnp.sum(ge, axis=0, keepdims=True)
    g_w = jnp.max(g_prob, axis=0, keepdims=True)
    g_sel = first_at(gvalid & (g_prob == g_w))

    le = lt[(N_GROUPS - 1) * EXPERTS_PER_GROUP:N_GROUPS * EXPERTS_PER_GROUP]
    for g in range(N_GROUPS - 2, -1, -1):
        le = jnp.where(g_sel == g, lt[g * EXPERTS_PER_GROUP:(g + 1) * EXPERTS_PER_GROUP], le)
    ee = jnp.exp(le - jnp.max(le, axis=0, keepdims=True))
    e_prob = ee / jnp.sum(ee, axis=0, keepdims=True)
    p1 = jnp.max(e_prob, axis=0, keepdims=True)
    i1 = first_at(e_prob == p1)
    rest = row != i1
    p2 = jnp.max(jnp.where(rest, e_prob, -1.0), axis=0, keepdims=True)
    i2 = first_at(rest & (e_prob == p2))
    psum = p1 + p2
    w1 = g_w * (p1 / psum)
    w2 = g_w * (p2 / psum)

    def expert_rows(i_sel):
        return jnp.concatenate([jnp.where((g_sel == g) & (row == i_sel), 1.0, 0.0)
                                for g in range(N_GROUPS)], axis=0)

    hit1 = expert_rows(i1)
    hit2 = expert_rows(i2)
    onehot = hit1 + hit2
    before = jnp.dot(onehot, upper_ref[...], preferred_element_type=F32) + carry_s[:, 0:1]
    r1 = jnp.sum(hit1 * before, axis=0, keepdims=True)
    r2 = jnp.sum(hit2 * before, axis=0, keepdims=True)
    carry_s[...] = carry_s[...] + jnp.sum(onehot, axis=1, keepdims=True)

    base = g_sel * EXPERTS_PER_GROUP
    info_t = jnp.where(row == 0, base + i1,
             jnp.where(row == 1, base + i2,
             jnp.where(row == 2, r1,
             jnp.where(row == 3, r2,
             jnp.where(row == 4, w1,
             jnp.where(row == 5, w2, 0.0))))))
    rit_ref[...] = info_t
    ri_ref[...] = jnp.transpose(
        jnp.concatenate([info_t, jnp.zeros((LANES - ROW_TILE, rows), F32)], axis=0))
    cnt_ref[...] = carry_s[...]


def _out_projection(a, cv, w_out_bf, x2, mod3, norm2_g, wr_bf, tri, seq):
    n, d = x2.shape
    tiles_per_batch = seq // OUT_ROWS
    return pl.pallas_call(
        _outproj_kernel,
        grid=(n // OUT_ROWS,),
        in_specs=[
            pl.BlockSpec((OUT_ROWS, ATTN_WIDTH), lambda i: (i, 0)),
            pl.BlockSpec((OUT_ROWS, CONV_WIDTH), lambda i: (i, 0)),
            pl.BlockSpec((d, d), lambda i: (0, 0)),
            pl.BlockSpec((OUT_ROWS, d), lambda i: (i, 0)),
            pl.BlockSpec((1, N_MOD, d), lambda i: (i // tiles_per_batch, 0, 0)),
            pl.BlockSpec((1, d), lambda i: (0, 0)),
            pl.BlockSpec((d, LANES), lambda i: (0, 0)),
            pl.BlockSpec((OUT_ROWS, OUT_ROWS), lambda i: (0, 0)),
        ],
        out_specs=[
            pl.BlockSpec((OUT_ROWS, d), lambda i: (i, 0)),
            pl.BlockSpec((OUT_ROWS * ROW_TILE, LANES), lambda i: (i, 0)),
            pl.BlockSpec((OUT_ROWS, LANES), lambda i: (i, 0)),
            pl.BlockSpec((ROW_TILE, OUT_ROWS), lambda i: (0, i)),
            pl.BlockSpec((N_EXPERTS, LANES), lambda i: (0, 0)),
        ],
        out_shape=[
            jax.ShapeDtypeStruct((n, d), F32),
            jax.ShapeDtypeStruct((n * ROW_TILE, LANES), F32),
            jax.ShapeDtypeStruct((n, LANES), F32),
            jax.ShapeDtypeStruct((ROW_TILE, n), F32),
            jax.ShapeDtypeStruct((N_EXPERTS, LANES), F32),
        ],
        scratch_shapes=[pltpu.VMEM((N_EXPERTS, LANES), F32)],
        compiler_params=_params(("arbitrary",)),
        name="out_projection_routing",
    )(a, cv, w_out_bf, x2, mod3, norm2_g, wr_bf, tri)


def _dispatch_kernel(pend_ref, padded_ref, d0_ref, d1_ref, h_ref, xs_hbm, zero_s, sem):
    tile = MOE_ROWS * ROW_TILE

    @pl.when(pl.program_id(0) == 0)
    def _():
        zero_s[...] = jnp.zeros_like(zero_s)

        def last_tile(e):
            start = pl.multiple_of((pend_ref[e] - MOE_ROWS) * ROW_TILE, tile)
            return pltpu.make_async_copy(zero_s, xs_hbm.at[pl.ds(start, tile)], sem)

        def spare_tile(j):
            return pltpu.make_async_copy(zero_s, xs_hbm.at[pl.ds(j * tile, tile)], sem)

        n_tiles = xs_hbm.shape[0] // tile
        used = pend_ref[N_EXPERTS - 1] // MOE_ROWS
        for e in range(N_EXPERTS):
            @pl.when(padded_ref[e] > 0)
            def _():
                last_tile(e).start()
        for j in range(n_tiles - N_EXPERTS, n_tiles):
            @pl.when(j >= used)
            def _():
                spare_tile(j).start()
        for e in range(N_EXPERTS):
            @pl.when(padded_ref[e] > 0)
            def _():
                last_tile(e).wait()
        for j in range(n_tiles - N_EXPERTS, n_tiles):
            @pl.when(j >= used)
            def _():
                spare_tile(j).wait()

    def start(j, carry):
        for u in range(DMA_UNROLL):
            t = j * DMA_UNROLL + u
            for k, d_ref in enumerate((d0_ref, d1_ref)):
                pltpu.make_async_copy(_row_tile(h_ref, t), _row_tile(xs_hbm, d_ref[0, 0, t]),
                                      sem).start(priority=k)
        return carry

    lax.fori_loop(0, DISPATCH_ROWS // DMA_UNROLL, start, 0)

    def drain(j, carry):
        for _ in range(DMA_UNROLL * TOP_K):
            pltpu.make_async_copy(_row_tile(h_ref, 0), _row_tile(xs_hbm, 0), sem).wait()
        return carry

    lax.fori_loop(0, DISPATCH_ROWS // DMA_UNROLL, drain, 0)


def _dispatch(pends, padded, dest0, dest1, h2t, p_rows):
    n = h2t.shape[0] // ROW_TILE
    steps = n // DISPATCH_ROWS
    smem_rows = lambda: pl.BlockSpec((1, 1, DISPATCH_ROWS), lambda i, pe, pa: (i, 0, 0),
                                     memory_space=pltpu.SMEM)
    return pl.pallas_call(
        _dispatch_kernel,
        grid_spec=pltpu.PrefetchScalarGridSpec(
            num_scalar_prefetch=2,
            grid=(steps,),
            in_specs=[
                smem_rows(),
                smem_rows(),
                pl.BlockSpec((DISPATCH_ROWS * ROW_TILE, LANES), lambda i, pe, pa: (i, 0)),
            ],
            out_specs=pl.BlockSpec(memory_space=pl.ANY),
            scratch_shapes=[pltpu.VMEM((MOE_ROWS * ROW_TILE, LANES), F32), pltpu.SemaphoreType.DMA(())],
        ),
        out_shape=jax.ShapeDtypeStruct((p_rows * ROW_TILE, LANES), F32),
        compiler_params=_params(("arbitrary",)),
        name="moe_dispatch",
    )(pends, padded, dest0.reshape(steps, 1, DISPATCH_ROWS), dest1.reshape(steps, 1, DISPATCH_ROWS), h2t)


def _experts_kernel(first_ref, count_ref, xs_hbm, wg_ref, wu_ref, wd_ref, y_hbm,
                    x_s, y_s, wg_s, wu_s, wd_s, in_sem, out_sem):
    e = pl.program_id(0)
    n = count_ref[e]
    first = first_ref[e]
    tile = MOE_ROWS * ROW_TILE

    def rows(j):
        return pl.ds(pl.multiple_of((first + j) * tile, tile), tile)

    def fetch(j, slot):
        return pltpu.make_async_copy(xs_hbm.at[rows(j)], x_s.at[slot], in_sem.at[slot])

    def writeback(j, slot):
        return pltpu.make_async_copy(y_s.at[slot], y_hbm.at[rows(j)], out_sem.at[slot])

    @pl.when(n > 0)
    def _():
        fetch(0, 0).start(priority=1)
        wg_s[...] = wg_ref[0].astype(BF16)
        wu_s[...] = wu_ref[0].astype(BF16)
        wd_s[...] = wd_ref[0].astype(BF16)

        def tile_step(j, carry):
            slot = j % 2

            @pl.when(j + 1 < n)
            def _():
                fetch(j + 1, 1 - slot).start(priority=1)

            fetch(j, slot).wait()

            @pl.when(j >= 2)
            def _():
                writeback(j - 2, slot).wait()

            x = _load_row_tiles(x_s, MOE_ROWS, slot).astype(BF16)
            g = jnp.dot(x, wg_s[...], preferred_element_type=F32)
            u = jnp.dot(x, wu_s[...], preferred_element_type=F32)
            h = ((g * jax.nn.sigmoid(g)) * u).astype(BF16)
            _store_row_tiles(y_s.at[slot], jnp.dot(h, wd_s[...], preferred_element_type=F32))
            writeback(j, slot).start(priority=1)
            return carry

        lax.fori_loop(0, n, tile_step, 0)

        @pl.when(n >= 2)
        def _():
            writeback(n - 2, n % 2).wait()

        writeback(n - 1, (n - 1) % 2).wait()


def _experts(first_tile, tile_count, xs, w_gate, w_up, w_down):
    n_exp, d, d_exp = w_gate.shape
    tile = MOE_ROWS * ROW_TILE
    w_map = lambda i, ft, tc: (i, 0, 0)
    return pl.pallas_call(
        _experts_kernel,
        grid_spec=pltpu.PrefetchScalarGridSpec(
            num_scalar_prefetch=2,
            grid=(n_exp,),
            in_specs=[
                pl.BlockSpec(memory_space=pl.ANY),
                pl.BlockSpec((1, d, d_exp), w_map),
                pl.BlockSpec((1, d, d_exp), w_map),
                pl.BlockSpec((1, d_exp, d), w_map),
            ],
            out_specs=pl.BlockSpec(memory_space=pl.ANY),
            scratch_shapes=[
                pltpu.VMEM((2, tile, LANES), F32),
                pltpu.VMEM((2, tile, LANES), F32),
                pltpu.VMEM((d, d_exp), BF16),
                pltpu.VMEM((d, d_exp), BF16),
                pltpu.VMEM((d_exp, d), BF16),
                pltpu.SemaphoreType.DMA((2,)),
                pltpu.SemaphoreType.DMA((2,)),
            ],
        ),
        out_shape=jax.ShapeDtypeStruct(xs.shape, F32),
        input_output_aliases={2: 0},
        compiler_params=_params(("arbitrary",)),
        name="moe_experts",
    )(first_tile, tile_count, xs, w_gate, w_up, w_down)


def _combine_kernel(d0_ref, d1_ref, d0n_ref, d1n_ref, ri_ref, xn_ref, mod_ref, fg_ref, y_hbm,
                    o_ref, buf, sem):
    i = pl.program_id(0)
    slot = i % 2

    def gather(refs, to_slot):
        def start(j, carry):
            for u in range(DMA_UNROLL):
                t = j * DMA_UNROLL + u
                for k, d_ref in enumerate(refs):
                    pltpu.make_async_copy(_row_tile(y_hbm, d_ref[0, 0, t]), _row_tile(buf, t, to_slot, k),
                                          sem.at[to_slot]).start(priority=k)
            return carry

        lax.fori_loop(0, COMBINE_ROWS // DMA_UNROLL, start, 0)

    @pl.when(i == 0)
    def _():
        gather((d0_ref, d1_ref), 0)

    @pl.when(i + 1 < pl.num_programs(0))
    def _():
        gather((d0n_ref, d1n_ref), 1 - slot)

    def drain(j, carry):
        for _ in range(DMA_UNROLL * TOP_K):
            pltpu.make_async_copy(_row_tile(y_hbm, 0), _row_tile(buf, 0, slot, 0), sem.at[slot]).wait()
        return carry

    lax.fori_loop(0, COMBINE_ROWS // DMA_UNROLL, drain, 0)

    ri = ri_ref[...]
    y = (_load_row_tiles(buf, COMBINE_ROWS, slot, 0) * ri[:, 4:5]
         + _load_row_tiles(buf, COMBINE_ROWS, slot, 1) * ri[:, 5:6])
    xf = xn_ref[...] + mod_ref[0][5:6] * y
    o_ref[...] = _rms(xf) * fg_ref[...]


def _combine(dest0, dest1, rinfo, x_new, mod3, final_g, ybuf, seq):
    n, d = x_new.shape
    steps = n // COMBINE_ROWS
    tiles_per_batch = seq // COMBINE_ROWS
    cur = lambda: pl.BlockSpec((1, 1, COMBINE_ROWS), lambda i: (i, 0, 0), memory_space=pltpu.SMEM)
    nxt = lambda: pl.BlockSpec((1, 1, COMBINE_ROWS), lambda i: (jnp.minimum(i + 1, steps - 1), 0, 0),
                               memory_space=pltpu.SMEM)
    d0 = dest0.reshape(steps, 1, COMBINE_ROWS)
    d1 = dest1.reshape(steps, 1, COMBINE_ROWS)
    return pl.pallas_call(
        _combine_kernel,
        grid=(steps,),
        in_specs=[
            cur(), cur(), nxt(), nxt(),
            pl.BlockSpec((COMBINE_ROWS, LANES), lambda i: (i, 0)),
            pl.BlockSpec((COMBINE_ROWS, d), lambda i: (i, 0)),
            pl.BlockSpec((1, N_MOD, d), lambda i: (i // tiles_per_batch, 0, 0)),
            pl.BlockSpec((1, d), lambda i: (0, 0)),
            pl.BlockSpec(memory_space=pl.ANY),
        ],
        out_specs=pl.BlockSpec((COMBINE_ROWS, d), lambda i: (i, 0)),
        out_shape=jax.ShapeDtypeStruct((n, d), F32),
        scratch_shapes=[pltpu.VMEM((2, TOP_K, COMBINE_ROWS * ROW_TILE, LANES), F32),
                        pltpu.SemaphoreType.DMA((2,))],
        compiler_params=_params(("arbitrary",)),
        name="moe_combine",
    )(d0, d1, d0, d1, rinfo, x_new, mod3, final_g, ybuf)


def _rope_tables(seq):
    rows = seq // GRID_W
    row_idx = jnp.repeat(jnp.arange(rows, dtype=F32), GRID_W)
    col_idx = jnp.tile(jnp.arange(GRID_W, dtype=F32), rows)
    inv_freq = ROPE_THETA ** (-jnp.arange(0, ROPE_AXIS_DIM, 2, dtype=F32) / ROPE_AXIS_DIM)
    ang = jnp.stack([row_idx[:, None] * inv_freq, col_idx[:, None] * inv_freq], axis=1)
    cos = jnp.cos(ang)
    sin = jnp.sin(ang)
    cos_h = jnp.stack([cos, cos], axis=2).reshape(seq, HEAD_DIM)
    sin_h = jnp.stack([-sin, sin], axis=2).reshape(seq, HEAD_DIM)
    reps = LANES // HEAD_DIM
    return jnp.tile(cos_h, (1, reps)), jnp.tile(sin_h, (1, reps))


def kernel(x, c, ctx, c_ctx, w_mod, b_mod, norm1_g, w_in, q_norm_g, k_norm_g, conv_w, attn_out_g,
           conv_out_g, w_out, norm2_g, w_group, w_router, w_gate, w_up, w_down, final_g):
    assert w_mod.shape[0] == 1, "single-layer block"
    b, s, d = x.shape
    n = b * s
    assert b + 1 <= MOD_ROWS

    cond = jnp.zeros((MOD_ROWS, d), F32).at[:b].set(c).at[b].set(c_ctx)
    mod3 = _modulation(cond, w_mod[0], b_mod[0]).reshape(MOD_ROWS, N_MOD, d)

    cos_t, sin_t = _rope_tables(s)
    head_of = jnp.arange(ATTN_WIDTH) // HEAD_DIM
    bd = jnp.where(head_of[:, None] == head_of[None, :], 1.0 / HEAD_DIM, 0.0).astype(BF16)
    q, k, v, cv = _in_projection(
        x, ctx, mod3, norm1_g, w_in[0].astype(BF16),
        jnp.tile(q_norm_g[0], N_HEADS)[None], jnp.tile(k_norm_g[0], N_KV_HEADS)[None],
        cos_t, sin_t, conv_w[0], conv_out_g, bd)

    a = _attention(q, k, v, attn_out_g)

    wr = jnp.zeros((d, LANES), F32).at[:, :N_EXPERTS].set(w_router[0])
    wr = wr.at[:, N_EXPERTS:N_EXPERTS + N_GROUPS].set(w_group[0]).astype(BF16)
    ti = jnp.arange(OUT_ROWS)
    tri = (ti[:, None] < ti[None, :]).astype(F32)
    x_new, h2t, rinfo, rinfo_t, counts = _out_projection(
        a.reshape(n, ATTN_WIDTH), cv.reshape(n, CONV_WIDTH), w_out[0].astype(BF16),
        x.reshape(n, d), mod3, norm2_g, wr, tri, s)

    cnt = counts[:, 0].astype(jnp.int32)
    padded = ((cnt + MOE_ROWS - 1) // MOE_ROWS) * MOE_ROWS
    pends = jnp.cumsum(padded)
    pstarts = pends - padded
    experts = jnp.arange(N_EXPERTS, dtype=jnp.int32)[:, None]

    def slots(k):
        eid = rinfo_t[k].astype(jnp.int32)
        rank = rinfo_t[TOP_K + k].astype(jnp.int32)
        return jnp.sum(jnp.where(eid[None, :] == experts, pstarts[:, None], 0), axis=0) + rank

    dest0, dest1 = slots(0), slots(1)
    p_rows = n * TOP_K + N_EXPERTS * MOE_ROWS

    xs = _dispatch(pends, padded, dest0, dest1, h2t, p_rows)
    ybuf = _experts(pstarts // MOE_ROWS, padded // MOE_ROWS, xs, w_gate[0], w_up[0], w_down[0])
    out = _combine(dest0, dest1, rinfo, x_new, mod3, final_g.reshape(1, d), ybuf, s)
    return out.reshape(b, s, d)
```

```python
import functools

import jax
import jax.numpy as jnp
from jax import lax
from jax.experimental import pallas as pl
from jax.experimental.pallas import tpu as pltpu

F32 = jnp.float32
BF16 = jnp.bfloat16

D_MODEL = 1024
GRID_W = 64
ATTN_WIDTH = 512
N_HEADS = 8
N_KV_HEADS = 2
HEAD_DIM = 64
KV_REP = N_HEADS // N_KV_HEADS
KV_WIDTH = N_KV_HEADS * HEAD_DIM
CONV_WIDTH = 512
IN_COLS = ATTN_WIDTH + 2 * KV_WIDTH + 3 * CONV_WIDTH
ROPE_THETA = 10000.0
ROPE_AXIS_DIM = HEAD_DIM // 2
ROPE_FREQS = ROPE_AXIS_DIM // 2
N_GROUPS = 4
EXPERTS_PER_GROUP = 8
N_EXPERTS = N_GROUPS * EXPERTS_PER_GROUP
TOP_K = 2
D_EXPERT = 768
N_MOD = 6
EPS = 1e-6
LOG2_E = 1.4426950408889634
Q_SCALE = HEAD_DIM ** -0.5 * LOG2_E
V_LANES = 2 * HEAD_DIM

LANES = 128
ROW_TILE = 8
PACK_ROWS = 4
HIGH_HALF = -65536
MOD_ROWS = 16
IN_ROWS = 512
CONV_PAD = 8
ATTN_TQ = 128
ATTN_HEAD_SPLIT = 2
OUT_ROWS = 512
DISPATCH_ROWS = 1024
DMA_UNROLL = 8
MOE_ROWS = 256
COMBINE_ROWS = 256
VMEM_LIMIT = 56 * 1024 * 1024


def _params(semantics, vmem=None):
    return pltpu.CompilerParams(dimension_semantics=semantics,
                                vmem_limit_bytes=vmem if vmem else VMEM_LIMIT)


def _rms(x):
    return x * lax.rsqrt(jnp.mean(x * x, axis=-1, keepdims=True) + EPS)


def _store_row_tiles(ref, val):
    rows, width = val.shape
    assert width == 2 * PACK_ROWS * LANES
    half = width // 2

    def bits(v):
        return lax.bitcast_convert_type(v.astype(BF16).astype(F32), jnp.int32)

    words = lax.shift_right_logical(bits(val[:, :half]), 16) | (bits(val[:, half:]) & HIGH_HALF)
    for i in range(PACK_ROWS):
        ref[pl.ds(i, rows, stride=PACK_ROWS), :] = words[:, i * LANES:(i + 1) * LANES]


def _load_row_tiles(ref, rows, *lead):
    words = [ref[(*lead, pl.ds(i, rows, stride=PACK_ROWS), slice(None))] for i in range(PACK_ROWS)]
    low = [lax.bitcast_convert_type(lax.shift_left(w, 16), F32) for w in words]
    high = [lax.bitcast_convert_type(w & HIGH_HALF, F32) for w in words]
    return jnp.concatenate(low + high, axis=1)


def _row_tile(ref, row, *lead):
    return ref.at[(*lead, pl.ds(pl.multiple_of(row * PACK_ROWS, PACK_ROWS), PACK_ROWS))]


def _mod_kernel(c_ref, w_ref, b_ref, o_ref):
    c = c_ref[...]
    s = c * jax.nn.sigmoid(c)
    o_ref[...] = jnp.dot(s.astype(BF16), w_ref[...].astype(BF16),
                         preferred_element_type=F32) + b_ref[...]


def _modulation(cond, w_mod, b_mod):
    d = cond.shape[1]
    n_out = w_mod.shape[1]
    tn = 1024
    return pl.pallas_call(
        _mod_kernel,
        grid=(n_out // tn,),
        in_specs=[pl.BlockSpec((MOD_ROWS, d), lambda j: (0, 0)),
                  pl.BlockSpec((d, tn), lambda j: (0, j)),
                  pl.BlockSpec((1, tn), lambda j: (0, j))],
        out_specs=pl.BlockSpec((MOD_ROWS, tn), lambda j: (0, j)),
        out_shape=jax.ShapeDtypeStruct((MOD_ROWS, n_out), F32),
        compiler_params=_params(("arbitrary",)),
        name="modulation",
    )(cond, w_mod, b_mod.reshape(1, n_out))


def _with_ones(v):
    return jnp.concatenate([v, jnp.ones_like(v)], axis=1).astype(BF16)


def _rope(x, cos, sin_signed, is_lo):
    partner = jnp.where(is_lo, pltpu.roll(x, LANES - ROPE_FREQS, 1), pltpu.roll(x, ROPE_FREQS, 1))
    return x * cos + partner * sin_signed


def _inproj_kernel(x_ref, ctx_ref, mod_ref, cmod_ref, n1g_ref, w_ref, qg_ref, kg_ref,
                   cos_ref, sin_ref, convw_ref, cog_ref, bd_ref,
                   q_ref, k_ref, v_ref, cv_ref, p_s, gb_s):
    seq = x_ref.shape[1]
    ctx_len = ctx_ref.shape[1]
    g1 = n1g_ref[...]
    bd = bd_ref[...]
    bd_kv = bd_ref[0:KV_WIDTH, 0:KV_WIDTH]
    kg = kg_ref[...]
    qg = qg_ref[...]

    def head_ms(z, m):
        return jnp.dot((z * z).astype(BF16), m, preferred_element_type=F32)

    cmod = cmod_ref[0]
    hc = _rms(ctx_ref[0]) * g1 * (1.0 + cmod[1:2]) + cmod[0:1]
    zc = jnp.dot(hc.astype(BF16), w_ref[:, ATTN_WIDTH:ATTN_WIDTH + 2 * KV_WIDTH],
                 preferred_element_type=F32)
    kc = zc[:, :KV_WIDTH]
    kc = kc * lax.rsqrt(head_ms(kc, bd_kv) + EPS) * kg
    vc = zc[:, KV_WIDTH:]
    for g in range(N_KV_HEADS):
        k_ref[0, g, 0:ctx_len, :] = kc[:, g * HEAD_DIM:(g + 1) * HEAD_DIM].astype(BF16)
        v_ref[0, g, 0:ctx_len, :] = _with_ones(vc[:, g * HEAD_DIM:(g + 1) * HEAD_DIM])

    zeros = jnp.zeros((CONV_PAD, CONV_WIDTH), F32)
    p_s[0:CONV_PAD, :] = zeros
    p_s[CONV_PAD + seq:CONV_PAD + seq + CONV_PAD, :] = zeros

    mod = mod_ref[0]
    sh1 = mod[0:1]
    sc1 = mod[1:2]
    lane = lax.broadcasted_iota(jnp.int32, (IN_ROWS, LANES), 1)
    is_lo = (lane // ROPE_FREQS) % 2 == 0

    def proj_chunk(c, carry):
        r0 = pl.multiple_of(c * IN_ROWS, IN_ROWS)
        h = (_rms(x_ref[0, pl.ds(r0, IN_ROWS), :]) * g1 * (1.0 + sc1) + sh1).astype(BF16)
        cos = cos_ref[pl.ds(r0, IN_ROWS), :]
        sin = sin_ref[pl.ds(r0, IN_ROWS), :]
        zq = jnp.dot(h, w_ref[:, 0:ATTN_WIDTH], preferred_element_type=F32)
        qn = zq * lax.rsqrt(head_ms(zq, bd) + EPS) * qg
        for j in range(ATTN_WIDTH // LANES):
            blk = _rope(qn[:, j * LANES:(j + 1) * LANES], cos, sin, is_lo) * Q_SCALE
            for hh in range(LANES // HEAD_DIM):
                head = j * (LANES // HEAD_DIM) + hh
                q_ref[0, head, pl.ds(r0, IN_ROWS), :] = (
                    blk[:, hh * HEAD_DIM:(hh + 1) * HEAD_DIM].astype(BF16))
        zkv = jnp.dot(h, w_ref[:, ATTN_WIDTH:ATTN_WIDTH + 2 * KV_WIDTH], preferred_element_type=F32)
        kx = zkv[:, :KV_WIDTH]
        kx = _rope(kx * lax.rsqrt(head_ms(kx, bd_kv) + EPS) * kg, cos, sin, is_lo)
        vx = zkv[:, KV_WIDTH:]
        for g in range(N_KV_HEADS):
            k_ref[0, g, pl.ds(ctx_len + r0, IN_ROWS), :] = kx[:, g * HEAD_DIM:(g + 1) * HEAD_DIM].astype(BF16)
            v_ref[0, g, pl.ds(ctx_len + r0, IN_ROWS), :] = _with_ones(vx[:, g * HEAD_DIM:(g + 1) * HEAD_DIM])
        c0 = ATTN_WIDTH + 2 * KV_WIDTH
        gb_s[pl.ds(r0, IN_ROWS), :] = jnp.dot(h, w_ref[:, c0:c0 + CONV_WIDTH], preferred_element_type=F32)
        zc_ = jnp.dot(h, w_ref[:, c0 + CONV_WIDTH:c0 + 2 * CONV_WIDTH], preferred_element_type=F32)
        zu = jnp.dot(h, w_ref[:, c0 + 2 * CONV_WIDTH:c0 + 3 * CONV_WIDTH], preferred_element_type=F32)
        p_s[pl.ds(CONV_PAD + r0, IN_ROWS), :] = zc_ * zu
        return carry

    lax.fori_loop(0, seq // IN_ROWS, proj_chunk, 0)

    cw = convw_ref[...]
    cog = cog_ref[...]

    def conv_chunk(c, carry):
        r0 = pl.multiple_of(c * IN_ROWS, IN_ROWS)
        win = p_s[pl.ds(r0, IN_ROWS + 2 * CONV_PAD), :]
        n_win = IN_ROWS + 2 * CONV_PAD
        prev = pltpu.roll(win, 1, 0)[CONV_PAD:CONV_PAD + IN_ROWS]
        cur = win[CONV_PAD:CONV_PAD + IN_ROWS]
        nxt = pltpu.roll(win, n_win - 1, 0)[CONV_PAD:CONV_PAD + IN_ROWS]
        y = cw[0:1] * prev + cw[1:2] * cur + cw[2:3] * nxt
        cvv = gb_s[pl.ds(r0, IN_ROWS), :] * y
        cvn = cvv * lax.rsqrt(head_ms(cvv, bd) + EPS) * cog
        cv_ref[0, pl.ds(r0, IN_ROWS), :] = cvn.astype(BF16)
        return carry

    lax.fori_loop(0, seq // IN_ROWS, conv_chunk, 0)


def _in_projection(x, ctx, mod3, norm1_g, w_in_bf, qg_t, kg_t, cos_t, sin_t, conv_w, conv_out_g, bd):
    b, s, d = x.shape
    ctx_len = ctx.shape[1]
    n_keys = ctx_len + s
    const = lambda *shape: pl.BlockSpec(shape, lambda i: (0,) * len(shape))
    return pl.pallas_call(
        _inproj_kernel,
        grid=(b,),
        in_specs=[
            pl.BlockSpec((1, s, d), lambda i: (i, 0, 0)),
            pl.BlockSpec((1, ctx_len, d), lambda i: (i, 0, 0)),
            pl.BlockSpec((1, N_MOD, d), lambda i: (i, 0, 0)),
            pl.BlockSpec((1, N_MOD, d), lambda i: (b, 0, 0)),
            const(1, d),
            const(d, IN_COLS),
            const(1, ATTN_WIDTH),
            const(1, KV_WIDTH),
            const(s, LANES),
            const(s, LANES),
            const(3, CONV_WIDTH),
            const(1, CONV_WIDTH),
            const(ATTN_WIDTH, ATTN_WIDTH),
        ],
        out_specs=[
            pl.BlockSpec((1, N_HEADS, s, HEAD_DIM), lambda i: (i, 0, 0, 0)),
            pl.BlockSpec((1, N_KV_HEADS, n_keys, HEAD_DIM), lambda i: (i, 0, 0, 0)),
            pl.BlockSpec((1, N_KV_HEADS, n_keys, V_LANES), lambda i: (i, 0, 0, 0)),
            pl.BlockSpec((1, s, CONV_WIDTH), lambda i: (i, 0, 0)),
        ],
        out_shape=[
            jax.ShapeDtypeStruct((b, N_HEADS, s, HEAD_DIM), BF16),
            jax.ShapeDtypeStruct((b, N_KV_HEADS, n_keys, HEAD_DIM), BF16),
            jax.ShapeDtypeStruct((b, N_KV_HEADS, n_keys, V_LANES), BF16),
            jax.ShapeDtypeStruct((b, s, CONV_WIDTH), BF16),
        ],
        scratch_shapes=[pltpu.VMEM((s + 2 * CONV_PAD, CONV_WIDTH), F32),
                        pltpu.VMEM((s, CONV_WIDTH), F32)],
        compiler_params=_params(("arbitrary",)),
        name="in_projection",
    )(x, ctx, mod3, mod3, norm1_g, w_in_bf, qg_t, kg_t, cos_t, sin_t, conv_w, conv_out_g, bd)


def _attn_kernel(q_ref, k_ref, v_ref, g_ref, o_ref):
    tq = q_ref.shape[2]
    pieces = []
    hs = KV_REP // ATTN_HEAD_SPLIT
    subs = [(g, g * KV_REP + j * hs) for g in range(N_KV_HEADS) for j in range(ATTN_HEAD_SPLIT)]
    scores = []
    for g, h0 in subs:
        qs = jnp.concatenate([q_ref[0, h0 + r] for r in range(hs)], axis=0)
        scores.append(lax.dot_general(qs, k_ref[0, g], (((1,), (1,)), ((), ())),
                                      preferred_element_type=F32))
    probs = [jnp.exp2(s - jnp.max(s, axis=-1, keepdims=True)).astype(BF16) for s in scores]
    lane = lax.broadcasted_iota(jnp.int32, (hs * tq, LANES), 1)
    for (g, h0), p in zip(subs, probs):
        ov = jnp.dot(p, v_ref[0, g], preferred_element_type=F32)
        o = ov / pltpu.roll(ov, HEAD_DIM, 1)
        ms = jnp.sum(jnp.where(lane < HEAD_DIM, o * o, 0.0), axis=-1, keepdims=True) * (1.0 / HEAD_DIM)
        o = o * lax.rsqrt(ms + EPS)
        for r in range(hs):
            pieces.append(o[r * tq:(r + 1) * tq, 0:HEAD_DIM])
    o_ref[0] = (jnp.concatenate(pieces, axis=1) * g_ref[...]).astype(BF16)


def _attention(q, k, v, attn_out_g):
    b, _, s, _ = q.shape
    n_keys = k.shape[2]
    return pl.pallas_call(
        _attn_kernel,
        grid=(b, s // ATTN_TQ),
        in_specs=[
            pl.BlockSpec((1, N_HEADS, ATTN_TQ, HEAD_DIM), lambda i, j: (i, 0, j, 0)),
            pl.BlockSpec((1, N_KV_HEADS, n_keys, HEAD_DIM), lambda i, j: (i, 0, 0, 0)),
            pl.BlockSpec((1, N_KV_HEADS, n_keys, V_LANES), lambda i, j: (i, 0, 0, 0)),
            pl.BlockSpec((1, ATTN_WIDTH), lambda i, j: (0, 0)),
        ],
        out_specs=pl.BlockSpec((1, ATTN_TQ, ATTN_WIDTH), lambda i, j: (i, j, 0)),
        out_shape=jax.ShapeDtypeStruct((b, s, ATTN_WIDTH), BF16),
        compiler_params=_params(("arbitrary", "arbitrary")),
        name="attention",
    )(q, k, v, attn_out_g)


def _outproj_kernel(a_ref, cv_ref, w_ref, x_ref, mod_ref, n2g_ref, wr_ref, upper_ref,
                    xn_ref, h2_ref, ri_ref, rit_ref, cnt_ref, carry_s):
    i = pl.program_id(0)

    @pl.when(i == 0)
    def _():
        carry_s[...] = jnp.zeros_like(carry_s)

    mod = mod_ref[0]
    merged = (jnp.dot(a_ref[...], w_ref[0:ATTN_WIDTH, :], preferred_element_type=F32)
              + jnp.dot(cv_ref[...], w_ref[ATTN_WIDTH:, :], preferred_element_type=F32))
    xn = x_ref[...] + mod[2:3] * merged
    xn_ref[...] = xn
    h2 = _rms(xn) * n2g_ref[...] * (1.0 + mod[4:5]) + mod[3:4]
    _store_row_tiles(h2_ref, h2)

    logits = jnp.dot(h2.astype(BF16), wr_ref[...], preferred_element_type=F32)
    lt = jnp.transpose(logits)
    rows = logits.shape[0]
    row = lax.broadcasted_iota(jnp.int32, (EXPERTS_PER_GROUP, rows), 0).astype(F32)
    neg = jnp.float32(-jnp.inf)
    none = jnp.float32(EXPERTS_PER_GROUP)

    def first_at(mask):
        return jnp.min(jnp.where(mask, row, none), axis=0, keepdims=True)

    gvalid = row < N_GROUPS
    lg = jnp.where(gvalid, lt[N_EXPERTS:N_EXPERTS + EXPERTS_PER_GROUP], neg)
    ge = jnp.exp(lg - jnp.max(lg, axis=0, keepdims=True))
    g_prob = ge / jnp.sum(ge, axis=0, keepdims=True)
    g_w = jnp.max(g_prob, axis=0, keepdims=True)
    g_sel = first_at(gvalid & (g_prob == g_w))

    le = lt[(N_GROUPS - 1) * EXPERTS_PER_GROUP:N_GROUPS * EXPERTS_PER_GROUP]
    for g in range(N_GROUPS - 2, -1, -1):
        le = jnp.where(g_sel == g, lt[g * EXPERTS_PER_GROUP:(g + 1) * EXPERTS_PER_GROUP], le)
    ee = jnp.exp(le - jnp.max(le, axis=0, keepdims=True))
    e_prob = ee / jnp.sum(ee, axis=0, keepdims=True)
    p1 = jnp.max(e_prob, axis=0, keepdims=True)
    i1 = first_at(e_prob == p1)
    rest = row != i1
    p2 = jnp.max(jnp.where(rest, e_prob, -1.0), axis=0, keepdims=True)
    i2 = first_at(rest & (e_prob == p2))
    psum = p1 + p2
    w1 = g_w * (p1 / psum)
    w2 = g_w * (p2 / psum)

    def expert_rows(i_sel):
        return jnp.concatenate([jnp.where((g_sel == g) & (row == i_sel), 1.0, 0.0)
                                for g in range(N_GROUPS)], axis=0)

    hit1 = expert_rows(i1)
    hit2 = expert_rows(i2)
    onehot = hit1 + hit2
    before = jnp.dot(onehot, upper_ref[...], preferred_element_type=F32) + carry_s[:, 0:1]
    r1 = jnp.sum(hit1 * before, axis=0, keepdims=True)
    r2 = jnp.sum(hit2 * before, axis=0, keepdims=True)
    carry_s[...] = carry_s[...] + jnp.sum(onehot, axis=1, keepdims=True)

    base = g_sel * EXPERTS_PER_GROUP
    info_t = jnp.where(row == 0, base + i1,
             jnp.where(row == 1, base + i2,
             jnp.where(row == 2, r1,
             jnp.where(row == 3, r2,
             jnp.where(row == 4, w1,
             jnp.where(row == 5, w2, 0.0))))))
    rit_ref[...] = info_t
    ri_ref[...] = jnp.transpose(
        jnp.concatenate([info_t, jnp.zeros((LANES - ROW_TILE, rows), F32)], axis=0))
    cnt_ref[...] = carry_s[...]


def _out_projection(a, cv, w_out_bf, x2, mod3, norm2_g, wr_bf, tri, seq):
    n, d = x2.shape
    tiles_per_batch = seq // OUT_ROWS
    return pl.pallas_call(
        _outproj_kernel,
        grid=(n // OUT_ROWS,),
        in_specs=[
            pl.BlockSpec((OUT_ROWS, ATTN_WIDTH), lambda i: (i, 0)),
            pl.BlockSpec((OUT_ROWS, CONV_WIDTH), lambda i: (i, 0)),
            pl.BlockSpec((d, d), lambda i: (0, 0)),
            pl.BlockSpec((OUT_ROWS, d), lambda i: (i, 0)),
            pl.BlockSpec((1, N_MOD, d), lambda i: (i // tiles_per_batch, 0, 0)),
            pl.BlockSpec((1, d), lambda i: (0, 0)),
            pl.BlockSpec((d, LANES), lambda i: (0, 0)),
            pl.BlockSpec((OUT_ROWS, OUT_ROWS), lambda i: (0, 0)),
        ],
        out_specs=[
            pl.BlockSpec((OUT_ROWS, d), lambda i: (i, 0)),
            pl.BlockSpec((OUT_ROWS * PACK_ROWS, LANES), lambda i: (i, 0)),
            pl.BlockSpec((OUT_ROWS, LANES), lambda i: (i, 0)),
            pl.BlockSpec((ROW_TILE, OUT_ROWS), lambda i: (0, i)),
            pl.BlockSpec((N_EXPERTS, LANES), lambda i: (0, 0)),
        ],
        out_shape=[
            jax.ShapeDtypeStruct((n, d), F32),
            jax.ShapeDtypeStruct((n * PACK_ROWS, LANES), jnp.int32),
            jax.ShapeDtypeStruct((n, LANES), F32),
            jax.ShapeDtypeStruct((ROW_TILE, n), F32),
            jax.ShapeDtypeStruct((N_EXPERTS, LANES), F32),
        ],
        scratch_shapes=[pltpu.VMEM((N_EXPERTS, LANES), F32)],
        compiler_params=_params(("arbitrary",)),
        name="out_projection_routing",
    )(a, cv, w_out_bf, x2, mod3, norm2_g, wr_bf, tri)


def _dispatch_kernel(pend_ref, padded_ref, d0_ref, d1_ref, h_ref, xs_hbm, zero_s, sem):
    tile = MOE_ROWS * PACK_ROWS

    @pl.when(pl.program_id(0) == 0)
    def _():
        zero_s[...] = jnp.zeros_like(zero_s)

        def last_tile(e):
            start = pl.multiple_of((pend_ref[e] - MOE_ROWS) * PACK_ROWS, tile)
            return pltpu.make_async_copy(zero_s, xs_hbm.at[pl.ds(start, tile)], sem)

        def spare_tile(j):
            return pltpu.make_async_copy(zero_s, xs_hbm.at[pl.ds(j * tile, tile)], sem)

        n_tiles = xs_hbm.shape[0] // tile
        used = pend_ref[N_EXPERTS - 1] // MOE_ROWS
        for e in range(N_EXPERTS):
            @pl.when(padded_ref[e] > 0)
            def _():
                last_tile(e).start()
        for j in range(n_tiles - N_EXPERTS, n_tiles):
            @pl.when(j >= used)
            def _():
                spare_tile(j).start()
        for e in range(N_EXPERTS):
            @pl.when(padded_ref[e] > 0)
            def _():
                last_tile(e).wait()
        for j in range(n_tiles - N_EXPERTS, n_tiles):
            @pl.when(j >= used)
            def _():
                spare_tile(j).wait()

    def start(j, carry):
        for u in range(DMA_UNROLL):
            t = j * DMA_UNROLL + u
            for k, d_ref in enumerate((d0_ref, d1_ref)):
                pltpu.make_async_copy(_row_tile(h_ref, t), _row_tile(xs_hbm, d_ref[0, 0, t]),
                                      sem).start(priority=k)
        return carry

    lax.fori_loop(0, DISPATCH_ROWS // DMA_UNROLL, start, 0)

    def drain(j, carry):
        for _ in range(DMA_UNROLL * TOP_K):
            pltpu.make_async_copy(_row_tile(h_ref, 0), _row_tile(xs_hbm, 0), sem).wait()
        return carry

    lax.fori_loop(0, DISPATCH_ROWS // DMA_UNROLL, drain, 0)


def _dispatch(pends, padded, dest0, dest1, h2t, p_rows):
    n = h2t.shape[0] // PACK_ROWS
    steps = n // DISPATCH_ROWS
    smem_rows = lambda: pl.BlockSpec((1, 1, DISPATCH_ROWS), lambda i, pe, pa: (i, 0, 0),
                                     memory_space=pltpu.SMEM)
    return pl.pallas_call(
        _dispatch_kernel,
        grid_spec=pltpu.PrefetchScalarGridSpec(
            num_scalar_prefetch=2,
            grid=(steps,),
            in_specs=[
                smem_rows(),
                smem_rows(),
                pl.BlockSpec((DISPATCH_ROWS * PACK_ROWS, LANES), lambda i, pe, pa: (i, 0)),
            ],
            out_specs=pl.BlockSpec(memory_space=pl.ANY),
            scratch_shapes=[pltpu.VMEM((MOE_ROWS * PACK_ROWS, LANES), jnp.int32),
                            pltpu.SemaphoreType.DMA(())],
        ),
        out_shape=jax.ShapeDtypeStruct((p_rows * PACK_ROWS, LANES), jnp.int32),
        compiler_params=_params(("arbitrary",)),
        name="moe_dispatch",
    )(pends, padded, dest0.reshape(steps, 1, DISPATCH_ROWS), dest1.reshape(steps, 1, DISPATCH_ROWS), h2t)


def _experts_kernel(first_ref, count_ref, xs_hbm, wg_ref, wu_ref, wd_ref, y_hbm,
                    x_s, y_s, wg_s, wu_s, wd_s, in_sem, out_sem):
    e = pl.program_id(0)
    n = count_ref[e]
    first = first_ref[e]
    tile = MOE_ROWS * PACK_ROWS

    def rows(j):
        return pl.ds(pl.multiple_of((first + j) * tile, tile), tile)

    def fetch(j, slot):
        return pltpu.make_async_copy(xs_hbm.at[rows(j)], x_s.at[slot], in_sem.at[slot])

    def writeback(j, slot):
        return pltpu.make_async_copy(y_s.at[slot], y_hbm.at[rows(j)], out_sem.at[slot])

    @pl.when(n > 0)
    def _():
        fetch(0, 0).start(priority=1)
        wg_s[...] = wg_ref[0].astype(BF16)
        wu_s[...] = wu_ref[0].astype(BF16)
        wd_s[...] = wd_ref[0].astype(BF16)

        def tile_step(j, carry):
            slot = j % 2

            @pl.when(j + 1 < n)
            def _():
                fetch(j + 1, 1 - slot).start(priority=1)

            fetch(j, slot).wait()

            @pl.when(j >= 2)
            def _():
                writeback(j - 2, slot).wait()

            x = _load_row_tiles(x_s, MOE_ROWS, slot).astype(BF16)
            g = jnp.dot(x, wg_s[...], preferred_element_type=F32)
            u = jnp.dot(x, wu_s[...], preferred_element_type=F32)
            h = ((g * jax.nn.sigmoid(g)) * u).astype(BF16)
            _store_row_tiles(y_s.at[slot], jnp.dot(h, wd_s[...], preferred_element_type=F32))
            writeback(j, slot).start(priority=1)
            return carry

        lax.fori_loop(0, n, tile_step, 0)

        @pl.when(n >= 2)
        def _():
            writeback(n - 2, n % 2).wait()

        writeback(n - 1, (n - 1) % 2).wait()


def _experts(first_tile, tile_count, xs, w_gate, w_up, w_down):
    n_exp, d, d_exp = w_gate.shape
    tile = MOE_ROWS * PACK_ROWS
    w_map = lambda i, ft, tc: (i, 0, 0)
    return pl.pallas_call(
        _experts_kernel,
        grid_spec=pltpu.PrefetchScalarGridSpec(
            num_scalar_prefetch=2,
            grid=(n_exp,),
            in_specs=[
                pl.BlockSpec(memory_space=pl.ANY),
                pl.BlockSpec((1, d, d_exp), w_map),
                pl.BlockSpec((1, d, d_exp), w_map),
                pl.BlockSpec((1, d_exp, d), w_map),
            ],
            out_specs=pl.BlockSpec(memory_space=pl.ANY),
            scratch_shapes=[
                pltpu.VMEM((2, tile, LANES), jnp.int32),
                pltpu.VMEM((2, tile, LANES), jnp.int32),
                pltpu.VMEM((d, d_exp), BF16),
                pltpu.VMEM((d, d_exp), BF16),
                pltpu.VMEM((d_exp, d), BF16),
                pltpu.SemaphoreType.DMA((2,)),
                pltpu.SemaphoreType.DMA((2,)),
            ],
        ),
        out_shape=jax.ShapeDtypeStruct(xs.shape, xs.dtype),
        input_output_aliases={2: 0},
        compiler_params=_params(("arbitrary",)),
        name="moe_experts",
    )(first_tile, tile_count, xs, w_gate, w_up, w_down)


def _combine_kernel(d0_ref, d1_ref, d0n_ref, d1n_ref, ri_ref, xn_ref, mod_ref, fg_ref, y_hbm,
                    o_ref, buf, sem):
    i = pl.program_id(0)
    slot = i % 2

    def gather(refs, to_slot):
        def start(j, carry):
            for u in range(DMA_UNROLL):
                t = j * DMA_UNROLL + u
                for k, d_ref in enumerate(refs):
                    pltpu.make_async_copy(_row_tile(y_hbm, d_ref[0, 0, t]), _row_tile(buf, t, to_slot, k),
                                          sem.at[to_slot]).start(priority=k)
            return carry

        lax.fori_loop(0, COMBINE_ROWS // DMA_UNROLL, start, 0)

    @pl.when(i == 0)
    def _():
        gather((d0_ref, d1_ref), 0)

    @pl.when(i + 1 < pl.num_programs(0))
    def _():
        gather((d0n_ref, d1n_ref), 1 - slot)

    def drain(j, carry):
        for _ in range(DMA_UNROLL * TOP_K):
            pltpu.make_async_copy(_row_tile(y_hbm, 0), _row_tile(buf, 0, slot, 0), sem.at[slot]).wait()
        return carry

    lax.fori_loop(0, COMBINE_ROWS // DMA_UNROLL, drain, 0)

    ri = ri_ref[...]
    y = (_load_row_tiles(buf, COMBINE_ROWS, slot, 0) * ri[:, 4:5]
         + _load_row_tiles(buf, COMBINE_ROWS, slot, 1) * ri[:, 5:6])
    xf = xn_ref[...] + mod_ref[0][5:6] * y
    o_ref[...] = _rms(xf) * fg_ref[...]


def _combine(dest0, dest1, rinfo, x_new, mod3, final_g, ybuf, seq):
    n, d = x_new.shape
    steps = n // COMBINE_ROWS
    tiles_per_batch = seq // COMBINE_ROWS
    cur = lambda: pl.BlockSpec((1, 1, COMBINE_ROWS), lambda i: (i, 0, 0), memory_space=pltpu.SMEM)
    nxt = lambda: pl.BlockSpec((1, 1, COMBINE_ROWS), lambda i: (jnp.minimum(i + 1, steps - 1), 0, 0),
                               memory_space=pltpu.SMEM)
    d0 = dest0.reshape(steps, 1, COMBINE_ROWS)
    d1 = dest1.reshape(steps, 1, COMBINE_ROWS)
    return pl.pallas_call(
        _combine_kernel,
        grid=(steps,),
        in_specs=[
            cur(), cur(), nxt(), nxt(),
            pl.BlockSpec((COMBINE_ROWS, LANES), lambda i: (i, 0)),
            pl.BlockSpec((COMBINE_ROWS, d), lambda i: (i, 0)),
            pl.BlockSpec((1, N_MOD, d), lambda i: (i // tiles_per_batch, 0, 0)),
            pl.BlockSpec((1, d), lambda i: (0, 0)),
            pl.BlockSpec(memory_space=pl.ANY),
        ],
        out_specs=pl.BlockSpec((COMBINE_ROWS, d), lambda i: (i, 0)),
        out_shape=jax.ShapeDtypeStruct((n, d), F32),
        scratch_shapes=[pltpu.VMEM((2, TOP_K, COMBINE_ROWS * PACK_ROWS, LANES), jnp.int32),
                        pltpu.SemaphoreType.DMA((2,))],
        compiler_params=_params(("arbitrary",)),
        name="moe_combine",
    )(d0, d1, d0, d1, rinfo, x_new, mod3, final_g, ybuf)


def _rope_tables(seq):
    rows = seq // GRID_W
    row_idx = jnp.repeat(jnp.arange(rows, dtype=F32), GRID_W)
    col_idx = jnp.tile(jnp.arange(GRID_W, dtype=F32), rows)
    inv_freq = ROPE_THETA ** (-jnp.arange(0, ROPE_AXIS_DIM, 2, dtype=F32) / ROPE_AXIS_DIM)
    ang = jnp.stack([row_idx[:, None] * inv_freq, col_idx[:, None] * inv_freq], axis=1)
    cos = jnp.cos(ang)
    sin = jnp.sin(ang)
    cos_h = jnp.stack([cos, cos], axis=2).reshape(seq, HEAD_DIM)
    sin_h = jnp.stack([-sin, sin], axis=2).reshape(seq, HEAD_DIM)
    reps = LANES // HEAD_DIM
    return jnp.tile(cos_h, (1, reps)), jnp.tile(sin_h, (1, reps))


def kernel(x, c, ctx, c_ctx, w_mod, b_mod, norm1_g, w_in, q_norm_g, k_norm_g, conv_w, attn_out_g,
           conv_out_g, w_out, norm2_g, w_group, w_router, w_gate, w_up, w_down, final_g):
    assert w_mod.shape[0] == 1, "single-layer block"
    b, s, d = x.shape
    n = b * s
    assert b + 1 <= MOD_ROWS

    cond = jnp.zeros((MOD_ROWS, d), F32).at[:b].set(c).at[b].set(c_ctx)
    mod3 = _modulation(cond, w_mod[0], b_mod[0]).reshape(MOD_ROWS, N_MOD, d)

    cos_t, sin_t = _rope_tables(s)
    head_of = jnp.arange(ATTN_WIDTH) // HEAD_DIM
    bd = jnp.where(head_of[:, None] == head_of[None, :], 1.0 / HEAD_DIM, 0.0).astype(BF16)
    q, k, v, cv = _in_projection(
        x, ctx, mod3, norm1_g, w_in[0].astype(BF16),
        jnp.tile(q_norm_g[0], N_HEADS)[None], jnp.tile(k_norm_g[0], N_KV_HEADS)[None],
        cos_t, sin_t, conv_w[0], conv_out_g, bd)

    a = _attention(q, k, v, attn_out_g)

    wr = jnp.zeros((d, LANES), F32).at[:, :N_EXPERTS].set(w_router[0])
    wr = wr.at[:, N_EXPERTS:N_EXPERTS + N_GROUPS].set(w_group[0]).astype(BF16)
    ti = jnp.arange(OUT_ROWS)
    tri = (ti[:, None] < ti[None, :]).astype(F32)
    x_new, h2t, rinfo, rinfo_t, counts = _out_projection(
        a.reshape(n, ATTN_WIDTH), cv.reshape(n, CONV_WIDTH), w_out[0].astype(BF16),
        x.reshape(n, d), mod3, norm2_g, wr, tri, s)

    cnt = counts[:, 0].astype(jnp.int32)
    padded = ((cnt + MOE_ROWS - 1) // MOE_ROWS) * MOE_ROWS
    pends = jnp.cumsum(padded)
    pstarts = pends - padded
    experts = jnp.arange(N_EXPERTS, dtype=jnp.int32)[:, None]

    def slots(k):
        eid = rinfo_t[k].astype(jnp.int32)
        rank = rinfo_t[TOP_K + k].astype(jnp.int32)
        return jnp.sum(jnp.where(eid[None, :] == experts, pstarts[:, None], 0), axis=0) + rank

    dest0, dest1 = slots(0), slots(1)
    p_rows = n * TOP_K + N_EXPERTS * MOE_ROWS

    xs = _dispatch(pends, padded, dest0, dest1, h2t, p_rows)
    ybuf = _experts(pstarts // MOE_ROWS, padded // MOE_ROWS, xs, w_gate[0], w_up[0], w_down[0])
    out = _combine(dest0, dest1, rinfo, x_new, mod3, final_g.reshape(1, d), ybuf, s)
    return out.reshape(b, s, d)
```

```python
import functools

import jax
import jax.numpy as jnp
from jax import lax
from jax.experimental import pallas as pl
from jax.experimental.pallas import tpu as pltpu

F32 = jnp.float32
BF16 = jnp.bfloat16

D_MODEL = 1024
GRID_W = 64
ATTN_WIDTH = 512
N_HEADS = 8
N_KV_HEADS = 2
HEAD_DIM = 64
KV_REP = N_HEADS // N_KV_HEADS
KV_WIDTH = N_KV_HEADS * HEAD_DIM
CONV_WIDTH = 512
IN_COLS = ATTN_WIDTH + 2 * KV_WIDTH + 3 * CONV_WIDTH
ROPE_THETA = 10000.0
ROPE_AXIS_DIM = HEAD_DIM // 2
ROPE_FREQS = ROPE_AXIS_DIM // 2
N_GROUPS = 4
EXPERTS_PER_GROUP = 8
N_EXPERTS = N_GROUPS * EXPERTS_PER_GROUP
TOP_K = 2
D_EXPERT = 768
N_MOD = 6
EPS = 1e-6
LOG2_E = 1.4426950408889634
Q_SCALE = HEAD_DIM ** -0.5 * LOG2_E
V_LANES = 2 * HEAD_DIM

LANES = 128
ROW_TILE = 8
PACK_ROWS = 4
HIGH_HALF = -65536
MOD_ROWS = 16
IN_ROWS = 512
CONV_PAD = 8
ATTN_TQ = 128
ATTN_HEAD_SPLIT = 2
OUT_ROWS = 512
DISPATCH_ROWS = 1024
DMA_UNROLL = 8
MOE_ROWS = 256
EXPERT_IN_SLOTS = 3
COMBINE_ROWS = 256
VMEM_LIMIT = 56 * 1024 * 1024


def _params(semantics, vmem=None):
    return pltpu.CompilerParams(dimension_semantics=semantics,
                                vmem_limit_bytes=vmem if vmem else VMEM_LIMIT)


def _rms(x):
    return x * lax.rsqrt(jnp.mean(x * x, axis=-1, keepdims=True) + EPS)


def _store_row_tiles(ref, val):
    rows, width = val.shape
    assert width == 2 * PACK_ROWS * LANES
    half = width // 2

    def bits(v):
        return lax.bitcast_convert_type(v.astype(BF16).astype(F32), jnp.int32)

    words = lax.shift_right_logical(bits(val[:, :half]), 16) | (bits(val[:, half:]) & HIGH_HALF)
    for i in range(PACK_ROWS):
        ref[pl.ds(i, rows, stride=PACK_ROWS), :] = words[:, i * LANES:(i + 1) * LANES]


def _load_row_tiles(ref, rows, *lead):
    words = [ref[(*lead, pl.ds(i, rows, stride=PACK_ROWS), slice(None))] for i in range(PACK_ROWS)]
    low = [lax.bitcast_convert_type(lax.shift_left(w, 16), F32) for w in words]
    high = [lax.bitcast_convert_type(w & HIGH_HALF, F32) for w in words]
    return jnp.concatenate(low + high, axis=1)


def _row_tile(ref, row, *lead):
    return ref.at[(*lead, pl.ds(pl.multiple_of(row * PACK_ROWS, PACK_ROWS), PACK_ROWS))]


def _mod_kernel(c_ref, w_ref, b_ref, o_ref):
    c = c_ref[...]
    s = c * jax.nn.sigmoid(c)
    o_ref[...] = jnp.dot(s.astype(BF16), w_ref[...].astype(BF16),
                         preferred_element_type=F32) + b_ref[...]


def _modulation(cond, w_mod, b_mod):
    d = cond.shape[1]
    n_out = w_mod.shape[1]
    tn = 1024
    return pl.pallas_call(
        _mod_kernel,
        grid=(n_out // tn,),
        in_specs=[pl.BlockSpec((MOD_ROWS, d), lambda j: (0, 0)),
                  pl.BlockSpec((d, tn), lambda j: (0, j)),
                  pl.BlockSpec((1, tn), lambda j: (0, j))],
        out_specs=pl.BlockSpec((MOD_ROWS, tn), lambda j: (0, j)),
        out_shape=jax.ShapeDtypeStruct((MOD_ROWS, n_out), F32),
        compiler_params=_params(("arbitrary",)),
        name="modulation",
    )(cond, w_mod, b_mod.reshape(1, n_out))


def _with_ones(v):
    return jnp.concatenate([v, jnp.ones_like(v)], axis=1).astype(BF16)


def _rope(x, cos, sin_signed, is_lo):
    partner = jnp.where(is_lo, pltpu.roll(x, LANES - ROPE_FREQS, 1), pltpu.roll(x, ROPE_FREQS, 1))
    return x * cos + partner * sin_signed


def _inproj_kernel(x_ref, ctx_ref, mod_ref, cmod_ref, n1g_ref, w_ref, qg_ref, kg_ref,
                   cos_ref, sin_ref, convw_ref, cog_ref, bd_ref,
                   q_ref, k_ref, v_ref, cv_ref, p_s, gb_s):
    seq = x_ref.shape[1]
    ctx_len = ctx_ref.shape[1]
    g1 = n1g_ref[...]
    bd = bd_ref[...]
    bd_kv = bd_ref[0:KV_WIDTH, 0:KV_WIDTH]
    kg = kg_ref[...]
    qg = qg_ref[...]

    def head_ms(z, m):
        return jnp.dot((z * z).astype(BF16), m, preferred_element_type=F32)

    cmod = cmod_ref[0]
    hc = _rms(ctx_ref[0]) * g1 * (1.0 + cmod[1:2]) + cmod[0:1]
    zc = jnp.dot(hc.astype(BF16), w_ref[:, ATTN_WIDTH:ATTN_WIDTH + 2 * KV_WIDTH],
                 preferred_element_type=F32)
    kc = zc[:, :KV_WIDTH]
    kc = kc * lax.rsqrt(head_ms(kc, bd_kv) + EPS) * kg
    vc = zc[:, KV_WIDTH:]
    for g in range(N_KV_HEADS):
        k_ref[0, g, 0:ctx_len, :] = kc[:, g * HEAD_DIM:(g + 1) * HEAD_DIM].astype(BF16)
        v_ref[0, g, 0:ctx_len, :] = _with_ones(vc[:, g * HEAD_DIM:(g + 1) * HEAD_DIM])

    zeros = jnp.zeros((CONV_PAD, CONV_WIDTH), F32)
    p_s[0:CONV_PAD, :] = zeros
    p_s[CONV_PAD + seq:CONV_PAD + seq + CONV_PAD, :] = zeros

    mod = mod_ref[0]
    sh1 = mod[0:1]
    sc1 = mod[1:2]
    lane = lax.broadcasted_iota(jnp.int32, (IN_ROWS, LANES), 1)
    is_lo = (lane // ROPE_FREQS) % 2 == 0

    def proj_chunk(c, carry):
        r0 = pl.multiple_of(c * IN_ROWS, IN_ROWS)
        h = (_rms(x_ref[0, pl.ds(r0, IN_ROWS), :]) * g1 * (1.0 + sc1) + sh1).astype(BF16)
        cos = cos_ref[pl.ds(r0, IN_ROWS), :]
        sin = sin_ref[pl.ds(r0, IN_ROWS), :]
        zq = jnp.dot(h, w_ref[:, 0:ATTN_WIDTH], preferred_element_type=F32)
        qn = zq * lax.rsqrt(head_ms(zq, bd) + EPS) * qg
        for j in range(ATTN_WIDTH // LANES):
            blk = _rope(qn[:, j * LANES:(j + 1) * LANES], cos, sin, is_lo) * Q_SCALE
            for hh in range(LANES // HEAD_DIM):
                head = j * (LANES // HEAD_DIM) + hh
                q_ref[0, head, pl.ds(r0, IN_ROWS), :] = (
                    blk[:, hh * HEAD_DIM:(hh + 1) * HEAD_DIM].astype(BF16))
        zkv = jnp.dot(h, w_ref[:, ATTN_WIDTH:ATTN_WIDTH + 2 * KV_WIDTH], preferred_element_type=F32)
        kx = zkv[:, :KV_WIDTH]
        kx = _rope(kx * lax.rsqrt(head_ms(kx, bd_kv) + EPS) * kg, cos, sin, is_lo)
        vx = zkv[:, KV_WIDTH:]
        for g in range(N_KV_HEADS):
            k_ref[0, g, pl.ds(ctx_len + r0, IN_ROWS), :] = kx[:, g * HEAD_DIM:(g + 1) * HEAD_DIM].astype(BF16)
            v_ref[0, g, pl.ds(ctx_len + r0, IN_ROWS), :] = _with_ones(vx[:, g * HEAD_DIM:(g + 1) * HEAD_DIM])
        c0 = ATTN_WIDTH + 2 * KV_WIDTH
        gb_s[pl.ds(r0, IN_ROWS), :] = jnp.dot(h, w_ref[:, c0:c0 + CONV_WIDTH], preferred_element_type=F32)
        zc_ = jnp.dot(h, w_ref[:, c0 + CONV_WIDTH:c0 + 2 * CONV_WIDTH], preferred_element_type=F32)
        zu = jnp.dot(h, w_ref[:, c0 + 2 * CONV_WIDTH:c0 + 3 * CONV_WIDTH], preferred_element_type=F32)
        p_s[pl.ds(CONV_PAD + r0, IN_ROWS), :] = zc_ * zu
        return carry

    lax.fori_loop(0, seq // IN_ROWS, proj_chunk, 0)

    cw = convw_ref[...]
    cog = cog_ref[...]

    def conv_chunk(c, carry):
        r0 = pl.multiple_of(c * IN_ROWS, IN_ROWS)
        win = p_s[pl.ds(r0, IN_ROWS + 2 * CONV_PAD), :]
        n_win = IN_ROWS + 2 * CONV_PAD
        prev = pltpu.roll(win, 1, 0)[CONV_PAD:CONV_PAD + IN_ROWS]
        cur = win[CONV_PAD:CONV_PAD + IN_ROWS]
        nxt = pltpu.roll(win, n_win - 1, 0)[CONV_PAD:CONV_PAD + IN_ROWS]
        y = cw[0:1] * prev + cw[1:2] * cur + cw[2:3] * nxt
        cvv = gb_s[pl.ds(r0, IN_ROWS), :] * y
        cvn = cvv * lax.rsqrt(head_ms(cvv, bd) + EPS) * cog
        cv_ref[0, pl.ds(r0, IN_ROWS), :] = cvn.astype(BF16)
        return carry

    lax.fori_loop(0, seq // IN_ROWS, conv_chunk, 0)


def _in_projection(x, ctx, mod3, norm1_g, w_in_bf, qg_t, kg_t, cos_t, sin_t, conv_w, conv_out_g, bd):
    b, s, d = x.shape
    ctx_len = ctx.shape[1]
    n_keys = ctx_len + s
    const = lambda *shape: pl.BlockSpec(shape, lambda i: (0,) * len(shape))
    return pl.pallas_call(
        _inproj_kernel,
        grid=(b,),
        in_specs=[
            pl.BlockSpec((1, s, d), lambda i: (i, 0, 0)),
            pl.BlockSpec((1, ctx_len, d), lambda i: (i, 0, 0)),
            pl.BlockSpec((1, N_MOD, d), lambda i: (i, 0, 0)),
            pl.BlockSpec((1, N_MOD, d), lambda i: (b, 0, 0)),
            const(1, d),
            const(d, IN_COLS),
            const(1, ATTN_WIDTH),
            const(1, KV_WIDTH),
            const(s, LANES),
            const(s, LANES),
            const(3, CONV_WIDTH),
            const(1, CONV_WIDTH),
            const(ATTN_WIDTH, ATTN_WIDTH),
        ],
        out_specs=[
            pl.BlockSpec((1, N_HEADS, s, HEAD_DIM), lambda i: (i, 0, 0, 0)),
            pl.BlockSpec((1, N_KV_HEADS, n_keys, HEAD_DIM), lambda i: (i, 0, 0, 0)),
            pl.BlockSpec((1, N_KV_HEADS, n_keys, V_LANES), lambda i: (i, 0, 0, 0)),
            pl.BlockSpec((1, s, CONV_WIDTH), lambda i: (i, 0, 0)),
        ],
        out_shape=[
            jax.ShapeDtypeStruct((b, N_HEADS, s, HEAD_DIM), BF16),
            jax.ShapeDtypeStruct((b, N_KV_HEADS, n_keys, HEAD_DIM), BF16),
            jax.ShapeDtypeStruct((b, N_KV_HEADS, n_keys, V_LANES), BF16),
            jax.ShapeDtypeStruct((b, s, CONV_WIDTH), BF16),
        ],
        scratch_shapes=[pltpu.VMEM((s + 2 * CONV_PAD, CONV_WIDTH), F32),
                        pltpu.VMEM((s, CONV_WIDTH), F32)],
        compiler_params=_params(("arbitrary",)),
        name="in_projection",
    )(x, ctx, mod3, mod3, norm1_g, w_in_bf, qg_t, kg_t, cos_t, sin_t, conv_w, conv_out_g, bd)


def _attn_kernel(q_ref, k_ref, v_ref, g_ref, o_ref):
    tq = q_ref.shape[2]
    pieces = []
    hs = KV_REP // ATTN_HEAD_SPLIT
    subs = [(g, g * KV_REP + j * hs) for g in range(N_KV_HEADS) for j in range(ATTN_HEAD_SPLIT)]
    scores = []
    for g, h0 in subs:
        qs = jnp.concatenate([q_ref[0, h0 + r] for r in range(hs)], axis=0)
        scores.append(lax.dot_general(qs, k_ref[0, g], (((1,), (1,)), ((), ())),
                                      preferred_element_type=F32))
    probs = [jnp.exp2(s - jnp.max(s, axis=-1, keepdims=True)).astype(BF16) for s in scores]
    lane = lax.broadcasted_iota(jnp.int32, (hs * tq, LANES), 1)
    for (g, h0), p in zip(subs, probs):
        ov = jnp.dot(p, v_ref[0, g], preferred_element_type=F32)
        o = ov / pltpu.roll(ov, HEAD_DIM, 1)
        ms = jnp.sum(jnp.where(lane < HEAD_DIM, o * o, 0.0), axis=-1, keepdims=True) * (1.0 / HEAD_DIM)
        o = o * lax.rsqrt(ms + EPS)
        for r in range(hs):
            pieces.append(o[r * tq:(r + 1) * tq, 0:HEAD_DIM])
    o_ref[0] = (jnp.concatenate(pieces, axis=1) * g_ref[...]).astype(BF16)


def _attention(q, k, v, attn_out_g):
    b, _, s, _ = q.shape
    n_keys = k.shape[2]
    return pl.pallas_call(
        _attn_kernel,
        grid=(b, s // ATTN_TQ),
        in_specs=[
            pl.BlockSpec((1, N_HEADS, ATTN_TQ, HEAD_DIM), lambda i, j: (i, 0, j, 0)),
            pl.BlockSpec((1, N_KV_HEADS, n_keys, HEAD_DIM), lambda i, j: (i, 0, 0, 0)),
            pl.BlockSpec((1, N_KV_HEADS, n_keys, V_LANES), lambda i, j: (i, 0, 0, 0)),
            pl.BlockSpec((1, ATTN_WIDTH), lambda i, j: (0, 0)),
        ],
        out_specs=pl.BlockSpec((1, ATTN_TQ, ATTN_WIDTH), lambda i, j: (i, j, 0)),
        out_shape=jax.ShapeDtypeStruct((b, s, ATTN_WIDTH), BF16),
        compiler_params=_params(("arbitrary", "arbitrary")),
        name="attention",
    )(q, k, v, attn_out_g)


def _outproj_kernel(a_ref, cv_ref, w_ref, x_ref, mod_ref, n2g_ref, wr_ref, upper_ref,
                    xn_ref, h2_ref, ri_ref, rit_ref, cnt_ref, carry_s):
    i = pl.program_id(0)

    @pl.when(i == 0)
    def _():
        carry_s[...] = jnp.zeros_like(carry_s)

    mod = mod_ref[0]
    merged = (jnp.dot(a_ref[...], w_ref[0:ATTN_WIDTH, :], preferred_element_type=F32)
              + jnp.dot(cv_ref[...], w_ref[ATTN_WIDTH:, :], preferred_element_type=F32))
    xn = x_ref[...] + mod[2:3] * merged
    xn_ref[...] = xn
    h2 = _rms(xn) * n2g_ref[...] * (1.0 + mod[4:5]) + mod[3:4]
    _store_row_tiles(h2_ref, h2)

    logits = jnp.dot(h2.astype(BF16), wr_ref[...], preferred_element_type=F32)
    lt = jnp.transpose(logits)
    rows = logits.shape[0]
    row = lax.broadcasted_iota(jnp.int32, (EXPERTS_PER_GROUP, rows), 0).astype(F32)
    neg = jnp.float32(-jnp.inf)
    none = jnp.float32(EXPERTS_PER_GROUP)

    def first_at(mask):
        return jnp.min(jnp.where(mask, row, none), axis=0, keepdims=True)

    gvalid = row < N_GROUPS
    lg = jnp.where(gvalid, lt[N_EXPERTS:N_EXPERTS + EXPERTS_PER_GROUP], neg)
    ge = jnp.exp(lg - jnp.max(lg, axis=0, keepdims=True))
    g_prob = ge / jnp.sum(ge, axis=0, keepdims=True)
    g_w = jnp.max(g_prob, axis=0, keepdims=True)
    g_sel = first_at(gvalid & (g_prob == g_w))

    le = lt[(N_GROUPS - 1) * EXPERTS_PER_GROUP:N_GROUPS * EXPERTS_PER_GROUP]
    for g in range(N_GROUPS - 2, -1, -1):
        le = jnp.where(g_sel == g, lt[g * EXPERTS_PER_GROUP:(g + 1) * EXPERTS_PER_GROUP], le)
    ee = jnp.exp(le - jnp.max(le, axis=0, keepdims=True))
    e_prob = ee / jnp.sum(ee, axis=0, keepdims=True)
    p1 = jnp.max(e_prob, axis=0, keepdims=True)
    i1 = first_at(e_prob == p1)
    rest = row != i1
    p2 = jnp.max(jnp.where(rest, e_prob, -1.0), axis=0, keepdims=True)
    i2 = first_at(rest & (e_prob == p2))
    psum = p1 + p2
    w1 = g_w * (p1 / psum)
    w2 = g_w * (p2 / psum)

    def expert_rows(i_sel):
        return jnp.concatenate([jnp.where((g_sel == g) & (row == i_sel), 1.0, 0.0)
                                for g in range(N_GROUPS)], axis=0)

    hit1 = expert_rows(i1)
    hit2 = expert_rows(i2)
    onehot = hit1 + hit2
    before = jnp.dot(onehot, upper_ref[...], preferred_element_type=F32) + carry_s[:, 0:1]
    r1 = jnp.sum(hit1 * before, axis=0, keepdims=True)
    r2 = jnp.sum(hit2 * before, axis=0, keepdims=True)
    carry_s[...] = carry_s[...] + jnp.sum(onehot, axis=1, keepdims=True)

    base = g_sel * EXPERTS_PER_GROUP
    info_t = jnp.where(row == 0, base + i1,
             jnp.where(row == 1, base + i2,
             jnp.where(row == 2, r1,
             jnp.where(row == 3, r2,
             jnp.where(row == 4, w1,
             jnp.where(row == 5, w2, 0.0))))))
    rit_ref[...] = info_t
    ri_ref[...] = jnp.transpose(
        jnp.concatenate([info_t, jnp.zeros((LANES - ROW_TILE, rows), F32)], axis=0))
    cnt_ref[...] = carry_s[...]


def _out_projection(a, cv, w_out_bf, x2, mod3, norm2_g, wr_bf, tri, seq):
    n, d = x2.shape
    tiles_per_batch = seq // OUT_ROWS
    return pl.pallas_call(
        _outproj_kernel,
        grid=(n // OUT_ROWS,),
        in_specs=[
            pl.BlockSpec((OUT_ROWS, ATTN_WIDTH), lambda i: (i, 0)),
            pl.BlockSpec((OUT_ROWS, CONV_WIDTH), lambda i: (i, 0)),
            pl.BlockSpec((d, d), lambda i: (0, 0)),
            pl.BlockSpec((OUT_ROWS, d), lambda i: (i, 0)),
            pl.BlockSpec((1, N_MOD, d), lambda i: (i // tiles_per_batch, 0, 0)),
            pl.BlockSpec((1, d), lambda i: (0, 0)),
            pl.BlockSpec((d, LANES), lambda i: (0, 0)),
            pl.BlockSpec((OUT_ROWS, OUT_ROWS), lambda i: (0, 0)),
        ],
        out_specs=[
            pl.BlockSpec((OUT_ROWS, d), lambda i: (i, 0)),
            pl.BlockSpec((OUT_ROWS * PACK_ROWS, LANES), lambda i: (i, 0)),
            pl.BlockSpec((OUT_ROWS, LANES), lambda i: (i, 0)),
            pl.BlockSpec((ROW_TILE, OUT_ROWS), lambda i: (0, i)),
            pl.BlockSpec((N_EXPERTS, LANES), lambda i: (0, 0)),
        ],
        out_shape=[
            jax.ShapeDtypeStruct((n, d), F32),
            jax.ShapeDtypeStruct((n * PACK_ROWS, LANES), jnp.int32),
            jax.ShapeDtypeStruct((n, LANES), F32),
            jax.ShapeDtypeStruct((ROW_TILE, n), F32),
            jax.ShapeDtypeStruct((N_EXPERTS, LANES), F32),
        ],
        scratch_shapes=[pltpu.VMEM((N_EXPERTS, LANES), F32)],
        compiler_params=_params(("arbitrary",)),
        name="out_projection_routing",
    )(a, cv, w_out_bf, x2, mod3, norm2_g, wr_bf, tri)


def _dispatch_kernel(pend_ref, padded_ref, d0_ref, d1_ref, h_ref, xs_hbm, zero_s, sem):
    tile = MOE_ROWS * PACK_ROWS

    @pl.when(pl.program_id(0) == 0)
    def _():
        zero_s[...] = jnp.zeros_like(zero_s)

        def last_tile(e):
            start = pl.multiple_of((pend_ref[e] - MOE_ROWS) * PACK_ROWS, tile)
            return pltpu.make_async_copy(zero_s, xs_hbm.at[pl.ds(start, tile)], sem)

        def spare_tile(j):
            return pltpu.make_async_copy(zero_s, xs_hbm.at[pl.ds(j * tile, tile)], sem)

        n_tiles = xs_hbm.shape[0] // tile
        used = pend_ref[N_EXPERTS - 1] // MOE_ROWS
        for e in range(N_EXPERTS):
            @pl.when(padded_ref[e] > 0)
            def _():
                last_tile(e).start()
        for j in range(n_tiles - N_EXPERTS, n_tiles):
            @pl.when(j >= used)
            def _():
                spare_tile(j).start()
        for e in range(N_EXPERTS):
            @pl.when(padded_ref[e] > 0)
            def _():
                last_tile(e).wait()
        for j in range(n_tiles - N_EXPERTS, n_tiles):
            @pl.when(j >= used)
            def _():
                spare_tile(j).wait()

    def start(j, carry):
        for u in range(DMA_UNROLL):
            t = j * DMA_UNROLL + u
            for k, d_ref in enumerate((d0_ref, d1_ref)):
                pltpu.make_async_copy(_row_tile(h_ref, t), _row_tile(xs_hbm, d_ref[0, 0, t]),
                                      sem).start(priority=k)
        return carry

    lax.fori_loop(0, DISPATCH_ROWS // DMA_UNROLL, start, 0)

    def drain(j, carry):
        for _ in range(DMA_UNROLL * TOP_K):
            pltpu.make_async_copy(_row_tile(h_ref, 0), _row_tile(xs_hbm, 0), sem).wait()
        return carry

    lax.fori_loop(0, DISPATCH_ROWS // DMA_UNROLL, drain, 0)


def _dispatch(pends, padded, dest0, dest1, h2t, p_rows):
    n = h2t.shape[0] // PACK_ROWS
    steps = n // DISPATCH_ROWS
    smem_rows = lambda: pl.BlockSpec((1, 1, DISPATCH_ROWS), lambda i, pe, pa: (i, 0, 0),
                                     memory_space=pltpu.SMEM)
    return pl.pallas_call(
        _dispatch_kernel,
        grid_spec=pltpu.PrefetchScalarGridSpec(
            num_scalar_prefetch=2,
            grid=(steps,),
            in_specs=[
                smem_rows(),
                smem_rows(),
                pl.BlockSpec((DISPATCH_ROWS * PACK_ROWS, LANES), lambda i, pe, pa: (i, 0)),
            ],
            out_specs=pl.BlockSpec(memory_space=pl.ANY),
            scratch_shapes=[pltpu.VMEM((MOE_ROWS * PACK_ROWS, LANES), jnp.int32),
                            pltpu.SemaphoreType.DMA(())],
        ),
        out_shape=jax.ShapeDtypeStruct((p_rows * PACK_ROWS, LANES), jnp.int32),
        compiler_params=_params(("arbitrary",)),
        name="moe_dispatch",
    )(pends, padded, dest0.reshape(steps, 1, DISPATCH_ROWS), dest1.reshape(steps, 1, DISPATCH_ROWS), h2t)


def _experts_kernel(first_ref, count_ref, xs_hbm, wg_ref, wu_ref, wd_ref, y_hbm,
                    x_s, y_s, wg_s, wu_s, wd_s, in_sem, out_sem):
    e = pl.program_id(0)
    last = pl.num_programs(0) - 1
    n = count_ref[e]
    tile = MOE_ROWS * PACK_ROWS
    ahead = EXPERT_IN_SLOTS - 1

    def rows(first, j):
        return pl.ds(pl.multiple_of((first + j) * tile, tile), tile)

    def fetch(first, j):
        slot = j % EXPERT_IN_SLOTS
        return pltpu.make_async_copy(xs_hbm.at[rows(first, j)], x_s.at[slot], in_sem.at[slot])

    def writeback(first, j):
        slot = j % 2
        return pltpu.make_async_copy(y_s.at[slot], y_hbm.at[rows(first, j)], out_sem.at[slot])

    def start_head(ex):
        for j in range(ahead):
            @pl.when(j < count_ref[ex])
            def _():
                fetch(first_ref[ex], j).start(priority=1)

    @pl.when(n > 0)
    def _():
        first = first_ref[e]

        @pl.when(jnp.logical_or(e == 0, count_ref[jnp.maximum(e - 1, 0)] == 0))
        def _():
            start_head(e)

        wg_s[...] = wg_ref[0].astype(BF16)
        wu_s[...] = wu_ref[0].astype(BF16)
        wd_s[...] = wd_ref[0].astype(BF16)

        def tile_step(j, carry):
            @pl.when(j + ahead < n)
            def _():
                fetch(first, j + ahead).start(priority=1)

            fetch(first, j).wait()

            @pl.when(j >= 2)
            def _():
                writeback(first, j - 2).wait()

            x = _load_row_tiles(x_s, MOE_ROWS, j % EXPERT_IN_SLOTS).astype(BF16)
            g = jnp.dot(x, wg_s[...], preferred_element_type=F32)
            u = jnp.dot(x, wu_s[...], preferred_element_type=F32)
            h = ((g * jax.nn.sigmoid(g)) * u).astype(BF16)
            _store_row_tiles(y_s.at[j % 2], jnp.dot(h, wd_s[...], preferred_element_type=F32))
            writeback(first, j).start(priority=1)
            return carry

        lax.fori_loop(0, n, tile_step, 0)

        @pl.when(e < last)
        def _():
            start_head(jnp.minimum(e + 1, last))

        @pl.when(n >= 2)
        def _():
            writeback(first, n - 2).wait()

        writeback(first, n - 1).wait()


def _experts(first_tile, tile_count, xs, w_gate, w_up, w_down):
    n_exp, d, d_exp = w_gate.shape
    tile = MOE_ROWS * PACK_ROWS
    w_map = lambda i, ft, tc: (i, 0, 0)
    return pl.pallas_call(
        _experts_kernel,
        grid_spec=pltpu.PrefetchScalarGridSpec(
            num_scalar_prefetch=2,
            grid=(n_exp,),
            in_specs=[
                pl.BlockSpec(memory_space=pl.ANY),
                pl.BlockSpec((1, d, d_exp), w_map),
                pl.BlockSpec((1, d, d_exp), w_map),
                pl.BlockSpec((1, d_exp, d), w_map),
            ],
            out_specs=pl.BlockSpec(memory_space=pl.ANY),
            scratch_shapes=[
                pltpu.VMEM((EXPERT_IN_SLOTS, tile, LANES), jnp.int32),
                pltpu.VMEM((2, tile, LANES), jnp.int32),
                pltpu.VMEM((d, d_exp), BF16),
                pltpu.VMEM((d, d_exp), BF16),
                pltpu.VMEM((d_exp, d), BF16),
                pltpu.SemaphoreType.DMA((EXPERT_IN_SLOTS,)),
                pltpu.SemaphoreType.DMA((2,)),
            ],
        ),
        out_shape=jax.ShapeDtypeStruct(xs.shape, xs.dtype),
        input_output_aliases={2: 0},
        compiler_params=_params(("arbitrary",)),
        name="moe_experts",
    )(first_tile, tile_count, xs, w_gate, w_up, w_down)


def _combine_kernel(d0_ref, d1_ref, d0n_ref, d1n_ref, ri_ref, xn_ref, mod_ref, fg_ref, y_hbm,
                    o_ref, buf, sem):
    i = pl.program_id(0)
    slot = i % 2

    def gather(refs, to_slot):
        def start(j, carry):
            for u in range(DMA_UNROLL):
                t = j * DMA_UNROLL + u
                for k, d_ref in enumerate(refs):
                    pltpu.make_async_copy(_row_tile(y_hbm, d_ref[0, 0, t]), _row_tile(buf, t, to_slot, k),
                                          sem.at[to_slot]).start(priority=k)
            return carry

        lax.fori_loop(0, COMBINE_ROWS // DMA_UNROLL, start, 0)

    @pl.when(i == 0)
    def _():
        gather((d0_ref, d1_ref), 0)

    @pl.when(i + 1 < pl.num_programs(0))
    def _():
        gather((d0n_ref, d1n_ref), 1 - slot)

    def drain(j, carry):
        for _ in range(DMA_UNROLL * TOP_K):
            pltpu.make_async_copy(_row_tile(y_hbm, 0), _row_tile(buf, 0, slot, 0), sem.at[slot]).wait()
        return carry

    lax.fori_loop(0, COMBINE_ROWS // DMA_UNROLL, drain, 0)

    ri = ri_ref[...]
    y = (_load_row_tiles(buf, COMBINE_ROWS, slot, 0) * ri[:, 4:5]
         + _load_row_tiles(buf, COMBINE_ROWS, slot, 1) * ri[:, 5:6])
    xf = xn_ref[...] + mod_ref[0][5:6] * y
    o_ref[...] = _rms(xf) * fg_ref[...]


def _combine(dest0, dest1, rinfo, x_new, mod3, final_g, ybuf, seq):
    n, d = x_new.shape
    steps = n // COMBINE_ROWS
    tiles_per_batch = seq // COMBINE_ROWS
    cur = lambda: pl.BlockSpec((1, 1, COMBINE_ROWS), lambda i: (i, 0, 0), memory_space=pltpu.SMEM)
    nxt = lambda: pl.BlockSpec((1, 1, COMBINE_ROWS), lambda i: (jnp.minimum(i + 1, steps - 1), 0, 0),
                               memory_space=pltpu.SMEM)
    d0 = dest0.reshape(steps, 1, COMBINE_ROWS)
    d1 = dest1.reshape(steps, 1, COMBINE_ROWS)
    return pl.pallas_call(
        _combine_kernel,
        grid=(steps,),
        in_specs=[
            cur(), cur(), nxt(), nxt(),
            pl.BlockSpec((COMBINE_ROWS, LANES), lambda i: (i, 0)),
            pl.BlockSpec((COMBINE_ROWS, d), lambda i: (i, 0)),
            pl.BlockSpec((1, N_MOD, d), lambda i: (i // tiles_per_batch, 0, 0)),
            pl.BlockSpec((1, d), lambda i: (0, 0)),
            pl.BlockSpec(memory_space=pl.ANY),
        ],
        out_specs=pl.BlockSpec((COMBINE_ROWS, d), lambda i: (i, 0)),
        out_shape=jax.ShapeDtypeStruct((n, d), F32),
        scratch_shapes=[pltpu.VMEM((2, TOP_K, COMBINE_ROWS * PACK_ROWS, LANES), jnp.int32),
                        pltpu.SemaphoreType.DMA((2,))],
        compiler_params=_params(("arbitrary",)),
        name="moe_combine",
    )(d0, d1, d0, d1, rinfo, x_new, mod3, final_g, ybuf)


def _rope_tables(seq):
    rows = seq // GRID_W
    row_idx = jnp.repeat(jnp.arange(rows, dtype=F32), GRID_W)
    col_idx = jnp.tile(jnp.arange(GRID_W, dtype=F32), rows)
    inv_freq = ROPE_THETA ** (-jnp.arange(0, ROPE_AXIS_DIM, 2, dtype=F32) / ROPE_AXIS_DIM)
    ang = jnp.stack([row_idx[:, None] * inv_freq, col_idx[:, None] * inv_freq], axis=1)
    cos = jnp.cos(ang)
    sin = jnp.sin(ang)
    cos_h = jnp.stack([cos, cos], axis=2).reshape(seq, HEAD_DIM)
    sin_h = jnp.stack([-sin, sin], axis=2).reshape(seq, HEAD_DIM)
    reps = LANES // HEAD_DIM
    return jnp.tile(cos_h, (1, reps)), jnp.tile(sin_h, (1, reps))


def kernel(x, c, ctx, c_ctx, w_mod, b_mod, norm1_g, w_in, q_norm_g, k_norm_g, conv_w, attn_out_g,
           conv_out_g, w_out, norm2_g, w_group, w_router, w_gate, w_up, w_down, final_g):
    assert w_mod.shape[0] == 1, "single-layer block"
    b, s, d = x.shape
    n = b * s
    assert b + 1 <= MOD_ROWS

    cond = jnp.zeros((MOD_ROWS, d), F32).at[:b].set(c).at[b].set(c_ctx)
    mod3 = _modulation(cond, w_mod[0], b_mod[0]).reshape(MOD_ROWS, N_MOD, d)

    cos_t, sin_t = _rope_tables(s)
    head_of = jnp.arange(ATTN_WIDTH) // HEAD_DIM
    bd = jnp.where(head_of[:, None] == head_of[None, :], 1.0 / HEAD_DIM, 0.0).astype(BF16)
    q, k, v, cv = _in_projection(
        x, ctx, mod3, norm1_g, w_in[0].astype(BF16),
        jnp.tile(q_norm_g[0], N_HEADS)[None], jnp.tile(k_norm_g[0], N_KV_HEADS)[None],
        cos_t, sin_t, conv_w[0], conv_out_g, bd)

    a = _attention(q, k, v, attn_out_g)

    wr = jnp.zeros((d, LANES), F32).at[:, :N_EXPERTS].set(w_router[0])
    wr = wr.at[:, N_EXPERTS:N_EXPERTS + N_GROUPS].set(w_group[0]).astype(BF16)
    ti = jnp.arange(OUT_ROWS)
    tri = (ti[:, None] < ti[None, :]).astype(F32)
    x_new, h2t, rinfo, rinfo_t, counts = _out_projection(
        a.reshape(n, ATTN_WIDTH), cv.reshape(n, CONV_WIDTH), w_out[0].astype(BF16),
        x.reshape(n, d), mod3, norm2_g, wr, tri, s)

    cnt = counts[:, 0].astype(jnp.int32)
    padded = ((cnt + MOE_ROWS - 1) // MOE_ROWS) * MOE_ROWS
    pends = jnp.cumsum(padded)
    pstarts = pends - padded
    experts = jnp.arange(N_EXPERTS, dtype=jnp.int32)[:, None]

    def slots(k):
        eid = rinfo_t[k].astype(jnp.int32)
        rank = rinfo_t[TOP_K + k].astype(jnp.int32)
        return jnp.sum(jnp.where(eid[None, :] == experts, pstarts[:, None], 0), axis=0) + rank

    dest0, dest1 = slots(0), slots(1)
    p_rows = n * TOP_K + N_EXPERTS * MOE_ROWS

    xs = _dispatch(pends, padded, dest0, dest1, h2t, p_rows)
    ybuf = _experts(pstarts // MOE_ROWS, padded // MOE_ROWS, xs, w_gate[0], w_up[0], w_down[0])
    out = _combine(dest0, dest1, rinfo, x_new, mod3, final_g.reshape(1, d), ybuf, s)
    return out.reshape(b, s, d)
```

```python
import functools

import jax
import jax.numpy as jnp
from jax import lax
from jax.experimental import pallas as pl
from jax.experimental.pallas import tpu as pltpu

F32 = jnp.float32
BF16 = jnp.bfloat16

D_MODEL = 1024
GRID_W = 64
ATTN_WIDTH = 512
N_HEADS = 8
N_KV_HEADS = 2
HEAD_DIM = 64
KV_REP = N_HEADS // N_KV_HEADS
KV_WIDTH = N_KV_HEADS * HEAD_DIM
CONV_WIDTH = 512
IN_COLS = ATTN_WIDTH + 2 * KV_WIDTH + 3 * CONV_WIDTH
ROPE_THETA = 10000.0
ROPE_AXIS_DIM = HEAD_DIM // 2
ROPE_FREQS = ROPE_AXIS_DIM // 2
N_GROUPS = 4
EXPERTS_PER_GROUP = 8
N_EXPERTS = N_GROUPS * EXPERTS_PER_GROUP
TOP_K = 2
D_EXPERT = 768
N_MOD = 6
EPS = 1e-6
LOG2_E = 1.4426950408889634
Q_SCALE = HEAD_DIM ** -0.5 * LOG2_E
V_LANES = 2 * HEAD_DIM

LANES = 128
ROW_TILE = 8
PACK_ROWS = 4
HIGH_HALF = -65536
MOD_ROWS = 16
IN_ROWS = 512
CONV_PAD = 8
ATTN_TQ = 256
ATTN_HEAD_SPLIT = 4
OUT_ROWS = 1024
DISPATCH_ROWS = 2048
DMA_UNROLL = 8
MOE_ROWS = 256
EXPERT_IN_SLOTS = 3
COMBINE_ROWS = 512
VMEM_LIMIT = 56 * 1024 * 1024


def _params(semantics, vmem=None):
    return pltpu.CompilerParams(dimension_semantics=semantics,
                                vmem_limit_bytes=vmem if vmem else VMEM_LIMIT)


def _rms(x):
    return x * lax.rsqrt(jnp.mean(x * x, axis=-1, keepdims=True) + EPS)


def _store_row_tiles(ref, val):
    rows, width = val.shape
    assert width == 2 * PACK_ROWS * LANES
    half = width // 2

    def bits(v):
        return lax.bitcast_convert_type(v.astype(BF16).astype(F32), jnp.int32)

    words = lax.shift_right_logical(bits(val[:, :half]), 16) | (bits(val[:, half:]) & HIGH_HALF)
    for i in range(PACK_ROWS):
        ref[pl.ds(i, rows, stride=PACK_ROWS), :] = words[:, i * LANES:(i + 1) * LANES]


def _load_row_tiles(ref, rows, *lead):
    words = [ref[(*lead, pl.ds(i, rows, stride=PACK_ROWS), slice(None))] for i in range(PACK_ROWS)]
    low = [lax.bitcast_convert_type(lax.shift_left(w, 16), F32) for w in words]
    high = [lax.bitcast_convert_type(w & HIGH_HALF, F32) for w in words]
    return jnp.concatenate(low + high, axis=1)


def _row_tile(ref, row, *lead):
    return ref.at[(*lead, pl.ds(pl.multiple_of(row * PACK_ROWS, PACK_ROWS), PACK_ROWS))]


def _mod_kernel(c_ref, w_ref, b_ref, o_ref):
    c = c_ref[...]
    s = c * jax.nn.sigmoid(c)
    o_ref[...] = jnp.dot(s.astype(BF16), w_ref[...].astype(BF16),
                         preferred_element_type=F32) + b_ref[...]


def _modulation(cond, w_mod, b_mod):
    d = cond.shape[1]
    n_out = w_mod.shape[1]
    tn = 1024
    return pl.pallas_call(
        _mod_kernel,
        grid=(n_out // tn,),
        in_specs=[pl.BlockSpec((MOD_ROWS, d), lambda j: (0, 0)),
                  pl.BlockSpec((d, tn), lambda j: (0, j)),
                  pl.BlockSpec((1, tn), lambda j: (0, j))],
        out_specs=pl.BlockSpec((MOD_ROWS, tn), lambda j: (0, j)),
        out_shape=jax.ShapeDtypeStruct((MOD_ROWS, n_out), F32),
        compiler_params=_params(("arbitrary",)),
        name="modulation",
    )(cond, w_mod, b_mod.reshape(1, n_out))


def _with_ones(v):
    return jnp.concatenate([v, jnp.ones_like(v)], axis=1).astype(BF16)


def _rope(x, cos, sin_signed, is_lo):
    partner = jnp.where(is_lo, pltpu.roll(x, LANES - ROPE_FREQS, 1), pltpu.roll(x, ROPE_FREQS, 1))
    return x * cos + partner * sin_signed


def _inproj_kernel(x_ref, ctx_ref, mod_ref, cmod_ref, n1g_ref, w_ref, qg_ref, kg_ref,
                   cos_ref, sin_ref, convw_ref, cog_ref, bd_ref,
                   q_ref, k_ref, v_ref, cv_ref, p_s, gb_s):
    seq = x_ref.shape[1]
    ctx_len = ctx_ref.shape[1]
    g1 = n1g_ref[...]
    bd = bd_ref[...]
    bd_kv = bd_ref[0:KV_WIDTH, 0:KV_WIDTH]
    kg = kg_ref[...]
    qg = qg_ref[...]

    def head_ms(z, m):
        return jnp.dot((z * z).astype(BF16), m, preferred_element_type=F32)

    cmod = cmod_ref[0]
    hc = _rms(ctx_ref[0]) * g1 * (1.0 + cmod[1:2]) + cmod[0:1]
    zc = jnp.dot(hc.astype(BF16), w_ref[:, ATTN_WIDTH:ATTN_WIDTH + 2 * KV_WIDTH],
                 preferred_element_type=F32)
    kc = zc[:, :KV_WIDTH]
    kc = kc * lax.rsqrt(head_ms(kc, bd_kv) + EPS) * kg
    vc = zc[:, KV_WIDTH:]
    for g in range(N_KV_HEADS):
        k_ref[0, g, 0:ctx_len, :] = kc[:, g * HEAD_DIM:(g + 1) * HEAD_DIM].astype(BF16)
        v_ref[0, g, 0:ctx_len, :] = _with_ones(vc[:, g * HEAD_DIM:(g + 1) * HEAD_DIM])

    zeros = jnp.zeros((CONV_PAD, CONV_WIDTH), F32)
    p_s[0:CONV_PAD, :] = zeros
    p_s[CONV_PAD + seq:CONV_PAD + seq + CONV_PAD, :] = zeros

    mod = mod_ref[0]
    sh1 = mod[0:1]
    sc1 = mod[1:2]
    lane = lax.broadcasted_iota(jnp.int32, (IN_ROWS, LANES), 1)
    is_lo = (lane // ROPE_FREQS) % 2 == 0

    def proj_chunk(c, carry):
        r0 = pl.multiple_of(c * IN_ROWS, IN_ROWS)
        h = (_rms(x_ref[0, pl.ds(r0, IN_ROWS), :]) * g1 * (1.0 + sc1) + sh1).astype(BF16)
        cos = cos_ref[pl.ds(r0, IN_ROWS), :]
        sin = sin_ref[pl.ds(r0, IN_ROWS), :]
        zq = jnp.dot(h, w_ref[:, 0:ATTN_WIDTH], preferred_element_type=F32)
        qn = zq * lax.rsqrt(head_ms(zq, bd) + EPS) * qg
        for j in range(ATTN_WIDTH // LANES):
            blk = _rope(qn[:, j * LANES:(j + 1) * LANES], cos, sin, is_lo) * Q_SCALE
            for hh in range(LANES // HEAD_DIM):
                head = j * (LANES // HEAD_DIM) + hh
                q_ref[0, head, pl.ds(r0, IN_ROWS), :] = (
                    blk[:, hh * HEAD_DIM:(hh + 1) * HEAD_DIM].astype(BF16))
        zkv = jnp.dot(h, w_ref[:, ATTN_WIDTH:ATTN_WIDTH + 2 * KV_WIDTH], preferred_element_type=F32)
        kx = zkv[:, :KV_WIDTH]
        kx = _rope(kx * lax.rsqrt(head_ms(kx, bd_kv) + EPS) * kg, cos, sin, is_lo)
        vx = zkv[:, KV_WIDTH:]
        for g in range(N_KV_HEADS):
            k_ref[0, g, pl.ds(ctx_len + r0, IN_ROWS), :] = kx[:, g * HEAD_DIM:(g + 1) * HEAD_DIM].astype(BF16)
            v_ref[0, g, pl.ds(ctx_len + r0, IN_ROWS), :] = _with_ones(vx[:, g * HEAD_DIM:(g + 1) * HEAD_DIM])
        c0 = ATTN_WIDTH + 2 * KV_WIDTH
        gb_s[pl.ds(r0, IN_ROWS), :] = jnp.dot(h, w_ref[:, c0:c0 + CONV_WIDTH], preferred_element_type=F32)
        zc_ = jnp.dot(h, w_ref[:, c0 + CONV_WIDTH:c0 + 2 * CONV_WIDTH], preferred_element_type=F32)
        zu = jnp.dot(h, w_ref[:, c0 + 2 * CONV_WIDTH:c0 + 3 * CONV_WIDTH], preferred_element_type=F32)
        p_s[pl.ds(CONV_PAD + r0, IN_ROWS), :] = zc_ * zu
        return carry

    lax.fori_loop(0, seq // IN_ROWS, proj_chunk, 0)

    cw = convw_ref[...]
    cog = cog_ref[...]

    def conv_chunk(c, carry):
        r0 = pl.multiple_of(c * IN_ROWS, IN_ROWS)
        win = p_s[pl.ds(r0, IN_ROWS + 2 * CONV_PAD), :]
        n_win = IN_ROWS + 2 * CONV_PAD
        prev = pltpu.roll(win, 1, 0)[CONV_PAD:CONV_PAD + IN_ROWS]
        cur = win[CONV_PAD:CONV_PAD + IN_ROWS]
        nxt = pltpu.roll(win, n_win - 1, 0)[CONV_PAD:CONV_PAD + IN_ROWS]
        y = cw[0:1] * prev + cw[1:2] * cur + cw[2:3] * nxt
        cvv = gb_s[pl.ds(r0, IN_ROWS), :] * y
        cvn = cvv * lax.rsqrt(head_ms(cvv, bd) + EPS) * cog
        cv_ref[0, pl.ds(r0, IN_ROWS), :] = cvn.astype(BF16)
        return carry

    lax.fori_loop(0, seq // IN_ROWS, conv_chunk, 0)


def _in_projection(x, ctx, mod3, norm1_g, w_in_bf, qg_t, kg_t, cos_t, sin_t, conv_w, conv_out_g, bd):
    b, s, d = x.shape
    ctx_len = ctx.shape[1]
    n_keys = ctx_len + s
    const = lambda *shape: pl.BlockSpec(shape, lambda i: (0,) * len(shape))
    return pl.pallas_call(
        _inproj_kernel,
        grid=(b,),
        in_specs=[
            pl.BlockSpec((1, s, d), lambda i: (i, 0, 0)),
            pl.BlockSpec((1, ctx_len, d), lambda i: (i, 0, 0)),
            pl.BlockSpec((1, N_MOD, d), lambda i: (i, 0, 0)),
            pl.BlockSpec((1, N_MOD, d), lambda i: (b, 0, 0)),
            const(1, d),
            const(d, IN_COLS),
            const(1, ATTN_WIDTH),
            const(1, KV_WIDTH),
            const(s, LANES),
            const(s, LANES),
            const(3, CONV_WIDTH),
            const(1, CONV_WIDTH),
            const(ATTN_WIDTH, ATTN_WIDTH),
        ],
        out_specs=[
            pl.BlockSpec((1, N_HEADS, s, HEAD_DIM), lambda i: (i, 0, 0, 0)),
            pl.BlockSpec((1, N_KV_HEADS, n_keys, HEAD_DIM), lambda i: (i, 0, 0, 0)),
            pl.BlockSpec((1, N_KV_HEADS, n_keys, V_LANES), lambda i: (i, 0, 0, 0)),
            pl.BlockSpec((1, s, CONV_WIDTH), lambda i: (i, 0, 0)),
        ],
        out_shape=[
            jax.ShapeDtypeStruct((b, N_HEADS, s, HEAD_DIM), BF16),
            jax.ShapeDtypeStruct((b, N_KV_HEADS, n_keys, HEAD_DIM), BF16),
            jax.ShapeDtypeStruct((b, N_KV_HEADS, n_keys, V_LANES), BF16),
            jax.ShapeDtypeStruct((b, s, CONV_WIDTH), BF16),
        ],
        scratch_shapes=[pltpu.VMEM((s + 2 * CONV_PAD, CONV_WIDTH), F32),
                        pltpu.VMEM((s, CONV_WIDTH), F32)],
        compiler_params=_params(("arbitrary",)),
        name="in_projection",
    )(x, ctx, mod3, mod3, norm1_g, w_in_bf, qg_t, kg_t, cos_t, sin_t, conv_w, conv_out_g, bd)


def _attn_kernel(q_ref, k_ref, v_ref, g_ref, o_ref):
    tq = q_ref.shape[2]
    pieces = []
    hs = KV_REP // ATTN_HEAD_SPLIT
    subs = [(g, g * KV_REP + j * hs) for g in range(N_KV_HEADS) for j in range(ATTN_HEAD_SPLIT)]
    scores = []
    for g, h0 in subs:
        qs = jnp.concatenate([q_ref[0, h0 + r] for r in range(hs)], axis=0)
        scores.append(lax.dot_general(qs, k_ref[0, g], (((1,), (1,)), ((), ())),
                                      preferred_element_type=F32))
    probs = [jnp.exp2(s - jnp.max(s, axis=-1, keepdims=True)).astype(BF16) for s in scores]
    lane = lax.broadcasted_iota(jnp.int32, (hs * tq, LANES), 1)
    for (g, h0), p in zip(subs, probs):
        ov = jnp.dot(p, v_ref[0, g], preferred_element_type=F32)
        o = ov / pltpu.roll(ov, HEAD_DIM, 1)
        ms = jnp.sum(jnp.where(lane < HEAD_DIM, o * o, 0.0), axis=-1, keepdims=True) * (1.0 / HEAD_DIM)
        o = o * lax.rsqrt(ms + EPS)
        for r in range(hs):
            pieces.append(o[r * tq:(r + 1) * tq, 0:HEAD_DIM])
    o_ref[0] = (jnp.concatenate(pieces, axis=1) * g_ref[...]).astype(BF16)


def _attention(q, k, v, attn_out_g):
    b, _, s, _ = q.shape
    n_keys = k.shape[2]
    return pl.pallas_call(
        _attn_kernel,
        grid=(b, s // ATTN_TQ),
        in_specs=[
            pl.BlockSpec((1, N_HEADS, ATTN_TQ, HEAD_DIM), lambda i, j: (i, 0, j, 0)),
            pl.BlockSpec((1, N_KV_HEADS, n_keys, HEAD_DIM), lambda i, j: (i, 0, 0, 0)),
            pl.BlockSpec((1, N_KV_HEADS, n_keys, V_LANES), lambda i, j: (i, 0, 0, 0)),
            pl.BlockSpec((1, ATTN_WIDTH), lambda i, j: (0, 0)),
        ],
        out_specs=pl.BlockSpec((1, ATTN_TQ, ATTN_WIDTH), lambda i, j: (i, j, 0)),
        out_shape=jax.ShapeDtypeStruct((b, s, ATTN_WIDTH), BF16),
        compiler_params=_params(("arbitrary", "arbitrary")),
        name="attention",
    )(q, k, v, attn_out_g)


def _outproj_kernel(a_ref, cv_ref, w_ref, x_ref, mod_ref, n2g_ref, wr_ref, upper_ref,
                    xn_ref, h2_ref, ri_ref, rit_ref, cnt_ref, carry_s):
    i = pl.program_id(0)

    @pl.when(i == 0)
    def _():
        carry_s[...] = jnp.zeros_like(carry_s)

    mod = mod_ref[0]
    merged = (jnp.dot(a_ref[...], w_ref[0:ATTN_WIDTH, :], preferred_element_type=F32)
              + jnp.dot(cv_ref[...], w_ref[ATTN_WIDTH:, :], preferred_element_type=F32))
    xn = x_ref[...] + mod[2:3] * merged
    xn_ref[...] = xn
    h2 = _rms(xn) * n2g_ref[...] * (1.0 + mod[4:5]) + mod[3:4]
    _store_row_tiles(h2_ref, h2)

    logits = jnp.dot(h2.astype(BF16), wr_ref[...], preferred_element_type=F32)
    lt = jnp.transpose(logits)
    rows = logits.shape[0]
    row = lax.broadcasted_iota(jnp.int32, (EXPERTS_PER_GROUP, rows), 0).astype(F32)
    neg = jnp.float32(-jnp.inf)
    none = jnp.float32(EXPERTS_PER_GROUP)

    def first_at(mask):
        return jnp.min(jnp.where(mask, row, none), axis=0, keepdims=True)

    gvalid = row < N_GROUPS
    lg = jnp.where(gvalid, lt[N_EXPERTS:N_EXPERTS + EXPERTS_PER_GROUP], neg)
    ge = jnp.exp(lg - jnp.max(lg, axis=0, keepdims=True))
    g_prob = ge / jnp.sum(ge, axis=0, keepdims=True)
    g_w = jnp.max(g_prob, axis=0, keepdims=True)
    g_sel = first_at(gvalid & (g_prob == g_w))

    le = lt[(N_GROUPS - 1) * EXPERTS_PER_GROUP:N_GROUPS * EXPERTS_PER_GROUP]
    for g in range(N_GROUPS - 2, -1, -1):
        le = jnp.where(g_sel == g, lt[g * EXPERTS_PER_GROUP:(g + 1) * EXPERTS_PER_GROUP], le)
    ee = jnp.exp(le - jnp.max(le, axis=0, keepdims=True))
    e_prob = ee / jnp.sum(ee, axis=0, keepdims=True)
    p1 = jnp.max(e_prob, axis=0, keepdims=True)
    i1 = first_at(e_prob == p1)
    rest = row != i1
    p2 = jnp.max(jnp.where(rest, e_prob, -1.0), axis=0, keepdims=True)
    i2 = first_at(rest & (e_prob == p2))
    psum = p1 + p2
    w1 = g_w * (p1 / psum)
    w2 = g_w * (p2 / psum)

    def expert_rows(i_sel):
        return jnp.concatenate([jnp.where((g_sel == g) & (row == i_sel), 1.0, 0.0)
                                for g in range(N_GROUPS)], axis=0)

    hit1 = expert_rows(i1)
    hit2 = expert_rows(i2)
    onehot = hit1 + hit2
    before = jnp.dot(onehot, upper_ref[...], preferred_element_type=F32) + carry_s[:, 0:1]
    r1 = jnp.sum(hit1 * before, axis=0, keepdims=True)
    r2 = jnp.sum(hit2 * before, axis=0, keepdims=True)
    carry_s[...] = carry_s[...] + jnp.sum(onehot, axis=1, keepdims=True)

    base = g_sel * EXPERTS_PER_GROUP
    info_t = jnp.where(row == 0, base + i1,
             jnp.where(row == 1, base + i2,
             jnp.where(row == 2, r1,
             jnp.where(row == 3, r2,
             jnp.where(row == 4, w1,
             jnp.where(row == 5, w2, 0.0))))))
    rit_ref[...] = info_t
    ri_ref[...] = jnp.transpose(
        jnp.concatenate([info_t, jnp.zeros((LANES - ROW_TILE, rows), F32)], axis=0))
    cnt_ref[...] = carry_s[...]


def _out_projection(a, cv, w_out_bf, x2, mod3, norm2_g, wr_bf, tri, seq):
    n, d = x2.shape
    tiles_per_batch = seq // OUT_ROWS
    return pl.pallas_call(
        _outproj_kernel,
        grid=(n // OUT_ROWS,),
        in_specs=[
            pl.BlockSpec((OUT_ROWS, ATTN_WIDTH), lambda i: (i, 0)),
            pl.BlockSpec((OUT_ROWS, CONV_WIDTH), lambda i: (i, 0)),
            pl.BlockSpec((d, d), lambda i: (0, 0)),
            pl.BlockSpec((OUT_ROWS, d), lambda i: (i, 0)),
            pl.BlockSpec((1, N_MOD, d), lambda i: (i // tiles_per_batch, 0, 0)),
            pl.BlockSpec((1, d), lambda i: (0, 0)),
            pl.BlockSpec((d, LANES), lambda i: (0, 0)),
            pl.BlockSpec((OUT_ROWS, OUT_ROWS), lambda i: (0, 0)),
        ],
        out_specs=[
            pl.BlockSpec((OUT_ROWS, d), lambda i: (i, 0)),
            pl.BlockSpec((OUT_ROWS * PACK_ROWS, LANES), lambda i: (i, 0)),
            pl.BlockSpec((OUT_ROWS, LANES), lambda i: (i, 0)),
            pl.BlockSpec((ROW_TILE, OUT_ROWS), lambda i: (0, i)),
            pl.BlockSpec((N_EXPERTS, LANES), lambda i: (0, 0)),
        ],
        out_shape=[
            jax.ShapeDtypeStruct((n, d), F32),
            jax.ShapeDtypeStruct((n * PACK_ROWS, LANES), jnp.int32),
            jax.ShapeDtypeStruct((n, LANES), F32),
            jax.ShapeDtypeStruct((ROW_TILE, n), F32),
            jax.ShapeDtypeStruct((N_EXPERTS, LANES), F32),
        ],
        scratch_shapes=[pltpu.VMEM((N_EXPERTS, LANES), F32)],
        compiler_params=_params(("arbitrary",)),
        name="out_projection_routing",
    )(a, cv, w_out_bf, x2, mod3, norm2_g, wr_bf, tri)


def _dispatch_kernel(pend_ref, padded_ref, d0_ref, d1_ref, h_ref, xs_hbm, zero_s, sem):
    tile = MOE_ROWS * PACK_ROWS

    @pl.when(pl.program_id(0) == 0)
    def _():
        zero_s[...] = jnp.zeros_like(zero_s)

        def last_tile(e):
            start = pl.multiple_of((pend_ref[e] - MOE_ROWS) * PACK_ROWS, tile)
            return pltpu.make_async_copy(zero_s, xs_hbm.at[pl.ds(start, tile)], sem)

        def spare_tile(j):
            return pltpu.make_async_copy(zero_s, xs_hbm.at[pl.ds(j * tile, tile)], sem)

        n_tiles = xs_hbm.shape[0] // tile
        used = pend_ref[N_EXPERTS - 1] // MOE_ROWS
        for e in range(N_EXPERTS):
            @pl.when(padded_ref[e] > 0)
            def _():
                last_tile(e).start()
        for j in range(n_tiles - N_EXPERTS, n_tiles):
            @pl.when(j >= used)
            def _():
                spare_tile(j).start()
        for e in range(N_EXPERTS):
            @pl.when(padded_ref[e] > 0)
            def _():
                last_tile(e).wait()
        for j in range(n_tiles - N_EXPERTS, n_tiles):
            @pl.when(j >= used)
            def _():
                spare_tile(j).wait()

    def start(j, carry):
        for u in range(DMA_UNROLL):
            t = j * DMA_UNROLL + u
            for k, d_ref in enumerate((d0_ref, d1_ref)):
                pltpu.make_async_copy(_row_tile(h_ref, t), _row_tile(xs_hbm, d_ref[0, 0, t]),
                                      sem).start(priority=k)
        return carry

    lax.fori_loop(0, DISPATCH_ROWS // DMA_UNROLL, start, 0)

    def drain(j, carry):
        for _ in range(DMA_UNROLL * TOP_K):
            pltpu.make_async_copy(_row_tile(h_ref, 0), _row_tile(xs_hbm, 0), sem).wait()
        return carry

    lax.fori_loop(0, DISPATCH_ROWS // DMA_UNROLL, drain, 0)


def _dispatch(pends, padded, dest0, dest1, h2t, p_rows):
    n = h2t.shape[0] // PACK_ROWS
    steps = n // DISPATCH_ROWS
    smem_rows = lambda: pl.BlockSpec((1, 1, DISPATCH_ROWS), lambda i, pe, pa: (i, 0, 0),
                                     memory_space=pltpu.SMEM)
    return pl.pallas_call(
        _dispatch_kernel,
        grid_spec=pltpu.PrefetchScalarGridSpec(
            num_scalar_prefetch=2,
            grid=(steps,),
            in_specs=[
                smem_rows(),
                smem_rows(),
                pl.BlockSpec((DISPATCH_ROWS * PACK_ROWS, LANES), lambda i, pe, pa: (i, 0)),
            ],
            out_specs=pl.BlockSpec(memory_space=pl.ANY),
            scratch_shapes=[pltpu.VMEM((MOE_ROWS * PACK_ROWS, LANES), jnp.int32),
                            pltpu.SemaphoreType.DMA(())],
        ),
        out_shape=jax.ShapeDtypeStruct((p_rows * PACK_ROWS, LANES), jnp.int32),
        compiler_params=_params(("arbitrary",)),
        name="moe_dispatch",
    )(pends, padded, dest0.reshape(steps, 1, DISPATCH_ROWS), dest1.reshape(steps, 1, DISPATCH_ROWS), h2t)


def _experts_kernel(first_ref, count_ref, xs_hbm, wg_ref, wu_ref, wd_ref, y_hbm,
                    x_s, y_s, wg_s, wu_s, wd_s, in_sem, out_sem):
    e = pl.program_id(0)
    last = pl.num_programs(0) - 1
    n = count_ref[e]
    tile = MOE_ROWS * PACK_ROWS
    ahead = EXPERT_IN_SLOTS - 1

    def rows(first, j):
        return pl.ds(pl.multiple_of((first + j) * tile, tile), tile)

    def fetch(first, j):
        slot = j % EXPERT_IN_SLOTS
        return pltpu.make_async_copy(xs_hbm.at[rows(first, j)], x_s.at[slot], in_sem.at[slot])

    def writeback(first, j):
        slot = j % 2
        return pltpu.make_async_copy(y_s.at[slot], y_hbm.at[rows(first, j)], out_sem.at[slot])

    def start_head(ex):
        for j in range(ahead):
            @pl.when(j < count_ref[ex])
            def _():
                fetch(first_ref[ex], j).start(priority=1)

    @pl.when(n > 0)
    def _():
        first = first_ref[e]

        @pl.when(jnp.logical_or(e == 0, count_ref[jnp.maximum(e - 1, 0)] == 0))
        def _():
            start_head(e)

        wg_s[...] = wg_ref[0].astype(BF16)
        wu_s[...] = wu_ref[0].astype(BF16)
        wd_s[...] = wd_ref[0].astype(BF16)

        def tile_step(j, carry):
            @pl.when(j + ahead < n)
            def _():
                fetch(first, j + ahead).start(priority=1)

            fetch(first, j).wait()

            @pl.when(j >= 2)
            def _():
                writeback(first, j - 2).wait()

            x = _load_row_tiles(x_s, MOE_ROWS, j % EXPERT_IN_SLOTS).astype(BF16)
            g = jnp.dot(x, wg_s[...], preferred_element_type=F32)
            u = jnp.dot(x, wu_s[...], preferred_element_type=F32)
            h = ((g * jax.nn.sigmoid(g)) * u).astype(BF16)
            _store_row_tiles(y_s.at[j % 2], jnp.dot(h, wd_s[...], preferred_element_type=F32))
            writeback(first, j).start(priority=1)
            return carry

        lax.fori_loop(0, n, tile_step, 0)

        @pl.when(e < last)
        def _():
            start_head(jnp.minimum(e + 1, last))

        @pl.when(n >= 2)
        def _():
            writeback(first, n - 2).wait()

        writeback(first, n - 1).wait()


def _experts(first_tile, tile_count, xs, w_gate, w_up, w_down):
    n_exp, d, d_exp = w_gate.shape
    tile = MOE_ROWS * PACK_ROWS
    w_map = lambda i, ft, tc: (i, 0, 0)
    return pl.pallas_call(
        _experts_kernel,
        grid_spec=pltpu.PrefetchScalarGridSpec(
            num_scalar_prefetch=2,
            grid=(n_exp,),
            in_specs=[
                pl.BlockSpec(memory_space=pl.ANY),
                pl.BlockSpec((1, d, d_exp), w_map),
                pl.BlockSpec((1, d, d_exp), w_map),
                pl.BlockSpec((1, d_exp, d), w_map),
            ],
            out_specs=pl.BlockSpec(memory_space=pl.ANY),
            scratch_shapes=[
                pltpu.VMEM((EXPERT_IN_SLOTS, tile, LANES), jnp.int32),
                pltpu.VMEM((2, tile, LANES), jnp.int32),
                pltpu.VMEM((d, d_exp), BF16),
                pltpu.VMEM((d, d_exp), BF16),
                pltpu.VMEM((d_exp, d), BF16),
                pltpu.SemaphoreType.DMA((EXPERT_IN_SLOTS,)),
                pltpu.SemaphoreType.DMA((2,)),
            ],
        ),
        out_shape=jax.ShapeDtypeStruct(xs.shape, xs.dtype),
        input_output_aliases={2: 0},
        compiler_params=_params(("arbitrary",)),
        name="moe_experts",
    )(first_tile, tile_count, xs, w_gate, w_up, w_down)


def _combine_kernel(d0_ref, d1_ref, d0n_ref, d1n_ref, ri_ref, xn_ref, mod_ref, fg_ref, y_hbm,
                    o_ref, buf, sem):
    i = pl.program_id(0)
    slot = i % 2

    def gather(refs, to_slot):
        def start(j, carry):
            for u in range(DMA_UNROLL):
                t = j * DMA_UNROLL + u
                for k, d_ref in enumerate(refs):
                    pltpu.make_async_copy(_row_tile(y_hbm, d_ref[0, 0, t]), _row_tile(buf, t, to_slot, k),
                                          sem.at[to_slot]).start(priority=k)
            return carry

        lax.fori_loop(0, COMBINE_ROWS // DMA_UNROLL, start, 0)

    @pl.when(i == 0)
    def _():
        gather((d0_ref, d1_ref), 0)

    @pl.when(i + 1 < pl.num_programs(0))
    def _():
        gather((d0n_ref, d1n_ref), 1 - slot)

    def drain(j, carry):
        for _ in range(DMA_UNROLL * TOP_K):
            pltpu.make_async_copy(_row_tile(y_hbm, 0), _row_tile(buf, 0, slot, 0), sem.at[slot]).wait()
        return carry

    lax.fori_loop(0, COMBINE_ROWS // DMA_UNROLL, drain, 0)

    ri = ri_ref[...]
    y = (_load_row_tiles(buf, COMBINE_ROWS, slot, 0) * ri[:, 4:5]
         + _load_row_tiles(buf, COMBINE_ROWS, slot, 1) * ri[:, 5:6])
    xf = xn_ref[...] + mod_ref[0][5:6] * y
    o_ref[...] = _rms(xf) * fg_ref[...]


def _combine(dest0, dest1, rinfo, x_new, mod3, final_g, ybuf, seq):
    n, d = x_new.shape
    steps = n // COMBINE_ROWS
    tiles_per_batch = seq // COMBINE_ROWS
    cur = lambda: pl.BlockSpec((1, 1, COMBINE_ROWS), lambda i: (i, 0, 0), memory_space=pltpu.SMEM)
    nxt = lambda: pl.BlockSpec((1, 1, COMBINE_ROWS), lambda i: (jnp.minimum(i + 1, steps - 1), 0, 0),
                               memory_space=pltpu.SMEM)
    d0 = dest0.reshape(steps, 1, COMBINE_ROWS)
    d1 = dest1.reshape(steps, 1, COMBINE_ROWS)
    return pl.pallas_call(
        _combine_kernel,
        grid=(steps,),
        in_specs=[
            cur(), cur(), nxt(), nxt(),
            pl.BlockSpec((COMBINE_ROWS, LANES), lambda i: (i, 0)),
            pl.BlockSpec((COMBINE_ROWS, d), lambda i: (i, 0)),
            pl.BlockSpec((1, N_MOD, d), lambda i: (i // tiles_per_batch, 0, 0)),
            pl.BlockSpec((1, d), lambda i: (0, 0)),
            pl.BlockSpec(memory_space=pl.ANY),
        ],
        out_specs=pl.BlockSpec((COMBINE_ROWS, d), lambda i: (i, 0)),
        out_shape=jax.ShapeDtypeStruct((n, d), F32),
        scratch_shapes=[pltpu.VMEM((2, TOP_K, COMBINE_ROWS * PACK_ROWS, LANES), jnp.int32),
                        pltpu.SemaphoreType.DMA((2,))],
        compiler_params=_params(("arbitrary",)),
        name="moe_combine",
    )(d0, d1, d0, d1, rinfo, x_new, mod3, final_g, ybuf)


def _rope_tables(seq):
    rows = seq // GRID_W
    row_idx = jnp.repeat(jnp.arange(rows, dtype=F32), GRID_W)
    col_idx = jnp.tile(jnp.arange(GRID_W, dtype=F32), rows)
    inv_freq = ROPE_THETA ** (-jnp.arange(0, ROPE_AXIS_DIM, 2, dtype=F32) / ROPE_AXIS_DIM)
    ang = jnp.stack([row_idx[:, None] * inv_freq, col_idx[:, None] * inv_freq], axis=1)
    cos = jnp.cos(ang)
    sin = jnp.sin(ang)
    cos_h = jnp.stack([cos, cos], axis=2).reshape(seq, HEAD_DIM)
    sin_h = jnp.stack([-sin, sin], axis=2).reshape(seq, HEAD_DIM)
    reps = LANES // HEAD_DIM
    return jnp.tile(cos_h, (1, reps)), jnp.tile(sin_h, (1, reps))


def kernel(x, c, ctx, c_ctx, w_mod, b_mod, norm1_g, w_in, q_norm_g, k_norm_g, conv_w, attn_out_g,
           conv_out_g, w_out, norm2_g, w_group, w_router, w_gate, w_up, w_down, final_g):
    assert w_mod.shape[0] == 1, "single-layer block"
    b, s, d = x.shape
    n = b * s
    assert b + 1 <= MOD_ROWS

    cond = jnp.zeros((MOD_ROWS, d), F32).at[:b].set(c).at[b].set(c_ctx)
    mod3 = _modulation(cond, w_mod[0], b_mod[0]).reshape(MOD_ROWS, N_MOD, d)

    cos_t, sin_t = _rope_tables(s)
    head_of = jnp.arange(ATTN_WIDTH) // HEAD_DIM
    bd = jnp.where(head_of[:, None] == head_of[None, :], 1.0 / HEAD_DIM, 0.0).astype(BF16)
    q, k, v, cv = _in_projection(
        x, ctx, mod3, norm1_g, w_in[0].astype(BF16),
        jnp.tile(q_norm_g[0], N_HEADS)[None], jnp.tile(k_norm_g[0], N_KV_HEADS)[None],
        cos_t, sin_t, conv_w[0], conv_out_g, bd)

    a = _attention(q, k, v, attn_out_g)

    wr = jnp.zeros((d, LANES), F32).at[:, :N_EXPERTS].set(w_router[0])
    wr = wr.at[:, N_EXPERTS:N_EXPERTS + N_GROUPS].set(w_group[0]).astype(BF16)
    ti = jnp.arange(OUT_ROWS)
    tri = (ti[:, None] < ti[None, :]).astype(F32)
    x_new, h2t, rinfo, rinfo_t, counts = _out_projection(
        a.reshape(n, ATTN_WIDTH), cv.reshape(n, CONV_WIDTH), w_out[0].astype(BF16),
        x.reshape(n, d), mod3, norm2_g, wr, tri, s)

    cnt = counts[:, 0].astype(jnp.int32)
    padded = ((cnt + MOE_ROWS - 1) // MOE_ROWS) * MOE_ROWS
    pends = jnp.cumsum(padded)
    pstarts = pends - padded
    experts = jnp.arange(N_EXPERTS, dtype=jnp.int32)[:, None]

    def slots(k):
        eid = rinfo_t[k].astype(jnp.int32)
        rank = rinfo_t[TOP_K + k].astype(jnp.int32)
        return jnp.sum(jnp.where(eid[None, :] == experts, pstarts[:, None], 0), axis=0) + rank

    dest0, dest1 = slots(0), slots(1)
    p_rows = n * TOP_K + N_EXPERTS * MOE_ROWS

    xs = _dispatch(pends, padded, dest0, dest1, h2t, p_rows)
    ybuf = _experts(pstarts // MOE_ROWS, padded // MOE_ROWS, xs, w_gate[0], w_up[0], w_down[0])
    out = _combine(dest0, dest1, rinfo, x_new, mod3, final_g.reshape(1, d), ybuf, s)
    return out.reshape(b, s, d)
```

```python
import functools

import jax
import jax.numpy as jnp
from jax import lax
from jax.experimental import pallas as pl
from jax.experimental.pallas import tpu as pltpu

F32 = jnp.float32
BF16 = jnp.bfloat16

D_MODEL = 1024
GRID_W = 64
ATTN_WIDTH = 512
N_HEADS = 8
N_KV_HEADS = 2
HEAD_DIM = 64
KV_REP = N_HEADS // N_KV_HEADS
KV_WIDTH = N_KV_HEADS * HEAD_DIM
CONV_WIDTH = 512
IN_COLS = ATTN_WIDTH + 2 * KV_WIDTH + 3 * CONV_WIDTH
ROPE_THETA = 10000.0
ROPE_AXIS_DIM = HEAD_DIM // 2
ROPE_FREQS = ROPE_AXIS_DIM // 2
N_GROUPS = 4
EXPERTS_PER_GROUP = 8
N_EXPERTS = N_GROUPS * EXPERTS_PER_GROUP
TOP_K = 2
D_EXPERT = 768
N_MOD = 6
EPS = 1e-6
LOG2_E = 1.4426950408889634
Q_SCALE = HEAD_DIM ** -0.5 * LOG2_E
V_LANES = 2 * HEAD_DIM

LANES = 128
ROW_TILE = 8
PACK_ROWS = 4
HIGH_HALF = -65536
MOD_ROWS = 16
IN_ROWS = 512
CONV_PAD = 8
ATTN_TQ = 512
ATTN_SUB_ROWS = 256
OUT_ROWS = 1024
DISPATCH_ROWS = 2048
DMA_UNROLL = 8
MOE_ROWS = 256
EXPERT_IN_SLOTS = 3
COMBINE_ROWS = 512
VMEM_LIMIT = 56 * 1024 * 1024


def _params(semantics, vmem=None):
    return pltpu.CompilerParams(dimension_semantics=semantics,
                                vmem_limit_bytes=vmem if vmem else VMEM_LIMIT)


def _rms(x):
    return x * lax.rsqrt(jnp.mean(x * x, axis=-1, keepdims=True) + EPS)


def _store_row_tiles(ref, val):
    rows, width = val.shape
    assert width == 2 * PACK_ROWS * LANES
    half = width // 2

    def bits(v):
        return lax.bitcast_convert_type(v.astype(BF16).astype(F32), jnp.int32)

    words = lax.shift_right_logical(bits(val[:, :half]), 16) | (bits(val[:, half:]) & HIGH_HALF)
    for i in range(PACK_ROWS):
        ref[pl.ds(i, rows, stride=PACK_ROWS), :] = words[:, i * LANES:(i + 1) * LANES]


def _load_row_tiles(ref, rows, *lead):
    words = [ref[(*lead, pl.ds(i, rows, stride=PACK_ROWS), slice(None))] for i in range(PACK_ROWS)]
    low = [lax.bitcast_convert_type(lax.shift_left(w, 16), F32) for w in words]
    high = [lax.bitcast_convert_type(w & HIGH_HALF, F32) for w in words]
    return jnp.concatenate(low + high, axis=1)


def _row_tile(ref, row, *lead):
    return ref.at[(*lead, pl.ds(pl.multiple_of(row * PACK_ROWS, PACK_ROWS), PACK_ROWS))]


def _mod_kernel(c_ref, w_ref, b_ref, o_ref):
    c = c_ref[...]
    s = c * jax.nn.sigmoid(c)
    o_ref[...] = jnp.dot(s.astype(BF16), w_ref[...].astype(BF16),
                         preferred_element_type=F32) + b_ref[...]


def _modulation(cond, w_mod, b_mod):
    d = cond.shape[1]
    n_out = w_mod.shape[1]
    tn = 1024
    return pl.pallas_call(
        _mod_kernel,
        grid=(n_out // tn,),
        in_specs=[pl.BlockSpec((MOD_ROWS, d), lambda j: (0, 0)),
                  pl.BlockSpec((d, tn), lambda j: (0, j)),
                  pl.BlockSpec((1, tn), lambda j: (0, j))],
        out_specs=pl.BlockSpec((MOD_ROWS, tn), lambda j: (0, j)),
        out_shape=jax.ShapeDtypeStruct((MOD_ROWS, n_out), F32),
        compiler_params=_params(("arbitrary",)),
        name="modulation",
    )(cond, w_mod, b_mod.reshape(1, n_out))


def _with_ones(v):
    return jnp.concatenate([v, jnp.ones_like(v)], axis=1).astype(BF16)


def _rope(x, cos, sin_signed, is_lo):
    partner = jnp.where(is_lo, pltpu.roll(x, LANES - ROPE_FREQS, 1), pltpu.roll(x, ROPE_FREQS, 1))
    return x * cos + partner * sin_signed


def _inproj_kernel(x_ref, ctx_ref, mod_ref, cmod_ref, n1g_ref, w_ref, qg_ref, kg_ref,
                   cos_ref, sin_ref, convw_ref, cog_ref, bd_ref,
                   q_ref, k_ref, v_ref, cv_ref, p_s, gb_s):
    seq = x_ref.shape[1]
    ctx_len = ctx_ref.shape[1]
    g1 = n1g_ref[...]
    bd = bd_ref[...]
    bd_kv = bd_ref[0:KV_WIDTH, 0:KV_WIDTH]
    kg = kg_ref[...]
    qg = qg_ref[...]

    def head_ms(z, m):
        return jnp.dot((z * z).astype(BF16), m, preferred_element_type=F32)

    cmod = cmod_ref[0]
    hc = _rms(ctx_ref[0]) * g1 * (1.0 + cmod[1:2]) + cmod[0:1]
    zc = jnp.dot(hc.astype(BF16), w_ref[:, ATTN_WIDTH:ATTN_WIDTH + 2 * KV_WIDTH],
                 preferred_element_type=F32)
    kc = zc[:, :KV_WIDTH]
    kc = kc * lax.rsqrt(head_ms(kc, bd_kv) + EPS) * kg
    vc = zc[:, KV_WIDTH:]
    for g in range(N_KV_HEADS):
        k_ref[0, g, 0:ctx_len, :] = kc[:, g * HEAD_DIM:(g + 1) * HEAD_DIM].astype(BF16)
        v_ref[0, g, 0:ctx_len, :] = _with_ones(vc[:, g * HEAD_DIM:(g + 1) * HEAD_DIM])

    zeros = jnp.zeros((CONV_PAD, CONV_WIDTH), F32)
    p_s[0:CONV_PAD, :] = zeros
    p_s[CONV_PAD + seq:CONV_PAD + seq + CONV_PAD, :] = zeros

    mod = mod_ref[0]
    sh1 = mod[0:1]
    sc1 = mod[1:2]
    lane = lax.broadcasted_iota(jnp.int32, (IN_ROWS, LANES), 1)
    is_lo = (lane // ROPE_FREQS) % 2 == 0

    def proj_chunk(c, carry):
        r0 = pl.multiple_of(c * IN_ROWS, IN_ROWS)
        h = (_rms(x_ref[0, pl.ds(r0, IN_ROWS), :]) * g1 * (1.0 + sc1) + sh1).astype(BF16)
        cos = cos_ref[pl.ds(r0, IN_ROWS), :]
        sin = sin_ref[pl.ds(r0, IN_ROWS), :]
        zq = jnp.dot(h, w_ref[:, 0:ATTN_WIDTH], preferred_element_type=F32)
        qn = zq * lax.rsqrt(head_ms(zq, bd) + EPS) * qg
        for j in range(ATTN_WIDTH // LANES):
            blk = _rope(qn[:, j * LANES:(j + 1) * LANES], cos, sin, is_lo) * Q_SCALE
            for hh in range(LANES // HEAD_DIM):
                head = j * (LANES // HEAD_DIM) + hh
                q_ref[0, head, pl.ds(r0, IN_ROWS), :] = (
                    blk[:, hh * HEAD_DIM:(hh + 1) * HEAD_DIM].astype(BF16))
        zkv = jnp.dot(h, w_ref[:, ATTN_WIDTH:ATTN_WIDTH + 2 * KV_WIDTH], preferred_element_type=F32)
        kx = zkv[:, :KV_WIDTH]
        kx = _rope(kx * lax.rsqrt(head_ms(kx, bd_kv) + EPS) * kg, cos, sin, is_lo)
        vx = zkv[:, KV_WIDTH:]
        for g in range(N_KV_HEADS):
            k_ref[0, g, pl.ds(ctx_len + r0, IN_ROWS), :] = kx[:, g * HEAD_DIM:(g + 1) * HEAD_DIM].astype(BF16)
            v_ref[0, g, pl.ds(ctx_len + r0, IN_ROWS), :] = _with_ones(vx[:, g * HEAD_DIM:(g + 1) * HEAD_DIM])
        c0 = ATTN_WIDTH + 2 * KV_WIDTH
        gb_s[pl.ds(r0, IN_ROWS), :] = jnp.dot(h, w_ref[:, c0:c0 + CONV_WIDTH], preferred_element_type=F32)
        zc_ = jnp.dot(h, w_ref[:, c0 + CONV_WIDTH:c0 + 2 * CONV_WIDTH], preferred_element_type=F32)
        zu = jnp.dot(h, w_ref[:, c0 + 2 * CONV_WIDTH:c0 + 3 * CONV_WIDTH], preferred_element_type=F32)
        p_s[pl.ds(CONV_PAD + r0, IN_ROWS), :] = zc_ * zu
        return carry

    lax.fori_loop(0, seq // IN_ROWS, proj_chunk, 0)

    cw = convw_ref[...]
    cog = cog_ref[...]

    def conv_chunk(c, carry):
        r0 = pl.multiple_of(c * IN_ROWS, IN_ROWS)
        win = p_s[pl.ds(r0, IN_ROWS + 2 * CONV_PAD), :]
        n_win = IN_ROWS + 2 * CONV_PAD
        prev = pltpu.roll(win, 1, 0)[CONV_PAD:CONV_PAD + IN_ROWS]
        cur = win[CONV_PAD:CONV_PAD + IN_ROWS]
        nxt = pltpu.roll(win, n_win - 1, 0)[CONV_PAD:CONV_PAD + IN_ROWS]
        y = cw[0:1] * prev + cw[1:2] * cur + cw[2:3] * nxt
        cvv = gb_s[pl.ds(r0, IN_ROWS), :] * y
        cvn = cvv * lax.rsqrt(head_ms(cvv, bd) + EPS) * cog
        cv_ref[0, pl.ds(r0, IN_ROWS), :] = cvn.astype(BF16)
        return carry

    lax.fori_loop(0, seq // IN_ROWS, conv_chunk, 0)


def _in_projection(x, ctx, mod3, norm1_g, w_in_bf, qg_t, kg_t, cos_t, sin_t, conv_w, conv_out_g, bd):
    b, s, d = x.shape
    ctx_len = ctx.shape[1]
    n_keys = ctx_len + s
    const = lambda *shape: pl.BlockSpec(shape, lambda i: (0,) * len(shape))
    return pl.pallas_call(
        _inproj_kernel,
        grid=(b,),
        in_specs=[
            pl.BlockSpec((1, s, d), lambda i: (i, 0, 0)),
            pl.BlockSpec((1, ctx_len, d), lambda i: (i, 0, 0)),
            pl.BlockSpec((1, N_MOD, d), lambda i: (i, 0, 0)),
            pl.BlockSpec((1, N_MOD, d), lambda i: (b, 0, 0)),
            const(1, d),
            const(d, IN_COLS),
            const(1, ATTN_WIDTH),
            const(1, KV_WIDTH),
            const(s, LANES),
            const(s, LANES),
            const(3, CONV_WIDTH),
            const(1, CONV_WIDTH),
            const(ATTN_WIDTH, ATTN_WIDTH),
        ],
        out_specs=[
            pl.BlockSpec((1, N_HEADS, s, HEAD_DIM), lambda i: (i, 0, 0, 0)),
            pl.BlockSpec((1, N_KV_HEADS, n_keys, HEAD_DIM), lambda i: (i, 0, 0, 0)),
            pl.BlockSpec((1, N_KV_HEADS, n_keys, V_LANES), lambda i: (i, 0, 0, 0)),
            pl.BlockSpec((1, s, CONV_WIDTH), lambda i: (i, 0, 0)),
        ],
        out_shape=[
            jax.ShapeDtypeStruct((b, N_HEADS, s, HEAD_DIM), BF16),
            jax.ShapeDtypeStruct((b, N_KV_HEADS, n_keys, HEAD_DIM), BF16),
            jax.ShapeDtypeStruct((b, N_KV_HEADS, n_keys, V_LANES), BF16),
            jax.ShapeDtypeStruct((b, s, CONV_WIDTH), BF16),
        ],
        scratch_shapes=[pltpu.VMEM((s + 2 * CONV_PAD, CONV_WIDTH), F32),
                        pltpu.VMEM((s, CONV_WIDTH), F32)],
        compiler_params=_params(("arbitrary",)),
        name="in_projection",
    )(x, ctx, mod3, mod3, norm1_g, w_in_bf, qg_t, kg_t, cos_t, sin_t, conv_w, conv_out_g, bd)


def _attn_kernel(q_ref, k_ref, v_ref, g_ref, o_ref):
    tq = q_ref.shape[2]
    subs = [(r0, h) for r0 in range(0, tq, ATTN_SUB_ROWS) for h in range(N_HEADS)]
    scores = [lax.dot_general(q_ref[0, h, r0:r0 + ATTN_SUB_ROWS, :], k_ref[0, h // KV_REP],
                              (((1,), (1,)), ((), ())), preferred_element_type=F32)
              for r0, h in subs]
    probs = [jnp.exp2(s - jnp.max(s, axis=-1, keepdims=True)).astype(BF16) for s in scores]
    lane = lax.broadcasted_iota(jnp.int32, (ATTN_SUB_ROWS, LANES), 1)
    rows = []
    for i, ((r0, h), p) in enumerate(zip(subs, probs)):
        ov = jnp.dot(p, v_ref[0, h // KV_REP], preferred_element_type=F32)
        o = ov / pltpu.roll(ov, HEAD_DIM, 1)
        ms = jnp.sum(jnp.where(lane < HEAD_DIM, o * o, 0.0), axis=-1, keepdims=True) * (1.0 / HEAD_DIM)
        if h == 0:
            rows.append([])
        rows[-1].append((o * lax.rsqrt(ms + EPS))[:, 0:HEAD_DIM])
    out = jnp.concatenate([jnp.concatenate(r, axis=1) for r in rows], axis=0)
    o_ref[0] = (out * g_ref[...]).astype(BF16)


def _attention(q, k, v, attn_out_g):
    b, _, s, _ = q.shape
    n_keys = k.shape[2]
    return pl.pallas_call(
        _attn_kernel,
        grid=(b, s // ATTN_TQ),
        in_specs=[
            pl.BlockSpec((1, N_HEADS, ATTN_TQ, HEAD_DIM), lambda i, j: (i, 0, j, 0)),
            pl.BlockSpec((1, N_KV_HEADS, n_keys, HEAD_DIM), lambda i, j: (i, 0, 0, 0)),
            pl.BlockSpec((1, N_KV_HEADS, n_keys, V_LANES), lambda i, j: (i, 0, 0, 0)),
            pl.BlockSpec((1, ATTN_WIDTH), lambda i, j: (0, 0)),
        ],
        out_specs=pl.BlockSpec((1, ATTN_TQ, ATTN_WIDTH), lambda i, j: (i, j, 0)),
        out_shape=jax.ShapeDtypeStruct((b, s, ATTN_WIDTH), BF16),
        compiler_params=_params(("arbitrary", "arbitrary")),
        name="attention",
    )(q, k, v, attn_out_g)


def _outproj_kernel(a_ref, cv_ref, w_ref, x_ref, mod_ref, n2g_ref, wr_ref, upper_ref,
                    xn_ref, h2_ref, ri_ref, rit_ref, cnt_ref, carry_s):
    i = pl.program_id(0)

    @pl.when(i == 0)
    def _():
        carry_s[...] = jnp.zeros_like(carry_s)

    mod = mod_ref[0]
    merged = (jnp.dot(a_ref[...], w_ref[0:ATTN_WIDTH, :], preferred_element_type=F32)
              + jnp.dot(cv_ref[...], w_ref[ATTN_WIDTH:, :], preferred_element_type=F32))
    xn = x_ref[...] + mod[2:3] * merged
    xn_ref[...] = xn
    h2 = _rms(xn) * n2g_ref[...] * (1.0 + mod[4:5]) + mod[3:4]
    _store_row_tiles(h2_ref, h2)

    logits = jnp.dot(h2.astype(BF16), wr_ref[...], preferred_element_type=F32)
    lt = jnp.transpose(logits)
    rows = logits.shape[0]
    row = lax.broadcasted_iota(jnp.int32, (EXPERTS_PER_GROUP, rows), 0).astype(F32)
    neg = jnp.float32(-jnp.inf)
    none = jnp.float32(EXPERTS_PER_GROUP)

    def first_at(mask):
        return jnp.min(jnp.where(mask, row, none), axis=0, keepdims=True)

    gvalid = row < N_GROUPS
    lg = jnp.where(gvalid, lt[N_EXPERTS:N_EXPERTS + EXPERTS_PER_GROUP], neg)
    ge = jnp.exp(lg - jnp.max(lg, axis=0, keepdims=True))
    g_prob = ge / jnp.sum(ge, axis=0, keepdims=True)
    g_w = jnp.max(g_prob, axis=0, keepdims=True)
    g_sel = first_at(gvalid & (g_prob == g_w))

    le = lt[(N_GROUPS - 1) * EXPERTS_PER_GROUP:N_GROUPS * EXPERTS_PER_GROUP]
    for g in range(N_GROUPS - 2, -1, -1):
        le = jnp.where(g_sel == g, lt[g * EXPERTS_PER_GROUP:(g + 1) * EXPERTS_PER_GROUP], le)
    ee = jnp.exp(le - jnp.max(le, axis=0, keepdims=True))
    e_prob = ee / jnp.sum(ee, axis=0, keepdims=True)
    p1 = jnp.max(e_prob, axis=0, keepdims=True)
    i1 = first_at(e_prob == p1)
    rest = row != i1
    p2 = jnp.max(jnp.where(rest, e_prob, -1.0), axis=0, keepdims=True)
    i2 = first_at(rest & (e_prob == p2))
    psum = p1 + p2
    w1 = g_w * (p1 / psum)
    w2 = g_w * (p2 / psum)

    def expert_rows(i_sel):
        return jnp.concatenate([jnp.where((g_sel == g) & (row == i_sel), 1.0, 0.0)
                                for g in range(N_GROUPS)], axis=0)

    hit1 = expert_rows(i1)
    hit2 = expert_rows(i2)
    onehot = hit1 + hit2
    before = jnp.dot(onehot, upper_ref[...], preferred_element_type=F32) + carry_s[:, 0:1]
    r1 = jnp.sum(hit1 * before, axis=0, keepdims=True)
    r2 = jnp.sum(hit2 * before, axis=0, keepdims=True)
    carry_s[...] = carry_s[...] + jnp.sum(onehot, axis=1, keepdims=True)

    base = g_sel * EXPERTS_PER_GROUP
    info_t = jnp.where(row == 0, base + i1,
             jnp.where(row == 1, base + i2,
             jnp.where(row == 2, r1,
             jnp.where(row == 3, r2,
             jnp.where(row == 4, w1,
             jnp.where(row == 5, w2, 0.0))))))
    rit_ref[...] = info_t
    ri_ref[...] = jnp.transpose(
        jnp.concatenate([info_t, jnp.zeros((LANES - ROW_TILE, rows), F32)], axis=0))
    cnt_ref[...] = carry_s[...]


def _out_projection(a, cv, w_out_bf, x2, mod3, norm2_g, wr_bf, tri, seq):
    n, d = x2.shape
    tiles_per_batch = seq // OUT_ROWS
    return pl.pallas_call(
        _outproj_kernel,
        grid=(n // OUT_ROWS,),
        in_specs=[
            pl.BlockSpec((OUT_ROWS, ATTN_WIDTH), lambda i: (i, 0)),
            pl.BlockSpec((OUT_ROWS, CONV_WIDTH), lambda i: (i, 0)),
            pl.BlockSpec((d, d), lambda i: (0, 0)),
            pl.BlockSpec((OUT_ROWS, d), lambda i: (i, 0)),
            pl.BlockSpec((1, N_MOD, d), lambda i: (i // tiles_per_batch, 0, 0)),
            pl.BlockSpec((1, d), lambda i: (0, 0)),
            pl.BlockSpec((d, LANES), lambda i: (0, 0)),
            pl.BlockSpec((OUT_ROWS, OUT_ROWS), lambda i: (0, 0)),
        ],
        out_specs=[
            pl.BlockSpec((OUT_ROWS, d), lambda i: (i, 0)),
            pl.BlockSpec((OUT_ROWS * PACK_ROWS, LANES), lambda i: (i, 0)),
            pl.BlockSpec((OUT_ROWS, LANES), lambda i: (i, 0)),
            pl.BlockSpec((ROW_TILE, OUT_ROWS), lambda i: (0, i)),
            pl.BlockSpec((N_EXPERTS, LANES), lambda i: (0, 0)),
        ],
        out_shape=[
            jax.ShapeDtypeStruct((n, d), F32),
            jax.ShapeDtypeStruct((n * PACK_ROWS, LANES), jnp.int32),
            jax.ShapeDtypeStruct((n, LANES), F32),
            jax.ShapeDtypeStruct((ROW_TILE, n), F32),
            jax.ShapeDtypeStruct((N_EXPERTS, LANES), F32),
        ],
        scratch_shapes=[pltpu.VMEM((N_EXPERTS, LANES), F32)],
        compiler_params=_params(("arbitrary",)),
        name="out_projection_routing",
    )(a, cv, w_out_bf, x2, mod3, norm2_g, wr_bf, tri)


def _dispatch_kernel(pend_ref, padded_ref, d0_ref, d1_ref, h_ref, xs_hbm, zero_s, sem):
    tile = MOE_ROWS * PACK_ROWS

    @pl.when(pl.program_id(0) == 0)
    def _():
        zero_s[...] = jnp.zeros_like(zero_s)

        def last_tile(e):
            start = pl.multiple_of((pend_ref[e] - MOE_ROWS) * PACK_ROWS, tile)
            return pltpu.make_async_copy(zero_s, xs_hbm.at[pl.ds(start, tile)], sem)

        def spare_tile(j):
            return pltpu.make_async_copy(zero_s, xs_hbm.at[pl.ds(j * tile, tile)], sem)

        n_tiles = xs_hbm.shape[0] // tile
        used = pend_ref[N_EXPERTS - 1] // MOE_ROWS
        for e in range(N_EXPERTS):
            @pl.when(padded_ref[e] > 0)
            def _():
                last_tile(e).start()
        for j in range(n_tiles - N_EXPERTS, n_tiles):
            @pl.when(j >= used)
            def _():
                spare_tile(j).start()
        for e in range(N_EXPERTS):
            @pl.when(padded_ref[e] > 0)
            def _():
                last_tile(e).wait()
        for j in range(n_tiles - N_EXPERTS, n_tiles):
            @pl.when(j >= used)
            def _():
                spare_tile(j).wait()

    def start(j, carry):
        for u in range(DMA_UNROLL):
            t = j * DMA_UNROLL + u
            for k, d_ref in enumerate((d0_ref, d1_ref)):
                pltpu.make_async_copy(_row_tile(h_ref, t), _row_tile(xs_hbm, d_ref[0, 0, t]),
                                      sem).start(priority=k)
        return carry

    lax.fori_loop(0, DISPATCH_ROWS // DMA_UNROLL, start, 0)

    def drain(j, carry):
        for _ in range(DMA_UNROLL * TOP_K):
            pltpu.make_async_copy(_row_tile(h_ref, 0), _row_tile(xs_hbm, 0), sem).wait()
        return carry

    lax.fori_loop(0, DISPATCH_ROWS // DMA_UNROLL, drain, 0)


def _dispatch(pends, padded, dest0, dest1, h2t, p_rows):
    n = h2t.shape[0] // PACK_ROWS
    steps = n // DISPATCH_ROWS
    smem_rows = lambda: pl.BlockSpec((1, 1, DISPATCH_ROWS), lambda i, pe, pa: (i, 0, 0),
                                     memory_space=pltpu.SMEM)
    return pl.pallas_call(
        _dispatch_kernel,
        grid_spec=pltpu.PrefetchScalarGridSpec(
            num_scalar_prefetch=2,
            grid=(steps,),
            in_specs=[
                smem_rows(),
                smem_rows(),
                pl.BlockSpec((DISPATCH_ROWS * PACK_ROWS, LANES), lambda i, pe, pa: (i, 0)),
            ],
            out_specs=pl.BlockSpec(memory_space=pl.ANY),
            scratch_shapes=[pltpu.VMEM((MOE_ROWS * PACK_ROWS, LANES), jnp.int32),
                            pltpu.SemaphoreType.DMA(())],
        ),
        out_shape=jax.ShapeDtypeStruct((p_rows * PACK_ROWS, LANES), jnp.int32),
        compiler_params=_params(("arbitrary",)),
        name="moe_dispatch",
    )(pends, padded, dest0.reshape(steps, 1, DISPATCH_ROWS), dest1.reshape(steps, 1, DISPATCH_ROWS), h2t)


def _experts_kernel(first_ref, count_ref, xs_hbm, wg_ref, wu_ref, wd_ref, y_hbm,
                    x_s, y_s, wg_s, wu_s, wd_s, in_sem, out_sem):
    e = pl.program_id(0)
    last = pl.num_programs(0) - 1
    n = count_ref[e]
    tile = MOE_ROWS * PACK_ROWS
    ahead = EXPERT_IN_SLOTS - 1

    def rows(first, j):
        return pl.ds(pl.multiple_of((first + j) * tile, tile), tile)

    def fetch(first, j):
        slot = j % EXPERT_IN_SLOTS
        return pltpu.make_async_copy(xs_hbm.at[rows(first, j)], x_s.at[slot], in_sem.at[slot])

    def writeback(first, j):
        slot = j % 2
        return pltpu.make_async_copy(y_s.at[slot], y_hbm.at[rows(first, j)], out_sem.at[slot])

    def start_head(ex):
        for j in range(ahead):
            @pl.when(j < count_ref[ex])
            def _():
                fetch(first_ref[ex], j).start(priority=1)

    @pl.when(n > 0)
    def _():
        first = first_ref[e]

        @pl.when(jnp.logical_or(e == 0, count_ref[jnp.maximum(e - 1, 0)] == 0))
        def _():
            start_head(e)

        wg_s[...] = wg_ref[0].astype(BF16)
        wu_s[...] = wu_ref[0].astype(BF16)
        wd_s[...] = wd_ref[0].astype(BF16)

        def tile_step(j, carry):
            @pl.when(j + ahead < n)
            def _():
                fetch(first, j + ahead).start(priority=1)

            fetch(first, j).wait()

            @pl.when(j >= 2)
            def _():
                writeback(first, j - 2).wait()

            x = _load_row_tiles(x_s, MOE_ROWS, j % EXPERT_IN_SLOTS).astype(BF16)
            g = jnp.dot(x, wg_s[...], preferred_element_type=F32)
            u = jnp.dot(x, wu_s[...], preferred_element_type=F32)
            h = ((g * jax.nn.sigmoid(g)) * u).astype(BF16)
            _store_row_tiles(y_s.at[j % 2], jnp.dot(h, wd_s[...], preferred_element_type=F32))
            writeback(first, j).start(priority=1)
            return carry

        lax.fori_loop(0, n, tile_step, 0)

        @pl.when(e < last)
        def _():
            start_head(jnp.minimum(e + 1, last))

        @pl.when(n >= 2)
        def _():
            writeback(first, n - 2).wait()

        writeback(first, n - 1).wait()


def _experts(first_tile, tile_count, xs, w_gate, w_up, w_down):
    n_exp, d, d_exp = w_gate.shape
    tile = MOE_ROWS * PACK_ROWS
    w_map = lambda i, ft, tc: (i, 0, 0)
    return pl.pallas_call(
        _experts_kernel,
        grid_spec=pltpu.PrefetchScalarGridSpec(
            num_scalar_prefetch=2,
            grid=(n_exp,),
            in_specs=[
                pl.BlockSpec(memory_space=pl.ANY),
                pl.BlockSpec((1, d, d_exp), w_map),
                pl.BlockSpec((1, d, d_exp), w_map),
                pl.BlockSpec((1, d_exp, d), w_map),
            ],
            out_specs=pl.BlockSpec(memory_space=pl.ANY),
            scratch_shapes=[
                pltpu.VMEM((EXPERT_IN_SLOTS, tile, LANES), jnp.int32),
                pltpu.VMEM((2, tile, LANES), jnp.int32),
                pltpu.VMEM((d, d_exp), BF16),
                pltpu.VMEM((d, d_exp), BF16),
                pltpu.VMEM((d_exp, d), BF16),
                pltpu.SemaphoreType.DMA((EXPERT_IN_SLOTS,)),
                pltpu.SemaphoreType.DMA((2,)),
            ],
        ),
        out_shape=jax.ShapeDtypeStruct(xs.shape, xs.dtype),
        input_output_aliases={2: 0},
        compiler_params=_params(("arbitrary",)),
        name="moe_experts",
    )(first_tile, tile_count, xs, w_gate, w_up, w_down)


def _combine_kernel(d0_ref, d1_ref, d0n_ref, d1n_ref, ri_ref, xn_ref, mod_ref, fg_ref, y_hbm,
                    o_ref, buf, sem):
    i = pl.program_id(0)
    slot = i % 2

    def gather(refs, to_slot):
        def start(j, carry):
            for u in range(DMA_UNROLL):
                t = j * DMA_UNROLL + u
                for k, d_ref in enumerate(refs):
                    pltpu.make_async_copy(_row_tile(y_hbm, d_ref[0, 0, t]), _row_tile(buf, t, to_slot, k),
                                          sem.at[to_slot]).start(priority=k)
            return carry

        lax.fori_loop(0, COMBINE_ROWS // DMA_UNROLL, start, 0)

    @pl.when(i == 0)
    def _():
        gather((d0_ref, d1_ref), 0)

    @pl.when(i + 1 < pl.num_programs(0))
    def _():
        gather((d0n_ref, d1n_ref), 1 - slot)

    def drain(j, carry):
        for _ in range(DMA_UNROLL * TOP_K):
            pltpu.make_async_copy(_row_tile(y_hbm, 0), _row_tile(buf, 0, slot, 0), sem.at[slot]).wait()
        return carry

    lax.fori_loop(0, COMBINE_ROWS // DMA_UNROLL, drain, 0)

    ri = ri_ref[...]
    y = (_load_row_tiles(buf, COMBINE_ROWS, slot, 0) * ri[:, 4:5]
         + _load_row_tiles(buf, COMBINE_ROWS, slot, 1) * ri[:, 5:6])
    xf = xn_ref[...] + mod_ref[0][5:6] * y
    o_ref[...] = _rms(xf) * fg_ref[...]


def _combine(dest0, dest1, rinfo, x_new, mod3, final_g, ybuf, seq):
    n, d = x_new.shape
    steps = n // COMBINE_ROWS
    tiles_per_batch = seq // COMBINE_ROWS
    cur = lambda: pl.BlockSpec((1, 1, COMBINE_ROWS), lambda i: (i, 0, 0), memory_space=pltpu.SMEM)
    nxt = lambda: pl.BlockSpec((1, 1, COMBINE_ROWS), lambda i: (jnp.minimum(i + 1, steps - 1), 0, 0),
                               memory_space=pltpu.SMEM)
    d0 = dest0.reshape(steps, 1, COMBINE_ROWS)
    d1 = dest1.reshape(steps, 1, COMBINE_ROWS)
    return pl.pallas_call(
        _combine_kernel,
        grid=(steps,),
        in_specs=[
            cur(), cur(), nxt(), nxt(),
            pl.BlockSpec((COMBINE_ROWS, LANES), lambda i: (i, 0)),
            pl.BlockSpec((COMBINE_ROWS, d), lambda i: (i, 0)),
            pl.BlockSpec((1, N_MOD, d), lambda i: (i // tiles_per_batch, 0, 0)),
            pl.BlockSpec((1, d), lambda i: (0, 0)),
            pl.BlockSpec(memory_space=pl.ANY),
        ],
        out_specs=pl.BlockSpec((COMBINE_ROWS, d), lambda i: (i, 0)),
        out_shape=jax.ShapeDtypeStruct((n, d), F32),
        scratch_shapes=[pltpu.VMEM((2, TOP_K, COMBINE_ROWS * PACK_ROWS, LANES), jnp.int32),
                        pltpu.SemaphoreType.DMA((2,))],
        compiler_params=_params(("arbitrary",)),
        name="moe_combine",
    )(d0, d1, d0, d1, rinfo, x_new, mod3, final_g, ybuf)


def _rope_tables(seq):
    rows = seq // GRID_W
    row_idx = jnp.repeat(jnp.arange(rows, dtype=F32), GRID_W)
    col_idx = jnp.tile(jnp.arange(GRID_W, dtype=F32), rows)
    inv_freq = ROPE_THETA ** (-jnp.arange(0, ROPE_AXIS_DIM, 2, dtype=F32) / ROPE_AXIS_DIM)
    ang = jnp.stack([row_idx[:, None] * inv_freq, col_idx[:, None] * inv_freq], axis=1)
    cos = jnp.cos(ang)
    sin = jnp.sin(ang)
    cos_h = jnp.stack([cos, cos], axis=2).reshape(seq, HEAD_DIM)
    sin_h = jnp.stack([-sin, sin], axis=2).reshape(seq, HEAD_DIM)
    reps = LANES // HEAD_DIM
    return jnp.tile(cos_h, (1, reps)), jnp.tile(sin_h, (1, reps))


def kernel(x, c, ctx, c_ctx, w_mod, b_mod, norm1_g, w_in, q_norm_g, k_norm_g, conv_w, attn_out_g,
           conv_out_g, w_out, norm2_g, w_group, w_router, w_gate, w_up, w_down, final_g):
    assert w_mod.shape[0] == 1, "single-layer block"
    b, s, d = x.shape
    n = b * s
    assert b + 1 <= MOD_ROWS

    cond = jnp.zeros((MOD_ROWS, d), F32).at[:b].set(c).at[b].set(c_ctx)
    mod3 = _modulation(cond, w_mod[0], b_mod[0]).reshape(MOD_ROWS, N_MOD, d)

    cos_t, sin_t = _rope_tables(s)
    head_of = jnp.arange(ATTN_WIDTH) // HEAD_DIM
    bd = jnp.where(head_of[:, None] == head_of[None, :], 1.0 / HEAD_DIM, 0.0).astype(BF16)
    q, k, v, cv = _in_projection(
        x, ctx, mod3, norm1_g, w_in[0].astype(BF16),
        jnp.tile(q_norm_g[0], N_HEADS)[None], jnp.tile(k_norm_g[0], N_KV_HEADS)[None],
        cos_t, sin_t, conv_w[0], conv_out_g, bd)

    a = _attention(q, k, v, attn_out_g)

    wr = jnp.zeros((d, LANES), F32).at[:, :N_EXPERTS].set(w_router[0])
    wr = wr.at[:, N_EXPERTS:N_EXPERTS + N_GROUPS].set(w_group[0]).astype(BF16)
    ti = jnp.arange(OUT_ROWS)
    tri = (ti[:, None] < ti[None, :]).astype(F32)
    x_new, h2t, rinfo, rinfo_t, counts = _out_projection(
        a.reshape(n, ATTN_WIDTH), cv.reshape(n, CONV_WIDTH), w_out[0].astype(BF16),
        x.reshape(n, d), mod3, norm2_g, wr, tri, s)

    cnt = counts[:, 0].astype(jnp.int32)
    padded = ((cnt + MOE_ROWS - 1) // MOE_ROWS) * MOE_ROWS
    pends = jnp.cumsum(padded)
    pstarts = pends - padded
    experts = jnp.arange(N_EXPERTS, dtype=jnp.int32)[:, None]

    def slots(k):
        eid = rinfo_t[k].astype(jnp.int32)
        rank = rinfo_t[TOP_K + k].astype(jnp.int32)
        return jnp.sum(jnp.where(eid[None, :] == experts, pstarts[:, None], 0), axis=0) + rank

    dest0, dest1 = slots(0), slots(1)
    p_rows = n * TOP_K + N_EXPERTS * MOE_ROWS

    xs = _dispatch(pends, padded, dest0, dest1, h2t, p_rows)
    ybuf = _experts(pstarts // MOE_ROWS, padded // MOE_ROWS, xs, w_gate[0], w_up[0], w_down[0])
    out = _combine(dest0, dest1, rinfo, x_new, mod3, final_g.reshape(1, d), ybuf, s)
    return out.reshape(b, s, d)
```

```python
import functools

import jax
import jax.numpy as jnp
from jax import lax
from jax.experimental import pallas as pl
from jax.experimental.pallas import tpu as pltpu

F32 = jnp.float32
BF16 = jnp.bfloat16

D_MODEL = 1024
GRID_W = 64
ATTN_WIDTH = 512
N_HEADS = 8
N_KV_HEADS = 2
HEAD_DIM = 64
KV_REP = N_HEADS // N_KV_HEADS
KV_WIDTH = N_KV_HEADS * HEAD_DIM
CONV_WIDTH = 512
IN_COLS = ATTN_WIDTH + 2 * KV_WIDTH + 3 * CONV_WIDTH
ROPE_THETA = 10000.0
ROPE_AXIS_DIM = HEAD_DIM // 2
ROPE_FREQS = ROPE_AXIS_DIM // 2
N_GROUPS = 4
EXPERTS_PER_GROUP = 8
N_EXPERTS = N_GROUPS * EXPERTS_PER_GROUP
TOP_K = 2
D_EXPERT = 768
N_MOD = 6
EPS = 1e-6
LOG2_E = 1.4426950408889634
Q_SCALE = HEAD_DIM ** -0.5 * LOG2_E
V_LANES = 2 * HEAD_DIM

LANES = 128
ROW_TILE = 8
PACK_ROWS = 4
HIGH_HALF = -65536
MOD_ROWS = 16
IN_ROWS = 512
CONV_PAD = 8
ATTN_TQ = 512
ATTN_SUB_ROWS = 256
OUT_ROWS = 1024
DISPATCH_ROWS = 2048
DMA_UNROLL = 8
MOE_ROWS = 256
EXPERT_IN_SLOTS = 3
COMBINE_ROWS = 512
VMEM_LIMIT = 56 * 1024 * 1024


def _params(semantics, vmem=None):
    return pltpu.CompilerParams(dimension_semantics=semantics,
                                vmem_limit_bytes=vmem if vmem else VMEM_LIMIT)


def _rms(x):
    return x * lax.rsqrt(jnp.mean(x * x, axis=-1, keepdims=True) + EPS)


def _store_row_tiles(ref, val):
    rows, width = val.shape
    assert width == 2 * PACK_ROWS * LANES
    half = width // 2

    def bits(v):
        return lax.bitcast_convert_type(v.astype(BF16).astype(F32), jnp.int32)

    words = lax.shift_right_logical(bits(val[:, :half]), 16) | (bits(val[:, half:]) & HIGH_HALF)
    for i in range(PACK_ROWS):
        ref[pl.ds(i, rows, stride=PACK_ROWS), :] = words[:, i * LANES:(i + 1) * LANES]


def _load_row_tiles(ref, rows, *lead):
    words = [ref[(*lead, pl.ds(i, rows, stride=PACK_ROWS), slice(None))] for i in range(PACK_ROWS)]
    low = [lax.bitcast_convert_type(lax.shift_left(w, 16), F32) for w in words]
    high = [lax.bitcast_convert_type(w & HIGH_HALF, F32) for w in words]
    return jnp.concatenate(low + high, axis=1)


def _row_tile(ref, row, *lead):
    return ref.at[(*lead, pl.ds(pl.multiple_of(row * PACK_ROWS, PACK_ROWS), PACK_ROWS))]


def _mod_kernel(c_ref, w_ref, b_ref, o_ref):
    c = c_ref[...]
    s = c * jax.nn.sigmoid(c)
    o_ref[...] = jnp.dot(s.astype(BF16), w_ref[...].astype(BF16),
                         preferred_element_type=F32) + b_ref[...]


def _modulation(cond, w_mod, b_mod):
    d = cond.shape[1]
    n_out = w_mod.shape[1]
    tn = 1024
    return pl.pallas_call(
        _mod_kernel,
        grid=(n_out // tn,),
        in_specs=[pl.BlockSpec((MOD_ROWS, d), lambda j: (0, 0)),
                  pl.BlockSpec((d, tn), lambda j: (0, j)),
                  pl.BlockSpec((1, tn), lambda j: (0, j))],
        out_specs=pl.BlockSpec((MOD_ROWS, tn), lambda j: (0, j)),
        out_shape=jax.ShapeDtypeStruct((MOD_ROWS, n_out), F32),
        compiler_params=_params(("arbitrary",)),
        name="modulation",
    )(cond, w_mod, b_mod.reshape(1, n_out))


def _with_ones(v):
    return jnp.concatenate([v, jnp.ones_like(v)], axis=1).astype(BF16)


def _rope(x, cos, sin_signed, is_lo):
    partner = jnp.where(is_lo, pltpu.roll(x, LANES - ROPE_FREQS, 1), pltpu.roll(x, ROPE_FREQS, 1))
    return x * cos + partner * sin_signed


def _inproj_kernel(x_ref, ctx_ref, mod_ref, cmod_ref, n1g_ref, w_ref, qg_ref, kg_ref,
                   cos_ref, sin_ref, convw_ref, cog_ref, bd_ref,
                   q_ref, k_ref, v_ref, cv_ref, p_s, gb_s):
    seq = x_ref.shape[1]
    ctx_len = ctx_ref.shape[1]
    g1 = n1g_ref[...]
    bd = bd_ref[...]
    bd_kv = bd_ref[0:KV_WIDTH, 0:KV_WIDTH]
    kg = kg_ref[...]
    qg = qg_ref[...]

    def head_ms(z, m):
        return jnp.dot((z * z).astype(BF16), m, preferred_element_type=F32)

    cmod = cmod_ref[0]
    hc = _rms(ctx_ref[0]) * g1 * (1.0 + cmod[1:2]) + cmod[0:1]
    zc = jnp.dot(hc.astype(BF16), w_ref[:, ATTN_WIDTH:ATTN_WIDTH + 2 * KV_WIDTH],
                 preferred_element_type=F32)
    kc = zc[:, :KV_WIDTH]
    kc = kc * lax.rsqrt(head_ms(kc, bd_kv) + EPS) * kg
    vc = zc[:, KV_WIDTH:]
    for g in range(N_KV_HEADS):
        k_ref[0, g, 0:ctx_len, :] = kc[:, g * HEAD_DIM:(g + 1) * HEAD_DIM].astype(BF16)
        v_ref[0, g, 0:ctx_len, :] = _with_ones(vc[:, g * HEAD_DIM:(g + 1) * HEAD_DIM])

    zeros = jnp.zeros((CONV_PAD, CONV_WIDTH), F32)
    p_s[0:CONV_PAD, :] = zeros
    p_s[CONV_PAD + seq:CONV_PAD + seq + CONV_PAD, :] = zeros

    mod = mod_ref[0]
    sh1 = mod[0:1]
    sc1 = mod[1:2]
    lane = lax.broadcasted_iota(jnp.int32, (IN_ROWS, LANES), 1)
    is_lo = (lane // ROPE_FREQS) % 2 == 0

    def proj_chunk(c, carry):
        r0 = pl.multiple_of(c * IN_ROWS, IN_ROWS)
        h = (_rms(x_ref[0, pl.ds(r0, IN_ROWS), :]) * g1 * (1.0 + sc1) + sh1).astype(BF16)
        cos = cos_ref[pl.ds(r0, IN_ROWS), :]
        sin = sin_ref[pl.ds(r0, IN_ROWS), :]
        zq = jnp.dot(h, w_ref[:, 0:ATTN_WIDTH], preferred_element_type=F32)
        qn = zq * lax.rsqrt(head_ms(zq, bd) + EPS) * qg
        for j in range(ATTN_WIDTH // LANES):
            blk = _rope(qn[:, j * LANES:(j + 1) * LANES], cos, sin, is_lo) * Q_SCALE
            for hh in range(LANES // HEAD_DIM):
                head = j * (LANES // HEAD_DIM) + hh
                q_ref[0, head, pl.ds(r0, IN_ROWS), :] = (
                    blk[:, hh * HEAD_DIM:(hh + 1) * HEAD_DIM].astype(BF16))
        zkv = jnp.dot(h, w_ref[:, ATTN_WIDTH:ATTN_WIDTH + 2 * KV_WIDTH], preferred_element_type=F32)
        kx = zkv[:, :KV_WIDTH]
        kx = _rope(kx * lax.rsqrt(head_ms(kx, bd_kv) + EPS) * kg, cos, sin, is_lo)
        vx = zkv[:, KV_WIDTH:]
        for g in range(N_KV_HEADS):
            k_ref[0, g, pl.ds(ctx_len + r0, IN_ROWS), :] = kx[:, g * HEAD_DIM:(g + 1) * HEAD_DIM].astype(BF16)
            v_ref[0, g, pl.ds(ctx_len + r0, IN_ROWS), :] = _with_ones(vx[:, g * HEAD_DIM:(g + 1) * HEAD_DIM])
        c0 = ATTN_WIDTH + 2 * KV_WIDTH
        gb_s[pl.ds(r0, IN_ROWS), :] = jnp.dot(h, w_ref[:, c0:c0 + CONV_WIDTH], preferred_element_type=F32)
        zc_ = jnp.dot(h, w_ref[:, c0 + CONV_WIDTH:c0 + 2 * CONV_WIDTH], preferred_element_type=F32)
        zu = jnp.dot(h, w_ref[:, c0 + 2 * CONV_WIDTH:c0 + 3 * CONV_WIDTH], preferred_element_type=F32)
        p_s[pl.ds(CONV_PAD + r0, IN_ROWS), :] = zc_ * zu
        return carry

    lax.fori_loop(0, seq // IN_ROWS, proj_chunk, 0)

    cw = convw_ref[...]
    cog = cog_ref[...]

    def conv_chunk(c, carry):
        r0 = pl.multiple_of(c * IN_ROWS, IN_ROWS)
        win = p_s[pl.ds(r0, IN_ROWS + 2 * CONV_PAD), :]
        n_win = IN_ROWS + 2 * CONV_PAD
        prev = pltpu.roll(win, 1, 0)[CONV_PAD:CONV_PAD + IN_ROWS]
        cur = win[CONV_PAD:CONV_PAD + IN_ROWS]
        nxt = pltpu.roll(win, n_win - 1, 0)[CONV_PAD:CONV_PAD + IN_ROWS]
        y = cw[0:1] * prev + cw[1:2] * cur + cw[2:3] * nxt
        cvv = gb_s[pl.ds(r0, IN_ROWS), :] * y
        cvn = cvv * lax.rsqrt(head_ms(cvv, bd) + EPS) * cog
        cv_ref[0, pl.ds(r0, IN_ROWS), :] = cvn.astype(BF16)
        return carry

    lax.fori_loop(0, seq // IN_ROWS, conv_chunk, 0)


def _in_projection(x, ctx, mod3, norm1_g, w_in_bf, qg_t, kg_t, cos_t, sin_t, conv_w, conv_out_g, bd):
    b, s, d = x.shape
    ctx_len = ctx.shape[1]
    n_keys = ctx_len + s
    const = lambda *shape: pl.BlockSpec(shape, lambda i: (0,) * len(shape))
    return pl.pallas_call(
        _inproj_kernel,
        grid=(b,),
        in_specs=[
            pl.BlockSpec((1, s, d), lambda i: (i, 0, 0)),
            pl.BlockSpec((1, ctx_len, d), lambda i: (i, 0, 0)),
            pl.BlockSpec((1, N_MOD, d), lambda i: (i, 0, 0)),
            pl.BlockSpec((1, N_MOD, d), lambda i: (b, 0, 0)),
            const(1, d),
            const(d, IN_COLS),
            const(1, ATTN_WIDTH),
            const(1, KV_WIDTH),
            const(s, LANES),
            const(s, LANES),
            const(3, CONV_WIDTH),
            const(1, CONV_WIDTH),
            const(ATTN_WIDTH, ATTN_WIDTH),
        ],
        out_specs=[
            pl.BlockSpec((1, N_HEADS, s, HEAD_DIM), lambda i: (i, 0, 0, 0)),
            pl.BlockSpec((1, N_KV_HEADS, n_keys, HEAD_DIM), lambda i: (i, 0, 0, 0)),
            pl.BlockSpec((1, N_KV_HEADS, n_keys, V_LANES), lambda i: (i, 0, 0, 0)),
            pl.BlockSpec((1, s, CONV_WIDTH), lambda i: (i, 0, 0)),
        ],
        out_shape=[
            jax.ShapeDtypeStruct((b, N_HEADS, s, HEAD_DIM), BF16),
            jax.ShapeDtypeStruct((b, N_KV_HEADS, n_keys, HEAD_DIM), BF16),
            jax.ShapeDtypeStruct((b, N_KV_HEADS, n_keys, V_LANES), BF16),
            jax.ShapeDtypeStruct((b, s, CONV_WIDTH), BF16),
        ],
        scratch_shapes=[pltpu.VMEM((s + 2 * CONV_PAD, CONV_WIDTH), F32),
                        pltpu.VMEM((s, CONV_WIDTH), F32)],
        compiler_params=_params(("arbitrary",)),
        name="in_projection",
    )(x, ctx, mod3, mod3, norm1_g, w_in_bf, qg_t, kg_t, cos_t, sin_t, conv_w, conv_out_g, bd)


def _attn_kernel(q_ref, k_ref, v_ref, g_ref, o_ref):
    tq = q_ref.shape[2]
    subs = [(r0, h) for r0 in range(0, tq, ATTN_SUB_ROWS) for h in range(N_HEADS)]
    scores = [lax.dot_general(q_ref[0, h, r0:r0 + ATTN_SUB_ROWS, :], k_ref[0, h // KV_REP],
                              (((1,), (1,)), ((), ())), preferred_element_type=F32)
              for r0, h in subs]
    probs = [jnp.exp2(s - jnp.max(s, axis=-1, keepdims=True)).astype(BF16) for s in scores]
    lane = lax.broadcasted_iota(jnp.int32, (ATTN_SUB_ROWS, LANES), 1)
    rows = []
    for i, ((r0, h), p) in enumerate(zip(subs, probs)):
        ov = jnp.dot(p, v_ref[0, h // KV_REP], preferred_element_type=F32)
        o = ov / pltpu.roll(ov, HEAD_DIM, 1)
        ms = jnp.sum(jnp.where(lane < HEAD_DIM, o * o, 0.0), axis=-1, keepdims=True) * (1.0 / HEAD_DIM)
        if h == 0:
            rows.append([])
        rows[-1].append((o * lax.rsqrt(ms + EPS))[:, 0:HEAD_DIM])
    out = jnp.concatenate([jnp.concatenate(r, axis=1) for r in rows], axis=0)
    o_ref[0] = (out * g_ref[...]).astype(BF16)


def _attention(q, k, v, attn_out_g):
    b, _, s, _ = q.shape
    n_keys = k.shape[2]
    return pl.pallas_call(
        _attn_kernel,
        grid=(b, s // ATTN_TQ),
        in_specs=[
            pl.BlockSpec((1, N_HEADS, ATTN_TQ, HEAD_DIM), lambda i, j: (i, 0, j, 0)),
            pl.BlockSpec((1, N_KV_HEADS, n_keys, HEAD_DIM), lambda i, j: (i, 0, 0, 0)),
            pl.BlockSpec((1, N_KV_HEADS, n_keys, V_LANES), lambda i, j: (i, 0, 0, 0)),
            pl.BlockSpec((1, ATTN_WIDTH), lambda i, j: (0, 0)),
        ],
        out_specs=pl.BlockSpec((1, ATTN_TQ, ATTN_WIDTH), lambda i, j: (i, j, 0)),
        out_shape=jax.ShapeDtypeStruct((b, s, ATTN_WIDTH), BF16),
        compiler_params=_params(("arbitrary", "arbitrary")),
        name="attention",
    )(q, k, v, attn_out_g)


def _outproj_kernel(a_ref, cv_ref, w_ref, x_ref, mod_ref, n2g_ref, wr_ref, upper_ref,
                    xn_ref, h2_ref, ri_ref, rit_ref, cnt_ref, carry_s):
    i = pl.program_id(0)

    @pl.when(i == 0)
    def _():
        carry_s[...] = jnp.zeros_like(carry_s)

    mod = mod_ref[0]
    merged = (jnp.dot(a_ref[...], w_ref[0:ATTN_WIDTH, :], preferred_element_type=F32)
              + jnp.dot(cv_ref[...], w_ref[ATTN_WIDTH:, :], preferred_element_type=F32))
    xn = x_ref[...] + mod[2:3] * merged
    xn_ref[...] = xn
    h2 = _rms(xn) * n2g_ref[...] * (1.0 + mod[4:5]) + mod[3:4]
    _store_row_tiles(h2_ref, h2)

    logits = jnp.dot(h2.astype(BF16), wr_ref[...], preferred_element_type=F32)
    lt = jnp.transpose(logits)
    rows = logits.shape[0]
    row = lax.broadcasted_iota(jnp.int32, (EXPERTS_PER_GROUP, rows), 0).astype(F32)
    neg = jnp.float32(-jnp.inf)
    none = jnp.float32(EXPERTS_PER_GROUP)

    def first_at(mask):
        return jnp.min(jnp.where(mask, row, none), axis=0, keepdims=True)

    gvalid = row < N_GROUPS
    lg = jnp.where(gvalid, lt[N_EXPERTS:N_EXPERTS + EXPERTS_PER_GROUP], neg)
    ge = jnp.exp(lg - jnp.max(lg, axis=0, keepdims=True))
    g_prob = ge / jnp.sum(ge, axis=0, keepdims=True)
    g_w = jnp.max(g_prob, axis=0, keepdims=True)
    g_sel = first_at(gvalid & (g_prob == g_w))

    le = lt[(N_GROUPS - 1) * EXPERTS_PER_GROUP:N_GROUPS * EXPERTS_PER_GROUP]
    for g in range(N_GROUPS - 2, -1, -1):
        le = jnp.where(g_sel == g, lt[g * EXPERTS_PER_GROUP:(g + 1) * EXPERTS_PER_GROUP], le)
    ee = jnp.exp(le - jnp.max(le, axis=0, keepdims=True))
    e_prob = ee / jnp.sum(ee, axis=0, keepdims=True)
    p1 = jnp.max(e_prob, axis=0, keepdims=True)
    i1 = first_at(e_prob == p1)
    rest = row != i1
    p2 = jnp.max(jnp.where(rest, e_prob, -1.0), axis=0, keepdims=True)
    i2 = first_at(rest & (e_prob == p2))
    psum = p1 + p2
    w1 = g_w * (p1 / psum)
    w2 = g_w * (p2 / psum)

    def expert_rows(i_sel):
        return jnp.concatenate([jnp.where((g_sel == g) & (row == i_sel), 1.0, 0.0)
                                for g in range(N_GROUPS)], axis=0)

    hit1 = expert_rows(i1)
    hit2 = expert_rows(i2)
    onehot = hit1 + hit2
    before = jnp.dot(onehot, upper_ref[...], preferred_element_type=F32) + carry_s[:, 0:1]
    r1 = jnp.sum(hit1 * before, axis=0, keepdims=True)
    r2 = jnp.sum(hit2 * before, axis=0, keepdims=True)
    carry_s[...] = carry_s[...] + jnp.sum(onehot, axis=1, keepdims=True)

    base = g_sel * EXPERTS_PER_GROUP
    info_t = jnp.where(row == 0, base + i1,
             jnp.where(row == 1, base + i2,
             jnp.where(row == 2, r1,
             jnp.where(row == 3, r2,
             jnp.where(row == 4, w1,
             jnp.where(row == 5, w2, 0.0))))))
    rit_ref[...] = info_t
    ri_ref[...] = jnp.transpose(
        jnp.concatenate([info_t, jnp.zeros((LANES - ROW_TILE, rows), F32)], axis=0))
    cnt_ref[...] = carry_s[...]


def _out_projection(a, cv, w_out_bf, x2, mod3, norm2_g, wr_bf, tri, seq):
    n, d = x2.shape
    tiles_per_batch = seq // OUT_ROWS
    return pl.pallas_call(
        _outproj_kernel,
        grid=(n // OUT_ROWS,),
        in_specs=[
            pl.BlockSpec((OUT_ROWS, ATTN_WIDTH), lambda i: (i, 0)),
            pl.BlockSpec((OUT_ROWS, CONV_WIDTH), lambda i: (i, 0)),
            pl.BlockSpec((d, d), lambda i: (0, 0)),
            pl.BlockSpec((OUT_ROWS, d), lambda i: (i, 0)),
            pl.BlockSpec((1, N_MOD, d), lambda i: (i // tiles_per_batch, 0, 0)),
            pl.BlockSpec((1, d), lambda i: (0, 0)),
            pl.BlockSpec((d, LANES), lambda i: (0, 0)),
            pl.BlockSpec((OUT_ROWS, OUT_ROWS), lambda i: (0, 0)),
        ],
        out_specs=[
            pl.BlockSpec((OUT_ROWS, d), lambda i: (i, 0)),
            pl.BlockSpec((OUT_ROWS * PACK_ROWS, LANES), lambda i: (i, 0)),
            pl.BlockSpec((OUT_ROWS, LANES), lambda i: (i, 0)),
            pl.BlockSpec((ROW_TILE, OUT_ROWS), lambda i: (0, i)),
            pl.BlockSpec((N_EXPERTS, LANES), lambda i: (0, 0)),
        ],
        out_shape=[
            jax.ShapeDtypeStruct((n, d), F32),
            jax.ShapeDtypeStruct((n * PACK_ROWS, LANES), jnp.int32),
            jax.ShapeDtypeStruct((n, LANES), F32),
            jax.ShapeDtypeStruct((ROW_TILE, n), F32),
            jax.ShapeDtypeStruct((N_EXPERTS, LANES), F32),
        ],
        scratch_shapes=[pltpu.VMEM((N_EXPERTS, LANES), F32)],
        compiler_params=_params(("arbitrary",)),
        name="out_projection_routing",
    )(a, cv, w_out_bf, x2, mod3, norm2_g, wr_bf, tri)


def _dispatch_kernel(pend_ref, padded_ref, d0_ref, d1_ref, h_ref, xs_hbm, zero_s, sem):
    tile = MOE_ROWS * PACK_ROWS

    @pl.when(pl.program_id(0) == 0)
    def _():
        zero_s[...] = jnp.zeros_like(zero_s)

        def last_tile(e):
            start = pl.multiple_of((pend_ref[e] - MOE_ROWS) * PACK_ROWS, tile)
            return pltpu.make_async_copy(zero_s, xs_hbm.at[pl.ds(start, tile)], sem)

        def spare_tile(j):
            return pltpu.make_async_copy(zero_s, xs_hbm.at[pl.ds(j * tile, tile)], sem)

        n_tiles = xs_hbm.shape[0] // tile
        used = pend_ref[N_EXPERTS - 1] // MOE_ROWS
        for e in range(N_EXPERTS):
            @pl.when(padded_ref[e] > 0)
            def _():
                last_tile(e).start()
        for j in range(n_tiles - N_EXPERTS, n_tiles):
            @pl.when(j >= used)
            def _():
                spare_tile(j).start()
        for e in range(N_EXPERTS):
            @pl.when(padded_ref[e] > 0)
            def _():
                last_tile(e).wait()
        for j in range(n_tiles - N_EXPERTS, n_tiles):
            @pl.when(j >= used)
            def _():
                spare_tile(j).wait()

    def start(j, carry):
        for u in range(DMA_UNROLL):
            t = j * DMA_UNROLL + u
            for k, d_ref in enumerate((d0_ref, d1_ref)):
                pltpu.make_async_copy(_row_tile(h_ref, t), _row_tile(xs_hbm, d_ref[0, 0, t]),
                                      sem).start(priority=k)
        return carry

    lax.fori_loop(0, DISPATCH_ROWS // DMA_UNROLL, start, 0)

    def drain(j, carry):
        for _ in range(DMA_UNROLL * TOP_K):
            pltpu.make_async_copy(_row_tile(h_ref, 0), _row_tile(xs_hbm, 0), sem).wait()
        return carry

    lax.fori_loop(0, DISPATCH_ROWS // DMA_UNROLL, drain, 0)


def _dispatch(pends, padded, dest0, dest1, h2t, p_rows):
    n = h2t.shape[0] // PACK_ROWS
    steps = n // DISPATCH_ROWS
    smem_rows = lambda: pl.BlockSpec((1, 1, DISPATCH_ROWS), lambda i, pe, pa: (i, 0, 0),
                                     memory_space=pltpu.SMEM)
    return pl.pallas_call(
        _dispatch_kernel,
        grid_spec=pltpu.PrefetchScalarGridSpec(
            num_scalar_prefetch=2,
            grid=(steps,),
            in_specs=[
                smem_rows(),
                smem_rows(),
                pl.BlockSpec((DISPATCH_ROWS * PACK_ROWS, LANES), lambda i, pe, pa: (i, 0)),
            ],
            out_specs=pl.BlockSpec(memory_space=pl.ANY),
            scratch_shapes=[pltpu.VMEM((MOE_ROWS * PACK_ROWS, LANES), jnp.int32),
                            pltpu.SemaphoreType.DMA(())],
        ),
        out_shape=jax.ShapeDtypeStruct((p_rows * PACK_ROWS, LANES), jnp.int32),
        compiler_params=_params(("arbitrary",)),
        name="moe_dispatch",
    )(pends, padded, dest0.reshape(steps, 1, DISPATCH_ROWS), dest1.reshape(steps, 1, DISPATCH_ROWS), h2t)


def _experts_kernel(first_ref, count_ref, xs_hbm, wg_ref, wu_ref, wd_ref, y_hbm,
                    x_s, y_s, wg_s, wu_s, wd_s, in_sem, out_sem, pending):
    e = pl.program_id(0)
    last = pl.num_programs(0) - 1

    @pl.when(e == 0)
    def _():
        pending[0] = 0
        pending[1] = 0

    n = count_ref[e]
    tile = MOE_ROWS * PACK_ROWS
    ahead = EXPERT_IN_SLOTS - 1

    def rows(first, j):
        return pl.ds(pl.multiple_of((first + j) * tile, tile), tile)

    def fetch(first, j):
        slot = j % EXPERT_IN_SLOTS
        return pltpu.make_async_copy(xs_hbm.at[rows(first, j)], x_s.at[slot], in_sem.at[slot])

    def writeback(first, j):
        slot = j % 2
        return pltpu.make_async_copy(y_s.at[slot], y_hbm.at[rows(first, j)], out_sem.at[slot])

    def start_head(ex):
        for j in range(ahead):
            @pl.when(j < count_ref[ex])
            def _():
                fetch(first_ref[ex], j).start(priority=1)

    @pl.when(n > 0)
    def _():
        first = first_ref[e]

        @pl.when(jnp.logical_or(e == 0, count_ref[jnp.maximum(e - 1, 0)] == 0))
        def _():
            start_head(e)

        wg_s[...] = wg_ref[0].astype(BF16)
        wu_s[...] = wu_ref[0].astype(BF16)
        wd_s[...] = wd_ref[0].astype(BF16)

        def tile_step(j, carry):
            @pl.when(j + ahead < n)
            def _():
                fetch(first, j + ahead).start(priority=1)

            fetch(first, j).wait()

            @pl.when(pending[j % 2] == 1)
            def _():
                writeback(first, j).wait()

            x = _load_row_tiles(x_s, MOE_ROWS, j % EXPERT_IN_SLOTS).astype(BF16)
            g = jnp.dot(x, wg_s[...], preferred_element_type=F32)
            u = jnp.dot(x, wu_s[...], preferred_element_type=F32)
            h = ((g * jax.nn.sigmoid(g)) * u).astype(BF16)
            _store_row_tiles(y_s.at[j % 2], jnp.dot(h, wd_s[...], preferred_element_type=F32))
            writeback(first, j).start(priority=1)
            pending[j % 2] = 1
            return carry

        lax.fori_loop(0, n, tile_step, 0)

        @pl.when(e < last)
        def _():
            start_head(jnp.minimum(e + 1, last))

    @pl.when(e == last)
    def _():
        for slot in range(2):
            @pl.when(pending[slot] == 1)
            def _():
                writeback(0, slot).wait()
                pending[slot] = 0


def _experts(first_tile, tile_count, xs, w_gate, w_up, w_down):
    n_exp, d, d_exp = w_gate.shape
    tile = MOE_ROWS * PACK_ROWS
    w_map = lambda i, ft, tc: (i, 0, 0)
    return pl.pallas_call(
        _experts_kernel,
        grid_spec=pltpu.PrefetchScalarGridSpec(
            num_scalar_prefetch=2,
            grid=(n_exp,),
            in_specs=[
                pl.BlockSpec(memory_space=pl.ANY),
                pl.BlockSpec((1, d, d_exp), w_map),
                pl.BlockSpec((1, d, d_exp), w_map),
                pl.BlockSpec((1, d_exp, d), w_map),
            ],
            out_specs=pl.BlockSpec(memory_space=pl.ANY),
            scratch_shapes=[
                pltpu.VMEM((EXPERT_IN_SLOTS, tile, LANES), jnp.int32),
                pltpu.VMEM((2, tile, LANES), jnp.int32),
                pltpu.VMEM((d, d_exp), BF16),
                pltpu.VMEM((d, d_exp), BF16),
                pltpu.VMEM((d_exp, d), BF16),
                pltpu.SemaphoreType.DMA((EXPERT_IN_SLOTS,)),
                pltpu.SemaphoreType.DMA((2,)),
                pltpu.SMEM((2,), jnp.int32),
            ],
        ),
        out_shape=jax.ShapeDtypeStruct(xs.shape, xs.dtype),
        input_output_aliases={2: 0},
        compiler_params=_params(("arbitrary",)),
        name="moe_experts",
    )(first_tile, tile_count, xs, w_gate, w_up, w_down)


def _combine_kernel(d0_ref, d1_ref, d0n_ref, d1n_ref, ri_ref, xn_ref, mod_ref, fg_ref, y_hbm,
                    o_ref, buf, sem):
    i = pl.program_id(0)
    slot = i % 2

    def gather(refs, to_slot):
        def start(j, carry):
            for u in range(DMA_UNROLL):
                t = j * DMA_UNROLL + u
                for k, d_ref in enumerate(refs):
                    pltpu.make_async_copy(_row_tile(y_hbm, d_ref[0, 0, t]), _row_tile(buf, t, to_slot, k),
                                          sem.at[to_slot]).start(priority=k)
            return carry

        lax.fori_loop(0, COMBINE_ROWS // DMA_UNROLL, start, 0)

    @pl.when(i == 0)
    def _():
        gather((d0_ref, d1_ref), 0)

    @pl.when(i + 1 < pl.num_programs(0))
    def _():
        gather((d0n_ref, d1n_ref), 1 - slot)

    def drain(j, carry):
        for _ in range(DMA_UNROLL * TOP_K):
            pltpu.make_async_copy(_row_tile(y_hbm, 0), _row_tile(buf, 0, slot, 0), sem.at[slot]).wait()
        return carry

    lax.fori_loop(0, COMBINE_ROWS // DMA_UNROLL, drain, 0)

    ri = ri_ref[...]
    y = (_load_row_tiles(buf, COMBINE_ROWS, slot, 0) * ri[:, 4:5]
         + _load_row_tiles(buf, COMBINE_ROWS, slot, 1) * ri[:, 5:6])
    xf = xn_ref[...] + mod_ref[0][5:6] * y
    o_ref[...] = _rms(xf) * fg_ref[...]


def _combine(dest0, dest1, rinfo, x_new, mod3, final_g, ybuf, seq):
    n, d = x_new.shape
    steps = n // COMBINE_ROWS
    tiles_per_batch = seq // COMBINE_ROWS
    cur = lambda: pl.BlockSpec((1, 1, COMBINE_ROWS), lambda i: (i, 0, 0), memory_space=pltpu.SMEM)
    nxt = lambda: pl.BlockSpec((1, 1, COMBINE_ROWS), lambda i: (jnp.minimum(i + 1, steps - 1), 0, 0),
                               memory_space=pltpu.SMEM)
    d0 = dest0.reshape(steps, 1, COMBINE_ROWS)
    d1 = dest1.reshape(steps, 1, COMBINE_ROWS)
    return pl.pallas_call(
        _combine_kernel,
        grid=(steps,),
        in_specs=[
            cur(), cur(), nxt(), nxt(),
            pl.BlockSpec((COMBINE_ROWS, LANES), lambda i: (i, 0)),
            pl.BlockSpec((COMBINE_ROWS, d), lambda i: (i, 0)),
            pl.BlockSpec((1, N_MOD, d), lambda i: (i // tiles_per_batch, 0, 0)),
            pl.BlockSpec((1, d), lambda i: (0, 0)),
            pl.BlockSpec(memory_space=pl.ANY),
        ],
        out_specs=pl.BlockSpec((COMBINE_ROWS, d), lambda i: (i, 0)),
        out_shape=jax.ShapeDtypeStruct((n, d), F32),
        scratch_shapes=[pltpu.VMEM((2, TOP_K, COMBINE_ROWS * PACK_ROWS, LANES), jnp.int32),
                        pltpu.SemaphoreType.DMA((2,))],
        compiler_params=_params(("arbitrary",)),
        name="moe_combine",
    )(d0, d1, d0, d1, rinfo, x_new, mod3, final_g, ybuf)


def _rope_tables(seq):
    rows = seq // GRID_W
    row_idx = jnp.repeat(jnp.arange(rows, dtype=F32), GRID_W)
    col_idx = jnp.tile(jnp.arange(GRID_W, dtype=F32), rows)
    inv_freq = ROPE_THETA ** (-jnp.arange(0, ROPE_AXIS_DIM, 2, dtype=F32) / ROPE_AXIS_DIM)
    ang = jnp.stack([row_idx[:, None] * inv_freq, col_idx[:, None] * inv_freq], axis=1)
    cos = jnp.cos(ang)
    sin = jnp.sin(ang)
    cos_h = jnp.stack([cos, cos], axis=2).reshape(seq, HEAD_DIM)
    sin_h = jnp.stack([-sin, sin], axis=2).reshape(seq, HEAD_DIM)
    reps = LANES // HEAD_DIM
    return jnp.tile(cos_h, (1, reps)), jnp.tile(sin_h, (1, reps))


def kernel(x, c, ctx, c_ctx, w_mod, b_mod, norm1_g, w_in, q_norm_g, k_norm_g, conv_w, attn_out_g,
           conv_out_g, w_out, norm2_g, w_group, w_router, w_gate, w_up, w_down, final_g):
    assert w_mod.shape[0] == 1, "single-layer block"
    b, s, d = x.shape
    n = b * s
    assert b + 1 <= MOD_ROWS

    cond = jnp.zeros((MOD_ROWS, d), F32).at[:b].set(c).at[b].set(c_ctx)
    mod3 = _modulation(cond, w_mod[0], b_mod[0]).reshape(MOD_ROWS, N_MOD, d)

    cos_t, sin_t = _rope_tables(s)
    head_of = jnp.arange(ATTN_WIDTH) // HEAD_DIM
    bd = jnp.where(head_of[:, None] == head_of[None, :], 1.0 / HEAD_DIM, 0.0).astype(BF16)
    q, k, v, cv = _in_projection(
        x, ctx, mod3, norm1_g, w_in[0].astype(BF16),
        jnp.tile(q_norm_g[0], N_HEADS)[None], jnp.tile(k_norm_g[0], N_KV_HEADS)[None],
        cos_t, sin_t, conv_w[0], conv_out_g, bd)

    a = _attention(q, k, v, attn_out_g)

    wr = jnp.zeros((d, LANES), F32).at[:, :N_EXPERTS].set(w_router[0])
    wr = wr.at[:, N_EXPERTS:N_EXPERTS + N_GROUPS].set(w_group[0]).astype(BF16)
    ti = jnp.arange(OUT_ROWS)
    tri = (ti[:, None] < ti[None, :]).astype(F32)
    x_new, h2t, rinfo, rinfo_t, counts = _out_projection(
        a.reshape(n, ATTN_WIDTH), cv.reshape(n, CONV_WIDTH), w_out[0].astype(BF16),
        x.reshape(n, d), mod3, norm2_g, wr, tri, s)

    cnt = counts[:, 0].astype(jnp.int32)
    padded = ((cnt + MOE_ROWS - 1) // MOE_ROWS) * MOE_ROWS
    pends = jnp.cumsum(padded)
    pstarts = pends - padded
    experts = jnp.arange(N_EXPERTS, dtype=jnp.int32)[:, None]

    def slots(k):
        eid = rinfo_t[k].astype(jnp.int32)
        rank = rinfo_t[TOP_K + k].astype(jnp.int32)
        return jnp.sum(jnp.where(eid[None, :] == experts, pstarts[:, None], 0), axis=0) + rank

    dest0, dest1 = slots(0), slots(1)
    p_rows = n * TOP_K + N_EXPERTS * MOE_ROWS

    xs = _dispatch(pends, padded, dest0, dest1, h2t, p_rows)
    ybuf = _experts(pstarts // MOE_ROWS, padded // MOE_ROWS, xs, w_gate[0], w_up[0], w_down[0])
    out = _combine(dest0, dest1, rinfo, x_new, mod3, final_g.reshape(1, d), ybuf, s)
    return out.reshape(b, s, d)
```

```python
import jax
import jax.numpy as jnp
from jax import lax
from jax.experimental import pallas as pl
from jax.experimental.pallas import tpu as pltpu

F32 = jnp.float32
BF16 = jnp.bfloat16

D_MODEL = 1024
GRID_W = 64
ATTN_WIDTH = 512
N_HEADS = 8
N_KV_HEADS = 2
HEAD_DIM = 64
KV_REP = N_HEADS // N_KV_HEADS
KV_WIDTH = N_KV_HEADS * HEAD_DIM
CONV_WIDTH = 512
IN_COLS = ATTN_WIDTH + 2 * KV_WIDTH + 3 * CONV_WIDTH
ROPE_THETA = 10000.0
ROPE_AXIS_DIM = HEAD_DIM // 2
ROPE_FREQS = ROPE_AXIS_DIM // 2
N_GROUPS = 4
EXPERTS_PER_GROUP = 8
N_EXPERTS = N_GROUPS * EXPERTS_PER_GROUP
TOP_K = 2
D_EXPERT = 768
N_MOD = 6
EPS = 1e-6
LOG2_E = 1.4426950408889634
Q_SCALE = HEAD_DIM ** -0.5 * LOG2_E
V_LANES = 2 * HEAD_DIM

LANES = 128
ROW_TILE = 8
PACK_ROWS = 4
HIGH_HALF = -65536
MOD_ROWS = 16
MOD_COLS = 2048
IN_ROWS = 512
CONV_PAD = 8
ATTN_TQ = 512
ATTN_SUB_ROWS = 256
OUT_ROWS = 1024
DISPATCH_ROWS = 2048
DMA_UNROLL = 8
MOE_ROWS = 256
EXPERT_IN_SLOTS = 3
COMBINE_ROWS = 512
VMEM_LIMIT = 56 * 1024 * 1024


def _params(semantics):
    return pltpu.CompilerParams(dimension_semantics=semantics, vmem_limit_bytes=VMEM_LIMIT)


def _rms(x):
    return x * lax.rsqrt(jnp.mean(x * x, axis=-1, keepdims=True) + EPS)


def _store_row_tiles(ref, val):
    rows, width = val.shape
    assert width == 2 * PACK_ROWS * LANES
    half = width // 2

    def bits(v):
        return lax.bitcast_convert_type(v.astype(BF16).astype(F32), jnp.int32)

    words = lax.shift_right_logical(bits(val[:, :half]), 16) | (bits(val[:, half:]) & HIGH_HALF)
    for i in range(PACK_ROWS):
        ref[pl.ds(i, rows, stride=PACK_ROWS), :] = words[:, i * LANES:(i + 1) * LANES]


def _load_row_tiles(ref, rows, *lead):
    words = [ref[(*lead, pl.ds(i, rows, stride=PACK_ROWS), slice(None))] for i in range(PACK_ROWS)]
    low = [lax.bitcast_convert_type(lax.shift_left(w, 16), F32) for w in words]
    high = [lax.bitcast_convert_type(w & HIGH_HALF, F32) for w in words]
    return jnp.concatenate(low + high, axis=1)


def _row_tile(ref, row, *lead):
    return ref.at[(*lead, pl.ds(pl.multiple_of(row * PACK_ROWS, PACK_ROWS), PACK_ROWS))]


def _mod_kernel(c_ref, w_ref, b_ref, o_ref):
    c = c_ref[...]
    s = c * jax.nn.sigmoid(c)
    o_ref[...] = jnp.dot(s.astype(BF16), w_ref[...].astype(BF16),
                         preferred_element_type=F32) + b_ref[...]


def _modulation(cond, w_mod, b_mod):
    d = cond.shape[1]
    n_out = w_mod.shape[1]
    tn = MOD_COLS
    return pl.pallas_call(
        _mod_kernel,
        grid=(n_out // tn,),
        in_specs=[pl.BlockSpec((MOD_ROWS, d), lambda j: (0, 0)),
                  pl.BlockSpec((d, tn), lambda j: (0, j)),
                  pl.BlockSpec((1, tn), lambda j: (0, j))],
        out_specs=pl.BlockSpec((MOD_ROWS, tn), lambda j: (0, j)),
        out_shape=jax.ShapeDtypeStruct((MOD_ROWS, n_out), F32),
        compiler_params=_params(("arbitrary",)),
        name="modulation",
    )(cond, w_mod, b_mod.reshape(1, n_out))


def _with_ones(v):
    return jnp.concatenate([v, jnp.ones_like(v)], axis=1).astype(BF16)


def _rope(x, cos, sin_signed, is_lo):
    partner = jnp.where(is_lo, pltpu.roll(x, LANES - ROPE_FREQS, 1), pltpu.roll(x, ROPE_FREQS, 1))
    return x * cos + partner * sin_signed


def _inproj_kernel(x_ref, ctx_ref, mod_ref, cmod_ref, n1g_ref, w_ref, qg_ref, kg_ref,
                   cos_ref, sin_ref, convw_ref, cog_ref, bd_ref,
                   q_ref, k_ref, v_ref, cv_ref, p_s, gb_s):
    seq = x_ref.shape[1]
    ctx_len = ctx_ref.shape[1]
    g1 = n1g_ref[...]
    bd = bd_ref[...]
    bd_kv = bd_ref[0:KV_WIDTH, 0:KV_WIDTH]
    kg = kg_ref[...]
    qg = qg_ref[...]

    def head_ms(z, m):
        return jnp.dot((z * z).astype(BF16), m, preferred_element_type=F32)

    cmod = cmod_ref[0]
    hc = _rms(ctx_ref[0]) * g1 * (1.0 + cmod[1:2]) + cmod[0:1]
    zc = jnp.dot(hc.astype(BF16), w_ref[:, ATTN_WIDTH:ATTN_WIDTH + 2 * KV_WIDTH],
                 preferred_element_type=F32)
    kc = zc[:, :KV_WIDTH]
    kc = kc * lax.rsqrt(head_ms(kc, bd_kv) + EPS) * kg
    vc = zc[:, KV_WIDTH:]
    for g in range(N_KV_HEADS):
        k_ref[0, g, 0:ctx_len, :] = kc[:, g * HEAD_DIM:(g + 1) * HEAD_DIM].astype(BF16)
        v_ref[0, g, 0:ctx_len, :] = _with_ones(vc[:, g * HEAD_DIM:(g + 1) * HEAD_DIM])

    zeros = jnp.zeros((CONV_PAD, CONV_WIDTH), F32)
    p_s[0:CONV_PAD, :] = zeros
    p_s[CONV_PAD + seq:CONV_PAD + seq + CONV_PAD, :] = zeros

    mod = mod_ref[0]
    sh1 = mod[0:1]
    sc1 = mod[1:2]
    lane = lax.broadcasted_iota(jnp.int32, (IN_ROWS, LANES), 1)
    is_lo = (lane // ROPE_FREQS) % 2 == 0
    first_head = lane < HEAD_DIM

    def pair_rsqrt(z):
        zz = z * z
        a = jnp.sum(jnp.where(first_head, zz, 0.0), axis=-1, keepdims=True) * (1.0 / HEAD_DIM)
        b = jnp.sum(jnp.where(first_head, 0.0, zz), axis=-1, keepdims=True) * (1.0 / HEAD_DIM)
        return jnp.where(first_head, lax.rsqrt(a + EPS), lax.rsqrt(b + EPS))

    def proj_chunk(c, carry):
        r0 = pl.multiple_of(c * IN_ROWS, IN_ROWS)
        h = (_rms(x_ref[0, pl.ds(r0, IN_ROWS), :]) * g1 * (1.0 + sc1) + sh1).astype(BF16)
        cos = cos_ref[pl.ds(r0, IN_ROWS), :]
        sin = sin_ref[pl.ds(r0, IN_ROWS), :]
        zq = jnp.dot(h, w_ref[:, 0:ATTN_WIDTH], preferred_element_type=F32)
        for j in range(ATTN_WIDTH // LANES):
            zj = zq[:, j * LANES:(j + 1) * LANES]
            qj = zj * pair_rsqrt(zj) * qg[:, j * LANES:(j + 1) * LANES]
            blk = _rope(qj, cos, sin, is_lo) * Q_SCALE
            for hh in range(LANES // HEAD_DIM):
                head = j * (LANES // HEAD_DIM) + hh
                q_ref[0, head, pl.ds(r0, IN_ROWS), :] = (
                    blk[:, hh * HEAD_DIM:(hh + 1) * HEAD_DIM].astype(BF16))
        zkv = jnp.dot(h, w_ref[:, ATTN_WIDTH:ATTN_WIDTH + 2 * KV_WIDTH], preferred_element_type=F32)
        kx = zkv[:, :KV_WIDTH]
        kx = _rope(kx * pair_rsqrt(kx) * kg, cos, sin, is_lo)
        vx = zkv[:, KV_WIDTH:]
        for g in range(N_KV_HEADS):
            k_ref[0, g, pl.ds(ctx_len + r0, IN_ROWS), :] = kx[:, g * HEAD_DIM:(g + 1) * HEAD_DIM].astype(BF16)
            v_ref[0, g, pl.ds(ctx_len + r0, IN_ROWS), :] = _with_ones(vx[:, g * HEAD_DIM:(g + 1) * HEAD_DIM])
        c0 = ATTN_WIDTH + 2 * KV_WIDTH
        gb_s[pl.ds(r0, IN_ROWS), :] = jnp.dot(h, w_ref[:, c0:c0 + CONV_WIDTH], preferred_element_type=F32)
        zc_ = jnp.dot(h, w_ref[:, c0 + CONV_WIDTH:c0 + 2 * CONV_WIDTH], preferred_element_type=F32)
        zu = jnp.dot(h, w_ref[:, c0 + 2 * CONV_WIDTH:c0 + 3 * CONV_WIDTH], preferred_element_type=F32)
        p_s[pl.ds(CONV_PAD + r0, IN_ROWS), :] = zc_ * zu
        return carry

    lax.fori_loop(0, seq // IN_ROWS, proj_chunk, 0)

    cw = convw_ref[...]
    cog = cog_ref[...]

    def conv_chunk(c, carry):
        r0 = pl.multiple_of(c * IN_ROWS, IN_ROWS)
        win = p_s[pl.ds(r0, IN_ROWS + 2 * CONV_PAD), :]
        n_win = IN_ROWS + 2 * CONV_PAD
        prev = pltpu.roll(win, 1, 0)[CONV_PAD:CONV_PAD + IN_ROWS]
        cur = win[CONV_PAD:CONV_PAD + IN_ROWS]
        nxt = pltpu.roll(win, n_win - 1, 0)[CONV_PAD:CONV_PAD + IN_ROWS]
        y = cw[0:1] * prev + cw[1:2] * cur + cw[2:3] * nxt
        cvv = gb_s[pl.ds(r0, IN_ROWS), :] * y
        cvn = cvv * lax.rsqrt(head_ms(cvv, bd) + EPS) * cog
        cv_ref[0, pl.ds(r0, IN_ROWS), :] = cvn.astype(BF16)
        return carry

    lax.fori_loop(0, seq // IN_ROWS, conv_chunk, 0)


def _in_projection(x, ctx, mod3, norm1_g, w_in_bf, qg_t, kg_t, cos_t, sin_t, conv_w, conv_out_g, bd):
    b, s, d = x.shape
    ctx_len = ctx.shape[1]
    n_keys = ctx_len + s
    const = lambda *shape: pl.BlockSpec(shape, lambda i: (0,) * len(shape))
    return pl.pallas_call(
        _inproj_kernel,
        grid=(b,),
        in_specs=[
            pl.BlockSpec((1, s, d), lambda i: (i, 0, 0)),
            pl.BlockSpec((1, ctx_len, d), lambda i: (i, 0, 0)),
            pl.BlockSpec((1, N_MOD, d), lambda i: (i, 0, 0)),
            pl.BlockSpec((1, N_MOD, d), lambda i: (b, 0, 0)),
            const(1, d),
            const(d, IN_COLS),
            const(1, ATTN_WIDTH),
            const(1, KV_WIDTH),
            const(s, LANES),
            const(s, LANES),
            const(3, CONV_WIDTH),
            const(1, CONV_WIDTH),
            const(ATTN_WIDTH, ATTN_WIDTH),
        ],
        out_specs=[
            pl.BlockSpec((1, N_HEADS, s, HEAD_DIM), lambda i: (i, 0, 0, 0)),
            pl.BlockSpec((1, N_KV_HEADS, n_keys, HEAD_DIM), lambda i: (i, 0, 0, 0)),
            pl.BlockSpec((1, N_KV_HEADS, n_keys, V_LANES), lambda i: (i, 0, 0, 0)),
            pl.BlockSpec((1, s, CONV_WIDTH), lambda i: (i, 0, 0)),
        ],
        out_shape=[
            jax.ShapeDtypeStruct((b, N_HEADS, s, HEAD_DIM), BF16),
            jax.ShapeDtypeStruct((b, N_KV_HEADS, n_keys, HEAD_DIM), BF16),
            jax.ShapeDtypeStruct((b, N_KV_HEADS, n_keys, V_LANES), BF16),
            jax.ShapeDtypeStruct((b, s, CONV_WIDTH), BF16),
        ],
        scratch_shapes=[pltpu.VMEM((s + 2 * CONV_PAD, CONV_WIDTH), F32),
                        pltpu.VMEM((s, CONV_WIDTH), F32)],
        compiler_params=_params(("arbitrary",)),
        name="in_projection",
    )(x, ctx, mod3, mod3, norm1_g, w_in_bf, qg_t, kg_t, cos_t, sin_t, conv_w, conv_out_g, bd)


def _attn_kernel(q_ref, k_ref, v_ref, g_ref, o_ref):
    tq = q_ref.shape[2]
    subs = [(r0, h) for r0 in range(0, tq, ATTN_SUB_ROWS) for h in range(N_HEADS)]
    scores = [lax.dot_general(q_ref[0, h, r0:r0 + ATTN_SUB_ROWS, :], k_ref[0, h // KV_REP],
                              (((1,), (1,)), ((), ())), preferred_element_type=F32)
              for r0, h in subs]
    probs = [jnp.exp2(s - jnp.max(s, axis=-1, keepdims=True)).astype(BF16) for s in scores]
    lane = lax.broadcasted_iota(jnp.int32, (ATTN_SUB_ROWS, LANES), 1)
    rows = []
    for i, ((r0, h), p) in enumerate(zip(subs, probs)):
        ov = jnp.dot(p, v_ref[0, h // KV_REP], preferred_element_type=F32)
        o = ov / pltpu.roll(ov, HEAD_DIM, 1)
        ms = jnp.sum(jnp.where(lane < HEAD_DIM, o * o, 0.0), axis=-1, keepdims=True) * (1.0 / HEAD_DIM)
        if h == 0:
            rows.append([])
        rows[-1].append((o * lax.rsqrt(ms + EPS))[:, 0:HEAD_DIM])
    out = jnp.concatenate([jnp.concatenate(r, axis=1) for r in rows], axis=0)
    o_ref[0] = (out * g_ref[...]).astype(BF16)


def _attention(q, k, v, attn_out_g):
    b, _, s, _ = q.shape
    n_keys = k.shape[2]
    return pl.pallas_call(
        _attn_kernel,
        grid=(b, s // ATTN_TQ),
        in_specs=[
            pl.BlockSpec((1, N_HEADS, ATTN_TQ, HEAD_DIM), lambda i, j: (i, 0, j, 0)),
            pl.BlockSpec((1, N_KV_HEADS, n_keys, HEAD_DIM), lambda i, j: (i, 0, 0, 0)),
            pl.BlockSpec((1, N_KV_HEADS, n_keys, V_LANES), lambda i, j: (i, 0, 0, 0)),
            pl.BlockSpec((1, ATTN_WIDTH), lambda i, j: (0, 0)),
        ],
        out_specs=pl.BlockSpec((1, ATTN_TQ, ATTN_WIDTH), lambda i, j: (i, j, 0)),
        out_shape=jax.ShapeDtypeStruct((b, s, ATTN_WIDTH), BF16),
        compiler_params=_params(("arbitrary", "arbitrary")),
        name="attention",
    )(q, k, v, attn_out_g)


def _outproj_kernel(a_ref, cv_ref, w_ref, x_ref, mod_ref, n2g_ref, wr_ref, upper_ref,
                    xn_ref, h2_ref, ri_ref, rit_ref, cnt_ref, carry_s):
    i = pl.program_id(0)

    @pl.when(i == 0)
    def _():
        carry_s[...] = jnp.zeros_like(carry_s)

    mod = mod_ref[0]
    merged = (jnp.dot(a_ref[...], w_ref[0:ATTN_WIDTH, :], preferred_element_type=F32)
              + jnp.dot(cv_ref[...], w_ref[ATTN_WIDTH:, :], preferred_element_type=F32))
    xn = x_ref[...] + mod[2:3] * merged
    xn_ref[...] = xn
    h2 = _rms(xn) * n2g_ref[...] * (1.0 + mod[4:5]) + mod[3:4]
    _store_row_tiles(h2_ref, h2)

    logits = jnp.dot(h2.astype(BF16), wr_ref[...], preferred_element_type=F32)
    lt = jnp.transpose(logits)
    rows = logits.shape[0]
    row = lax.broadcasted_iota(jnp.int32, (EXPERTS_PER_GROUP, rows), 0).astype(F32)
    neg = jnp.float32(-jnp.inf)
    none = jnp.float32(EXPERTS_PER_GROUP)

    def first_at(mask):
        return jnp.min(jnp.where(mask, row, none), axis=0, keepdims=True)

    gvalid = row < N_GROUPS
    lg = jnp.where(gvalid, lt[N_EXPERTS:N_EXPERTS + EXPERTS_PER_GROUP], neg)
    ge = jnp.exp(lg - jnp.max(lg, axis=0, keepdims=True))
    g_prob = ge / jnp.sum(ge, axis=0, keepdims=True)
    g_w = jnp.max(g_prob, axis=0, keepdims=True)
    g_sel = first_at(gvalid & (g_prob == g_w))

    le = lt[(N_GROUPS - 1) * EXPERTS_PER_GROUP:N_GROUPS * EXPERTS_PER_GROUP]
    for g in range(N_GROUPS - 2, -1, -1):
        le = jnp.where(g_sel == g, lt[g * EXPERTS_PER_GROUP:(g + 1) * EXPERTS_PER_GROUP], le)
    ee = jnp.exp(le - jnp.max(le, axis=0, keepdims=True))
    e_prob = ee / jnp.sum(ee, axis=0, keepdims=True)
    p1 = jnp.max(e_prob, axis=0, keepdims=True)
    i1 = first_at(e_prob == p1)
    rest = row != i1
    p2 = jnp.max(jnp.where(rest, e_prob, -1.0), axis=0, keepdims=True)
    i2 = first_at(rest & (e_prob == p2))
    psum = p1 + p2
    w1 = g_w * (p1 / psum)
    w2 = g_w * (p2 / psum)

    def expert_rows(i_sel):
        return jnp.concatenate([jnp.where((g_sel == g) & (row == i_sel), 1.0, 0.0)
                                for g in range(N_GROUPS)], axis=0)

    hit1 = expert_rows(i1)
    hit2 = expert_rows(i2)
    onehot = hit1 + hit2
    before = jnp.dot(onehot, upper_ref[...], preferred_element_type=F32) + carry_s[:, 0:1]
    r1 = jnp.sum(hit1 * before, axis=0, keepdims=True)
    r2 = jnp.sum(hit2 * before, axis=0, keepdims=True)
    carry_s[...] = carry_s[...] + jnp.sum(onehot, axis=1, keepdims=True)

    base = g_sel * EXPERTS_PER_GROUP
    info_t = jnp.where(row == 0, base + i1,
             jnp.where(row == 1, base + i2,
             jnp.where(row == 2, r1,
             jnp.where(row == 3, r2,
             jnp.where(row == 4, w1,
             jnp.where(row == 5, w2, 0.0))))))
    rit_ref[...] = info_t
    ri_ref[...] = jnp.transpose(
        jnp.concatenate([info_t, jnp.zeros((LANES - ROW_TILE, rows), F32)], axis=0))
    cnt_ref[...] = carry_s[...]


def _out_projection(a, cv, w_out_bf, x2, mod3, norm2_g, wr_bf, tri, seq):
    n, d = x2.shape
    tiles_per_batch = seq // OUT_ROWS
    return pl.pallas_call(
        _outproj_kernel,
        grid=(n // OUT_ROWS,),
        in_specs=[
            pl.BlockSpec((OUT_ROWS, ATTN_WIDTH), lambda i: (i, 0)),
            pl.BlockSpec((OUT_ROWS, CONV_WIDTH), lambda i: (i, 0)),
            pl.BlockSpec((d, d), lambda i: (0, 0)),
            pl.BlockSpec((OUT_ROWS, d), lambda i: (i, 0)),
            pl.BlockSpec((1, N_MOD, d), lambda i: (i // tiles_per_batch, 0, 0)),
            pl.BlockSpec((1, d), lambda i: (0, 0)),
            pl.BlockSpec((d, LANES), lambda i: (0, 0)),
            pl.BlockSpec((OUT_ROWS, OUT_ROWS), lambda i: (0, 0)),
        ],
        out_specs=[
            pl.BlockSpec((OUT_ROWS, d), lambda i: (i, 0)),
            pl.BlockSpec((OUT_ROWS * PACK_ROWS, LANES), lambda i: (i, 0)),
            pl.BlockSpec((OUT_ROWS, LANES), lambda i: (i, 0)),
            pl.BlockSpec((ROW_TILE, OUT_ROWS), lambda i: (0, i)),
            pl.BlockSpec((N_EXPERTS, LANES), lambda i: (0, 0)),
        ],
        out_shape=[
            jax.ShapeDtypeStruct((n, d), F32),
            jax.ShapeDtypeStruct((n * PACK_ROWS, LANES), jnp.int32),
            jax.ShapeDtypeStruct((n, LANES), F32),
            jax.ShapeDtypeStruct((ROW_TILE, n), F32),
            jax.ShapeDtypeStruct((N_EXPERTS, LANES), F32),
        ],
        scratch_shapes=[pltpu.VMEM((N_EXPERTS, LANES), F32)],
        compiler_params=_params(("arbitrary",)),
        name="out_projection_routing",
    )(a, cv, w_out_bf, x2, mod3, norm2_g, wr_bf, tri)


def _dispatch_kernel(pend_ref, padded_ref, d0_ref, d1_ref, h_ref, xs_hbm, zero_s, sem):
    tile = MOE_ROWS * PACK_ROWS

    @pl.when(pl.program_id(0) == 0)
    def _():
        zero_s[...] = jnp.zeros_like(zero_s)

        def last_tile(e):
            start = pl.multiple_of((pend_ref[e] - MOE_ROWS) * PACK_ROWS, tile)
            return pltpu.make_async_copy(zero_s, xs_hbm.at[pl.ds(start, tile)], sem)

        def spare_tile(j):
            return pltpu.make_async_copy(zero_s, xs_hbm.at[pl.ds(j * tile, tile)], sem)

        n_tiles = xs_hbm.shape[0] // tile
        used = pend_ref[N_EXPERTS - 1] // MOE_ROWS
        for e in range(N_EXPERTS):
            @pl.when(padded_ref[e] > 0)
            def _():
                last_tile(e).start()
        for j in range(n_tiles - N_EXPERTS, n_tiles):
            @pl.when(j >= used)
            def _():
                spare_tile(j).start()
        for e in range(N_EXPERTS):
            @pl.when(padded_ref[e] > 0)
            def _():
                last_tile(e).wait()
        for j in range(n_tiles - N_EXPERTS, n_tiles):
            @pl.when(j >= used)
            def _():
                spare_tile(j).wait()

    def start(j, carry):
        for u in range(DMA_UNROLL):
            t = j * DMA_UNROLL + u
            for k, d_ref in enumerate((d0_ref, d1_ref)):
                pltpu.make_async_copy(_row_tile(h_ref, t), _row_tile(xs_hbm, d_ref[0, 0, t]),
                                      sem).start(priority=k)
        return carry

    lax.fori_loop(0, DISPATCH_ROWS // DMA_UNROLL, start, 0)

    def drain(j, carry):
        for _ in range(DMA_UNROLL * TOP_K):
            pltpu.make_async_copy(_row_tile(h_ref, 0), _row_tile(xs_hbm, 0), sem).wait()
        return carry

    lax.fori_loop(0, DISPATCH_ROWS // DMA_UNROLL, drain, 0)


def _dispatch(pends, padded, dest0, dest1, h2t, p_rows):
    n = h2t.shape[0] // PACK_ROWS
    steps = n // DISPATCH_ROWS
    smem_rows = lambda: pl.BlockSpec((1, 1, DISPATCH_ROWS), lambda i, pe, pa: (i, 0, 0),
                                     memory_space=pltpu.SMEM)
    return pl.pallas_call(
        _dispatch_kernel,
        grid_spec=pltpu.PrefetchScalarGridSpec(
            num_scalar_prefetch=2,
            grid=(steps,),
            in_specs=[
                smem_rows(),
                smem_rows(),
                pl.BlockSpec((DISPATCH_ROWS * PACK_ROWS, LANES), lambda i, pe, pa: (i, 0)),
            ],
            out_specs=pl.BlockSpec(memory_space=pl.ANY),
            scratch_shapes=[pltpu.VMEM((MOE_ROWS * PACK_ROWS, LANES), jnp.int32),
                            pltpu.SemaphoreType.DMA(())],
        ),
        out_shape=jax.ShapeDtypeStruct((p_rows * PACK_ROWS, LANES), jnp.int32),
        compiler_params=_params(("arbitrary",)),
        name="moe_dispatch",
    )(pends, padded, dest0.reshape(steps, 1, DISPATCH_ROWS), dest1.reshape(steps, 1, DISPATCH_ROWS), h2t)


def _experts_kernel(first_ref, count_ref, xs_hbm, wg_ref, wu_ref, wd_ref, y_hbm,
                    x_s, y_s, wg_s, wu_s, wd_s, in_sem, out_sem):
    e = pl.program_id(0)
    last = pl.num_programs(0) - 1
    n = count_ref[e]
    tile = MOE_ROWS * PACK_ROWS
    ahead = EXPERT_IN_SLOTS - 1

    def rows(first, j):
        return pl.ds(pl.multiple_of((first + j) * tile, tile), tile)

    def fetch(first, j):
        slot = j % EXPERT_IN_SLOTS
        return pltpu.make_async_copy(xs_hbm.at[rows(first, j)], x_s.at[slot], in_sem.at[slot])

    def writeback(first, j):
        slot = j % 2
        return pltpu.make_async_copy(y_s.at[slot], y_hbm.at[rows(first, j)], out_sem.at[slot])

    def start_head(ex):
        for j in range(ahead):
            @pl.when(j < count_ref[ex])
            def _():
                fetch(first_ref[ex], j).start(priority=1)

    @pl.when(n > 0)
    def _():
        first = first_ref[e]

        @pl.when(jnp.logical_or(e == 0, count_ref[jnp.maximum(e - 1, 0)] == 0))
        def _():
            start_head(e)

        wg_s[...] = wg_ref[0].astype(BF16)
        wu_s[...] = wu_ref[0].astype(BF16)
        wd_s[...] = wd_ref[0].astype(BF16)

        def tile_step(j, carry):
            @pl.when(j + ahead < n)
            def _():
                fetch(first, j + ahead).start(priority=1)

            fetch(first, j).wait()

            @pl.when(j >= 2)
            def _():
                writeback(first, j - 2).wait()

            x = _load_row_tiles(x_s, MOE_ROWS, j % EXPERT_IN_SLOTS).astype(BF16)
            g = jnp.dot(x, wg_s[...], preferred_element_type=F32)
            u = jnp.dot(x, wu_s[...], preferred_element_type=F32)
            h = ((g * jax.nn.sigmoid(g)) * u).astype(BF16)
            _store_row_tiles(y_s.at[j % 2], jnp.dot(h, wd_s[...], preferred_element_type=F32))
            writeback(first, j).start(priority=1)
            return carry

        lax.fori_loop(0, n, tile_step, 0)

        @pl.when(e < last)
        def _():
            start_head(jnp.minimum(e + 1, last))

        @pl.when(n >= 2)
        def _():
            writeback(first, n - 2).wait()

        writeback(first, n - 1).wait()


def _experts(first_tile, tile_count, xs, w_gate, w_up, w_down):
    n_exp, d, d_exp = w_gate.shape
    tile = MOE_ROWS * PACK_ROWS
    w_map = lambda i, ft, tc: (i, 0, 0)
    return pl.pallas_call(
        _experts_kernel,
        grid_spec=pltpu.PrefetchScalarGridSpec(
            num_scalar_prefetch=2,
            grid=(n_exp,),
            in_specs=[
                pl.BlockSpec(memory_space=pl.ANY),
                pl.BlockSpec((1, d, d_exp), w_map),
                pl.BlockSpec((1, d, d_exp), w_map),
                pl.BlockSpec((1, d_exp, d), w_map),
            ],
            out_specs=pl.BlockSpec(memory_space=pl.ANY),
            scratch_shapes=[
                pltpu.VMEM((EXPERT_IN_SLOTS, tile, LANES), jnp.int32),
                pltpu.VMEM((2, tile, LANES), jnp.int32),
                pltpu.VMEM((d, d_exp), BF16),
                pltpu.VMEM((d, d_exp), BF16),
                pltpu.VMEM((d_exp, d), BF16),
                pltpu.SemaphoreType.DMA((EXPERT_IN_SLOTS,)),
                pltpu.SemaphoreType.DMA((2,)),
            ],
        ),
        out_shape=jax.ShapeDtypeStruct(xs.shape, xs.dtype),
        input_output_aliases={2: 0},
        compiler_params=_params(("arbitrary",)),
        name="moe_experts",
    )(first_tile, tile_count, xs, w_gate, w_up, w_down)


def _combine_kernel(d0_ref, d1_ref, d0n_ref, d1n_ref, ri_ref, xn_ref, mod_ref, fg_ref, y_hbm,
                    o_ref, buf, sem):
    i = pl.program_id(0)
    slot = i % 2

    def gather(refs, to_slot):
        def start(j, carry):
            for u in range(DMA_UNROLL):
                t = j * DMA_UNROLL + u
                for k, d_ref in enumerate(refs):
                    pltpu.make_async_copy(_row_tile(y_hbm, d_ref[0, 0, t]), _row_tile(buf, t, to_slot, k),
                                          sem.at[to_slot]).start(priority=k)
            return carry

        lax.fori_loop(0, COMBINE_ROWS // DMA_UNROLL, start, 0)

    @pl.when(i == 0)
    def _():
        gather((d0_ref, d1_ref), 0)

    @pl.when(i + 1 < pl.num_programs(0))
    def _():
        gather((d0n_ref, d1n_ref), 1 - slot)

    def drain(j, carry):
        for _ in range(DMA_UNROLL * TOP_K):
            pltpu.make_async_copy(_row_tile(y_hbm, 0), _row_tile(buf, 0, slot, 0), sem.at[slot]).wait()
        return carry

    lax.fori_loop(0, COMBINE_ROWS // DMA_UNROLL, drain, 0)

    ri = ri_ref[...]
    y = (_load_row_tiles(buf, COMBINE_ROWS, slot, 0) * ri[:, 4:5]
         + _load_row_tiles(buf, COMBINE_ROWS, slot, 1) * ri[:, 5:6])
    xf = xn_ref[...] + mod_ref[0][5:6] * y
    o_ref[...] = _rms(xf) * fg_ref[...]


def _combine(dest0, dest1, rinfo, x_new, mod3, final_g, ybuf, seq):
    n, d = x_new.shape
    steps = n // COMBINE_ROWS
    tiles_per_batch = seq // COMBINE_ROWS
    cur = lambda: pl.BlockSpec((1, 1, COMBINE_ROWS), lambda i: (i, 0, 0), memory_space=pltpu.SMEM)
    nxt = lambda: pl.BlockSpec((1, 1, COMBINE_ROWS), lambda i: (jnp.minimum(i + 1, steps - 1), 0, 0),
                               memory_space=pltpu.SMEM)
    d0 = dest0.reshape(steps, 1, COMBINE_ROWS)
    d1 = dest1.reshape(steps, 1, COMBINE_ROWS)
    return pl.pallas_call(
        _combine_kernel,
        grid=(steps,),
        in_specs=[
            cur(), cur(), nxt(), nxt(),
            pl.BlockSpec((COMBINE_ROWS, LANES), lambda i: (i, 0)),
            pl.BlockSpec((COMBINE_ROWS, d), lambda i: (i, 0)),
            pl.BlockSpec((1, N_MOD, d), lambda i: (i // tiles_per_batch, 0, 0)),
            pl.BlockSpec((1, d), lambda i: (0, 0)),
            pl.BlockSpec(memory_space=pl.ANY),
        ],
        out_specs=pl.BlockSpec((COMBINE_ROWS, d), lambda i: (i, 0)),
        out_shape=jax.ShapeDtypeStruct((n, d), F32),
        scratch_shapes=[pltpu.VMEM((2, TOP_K, COMBINE_ROWS * PACK_ROWS, LANES), jnp.int32),
                        pltpu.SemaphoreType.DMA((2,))],
        compiler_params=_params(("arbitrary",)),
        name="moe_combine",
    )(d0, d1, d0, d1, rinfo, x_new, mod3, final_g, ybuf)


def _rope_tables(seq):
    rows = seq // GRID_W
    row_idx = jnp.repeat(jnp.arange(rows, dtype=F32), GRID_W)
    col_idx = jnp.tile(jnp.arange(GRID_W, dtype=F32), rows)
    inv_freq = ROPE_THETA ** (-jnp.arange(0, ROPE_AXIS_DIM, 2, dtype=F32) / ROPE_AXIS_DIM)
    ang = jnp.stack([row_idx[:, None] * inv_freq, col_idx[:, None] * inv_freq], axis=1)
    cos = jnp.cos(ang)
    sin = jnp.sin(ang)
    cos_h = jnp.stack([cos, cos], axis=2).reshape(seq, HEAD_DIM)
    sin_h = jnp.stack([-sin, sin], axis=2).reshape(seq, HEAD_DIM)
    reps = LANES // HEAD_DIM
    return jnp.tile(cos_h, (1, reps)), jnp.tile(sin_h, (1, reps))


def kernel(x, c, ctx, c_ctx, w_mod, b_mod, norm1_g, w_in, q_norm_g, k_norm_g, conv_w, attn_out_g,
           conv_out_g, w_out, norm2_g, w_group, w_router, w_gate, w_up, w_down, final_g):
    assert w_mod.shape[0] == 1, "single-layer block"
    b, s, d = x.shape
    n = b * s
    assert b + 1 <= MOD_ROWS

    cond = jnp.zeros((MOD_ROWS, d), F32).at[:b].set(c).at[b].set(c_ctx)
    mod3 = _modulation(cond, w_mod[0], b_mod[0]).reshape(MOD_ROWS, N_MOD, d)

    cos_t, sin_t = _rope_tables(s)
    head_of = jnp.arange(ATTN_WIDTH) // HEAD_DIM
    bd = jnp.where(head_of[:, None] == head_of[None, :], 1.0 / HEAD_DIM, 0.0).astype(BF16)
    q, k, v, cv = _in_projection(
        x, ctx, mod3, norm1_g, w_in[0].astype(BF16),
        jnp.tile(q_norm_g[0], N_HEADS)[None], jnp.tile(k_norm_g[0], N_KV_HEADS)[None],
        cos_t, sin_t, conv_w[0], conv_out_g, bd)

    a = _attention(q, k, v, attn_out_g)

    wr = jnp.zeros((d, LANES), F32).at[:, :N_EXPERTS].set(w_router[0])
    wr = wr.at[:, N_EXPERTS:N_EXPERTS + N_GROUPS].set(w_group[0]).astype(BF16)
    ti = jnp.arange(OUT_ROWS)
    tri = (ti[:, None] < ti[None, :]).astype(F32)
    x_new, h2t, rinfo, rinfo_t, counts = _out_projection(
        a.reshape(n, ATTN_WIDTH), cv.reshape(n, CONV_WIDTH), w_out[0].astype(BF16),
        x.reshape(n, d), mod3, norm2_g, wr, tri, s)

    cnt = counts[:, 0].astype(jnp.int32)
    padded = ((cnt + MOE_ROWS - 1) // MOE_ROWS) * MOE_ROWS
    pends = jnp.cumsum(padded)
    pstarts = pends - padded
    experts = jnp.arange(N_EXPERTS, dtype=jnp.int32)[:, None]

    def slots(k):
        eid = rinfo_t[k].astype(jnp.int32)
        rank = rinfo_t[TOP_K + k].astype(jnp.int32)
        return jnp.sum(jnp.where(eid[None, :] == experts, pstarts[:, None], 0), axis=0) + rank

    dest0, dest1 = slots(0), slots(1)
    p_rows = n * TOP_K + N_EXPERTS * MOE_ROWS

    xs = _dispatch(pends, padded, dest0, dest1, h2t, p_rows)
    ybuf = _experts(pstarts // MOE_ROWS, padded // MOE_ROWS, xs, w_gate[0], w_up[0], w_down[0])
    out = _combine(dest0, dest1, rinfo, x_new, mod3, final_g.reshape(1, d), ybuf, s)
    return out.reshape(b, s, d)
```
